```python
import math
import jax, jax.numpy as jnp
from jax import lax
import numpy as np

D_MODEL = 1024
BATCH = 8
SEQ = 4096
DEPTH = 4

N_MIXERS = 3
HEAD_DIM = 64
ROPE_THETA = 500000.0
ROPE_DIM = HEAD_DIM // 4

NSA_HEADS = D_MODEL // HEAD_DIM
NSA_KV_HEADS = 4
NSA_GROUP = NSA_HEADS // NSA_KV_HEADS
CMP_LEN = 32
CMP_STRIDE = 16
CMP_HIDDEN = 256
SEL_LEN = 64
SEL_TOPN = 16
NSA_WINDOW = 512
NSA_QBLOCK = 32
FORCE_BONUS = 100.0

GLA_HEADS = 4
GLA_DK = D_MODEL // 2 // GLA_HEADS
GLA_DV = D_MODEL // GLA_HEADS
GLA_GATE_RANK = 16
GLA_TAU = 16.0
GLA_CHUNK = 64

DIL_HEADS = D_MODEL // HEAD_DIM
DIL_CONFIGS = ((128, 1), (512, 4), (2048, 16))
DIL_QBLOCK = 64

N_EXPERTS = 64
TOP_K = 8
N_GROUPS = 8
TOPK_GROUPS = 4
D_EXPERT = 256
ROUTED_SCALE = 2.5
MOE_BLOCK = 128

DN_ALPHA = (2.0 * DEPTH) ** 0.25
DN_BETA = (8.0 * DEPTH) ** -0.25
LN_EPS = 1e-5
NEG_INF = -1e30

kernel_name = 'hybrid_nsa_gla_dilated_moe_deepnorm'


def layer_norm(x, g, b):
    xf = x.astype(jnp.float32)
    mu = xf.mean(-1, keepdims=True)
    var = jnp.square(xf - mu).mean(-1, keepdims=True)
    return ((xf - mu) * lax.rsqrt(var + LN_EPS) * g + b).astype(x.dtype)


def partial_rope(x, pos):
    half = ROPE_DIM // 2
    inv = ROPE_THETA ** (-jnp.arange(half, dtype=jnp.float32) * 2.0 / ROPE_DIM)
    ang = pos.astype(jnp.float32)[:, None] * inv[None, :]
    cos = jnp.cos(ang)[None, :, None, :]
    sin = jnp.sin(ang)[None, :, None, :]
    xr = x[..., :ROPE_DIM].astype(jnp.float32)
    x1, x2 = xr[..., :half], xr[..., half:]
    rot = jnp.concatenate([x1 * cos - x2 * sin, x2 * cos + x1 * sin], -1).astype(x.dtype)
    return jnp.concatenate([rot, x[..., ROPE_DIM:]], -1)


def masked_softmax(s, mask, axes):
    s = jnp.where(mask, s.astype(jnp.float32), NEG_INF)
    m = jnp.max(s, axis=axes, keepdims=True)
    p = jnp.exp(s - m) * mask
    den = jnp.sum(p, axis=axes, keepdims=True)
    return p / jnp.maximum(den, 1e-30), m, den


def swiglu(x, w_in, w_out):
    g, u = jnp.split(x @ w_in, 2, axis=-1)
    return (jax.nn.silu(g) * u) @ w_out


def nsa_mixer(x, w_in, w_out, ck_pos, ck_w1, ck_w2, cv_pos, cv_w1, cv_w2):
    B, S, _ = x.shape
    H, KH, G, dh = NSA_HEADS, NSA_KV_HEADS, NSA_GROUP, HEAD_DIM
    kvw = KH * dh
    pos = jnp.arange(S)
    cuts = np.cumsum([H * dh, kvw, kvw, kvw, kvw, kvw, kvw]).tolist()
    q, kc, vc, ks, vs, kw, vw, gl = jnp.split(x @ w_in, cuts, axis=-1)
    q = partial_rope(q.reshape(B, S, H, dh), pos) * dh ** -0.5
    kc = partial_rope(kc.reshape(B, S, KH, dh), pos)
    ks = partial_rope(ks.reshape(B, S, KH, dh), pos)
    kw = partial_rope(kw.reshape(B, S, KH, dh), pos)
    vc, vs, vw = (t.reshape(B, S, KH, dh) for t in (vc, vs, vw))
    gates = jax.nn.sigmoid(gl.astype(jnp.float32)).reshape(B, S, H, 3)

    n_cmp = (S - CMP_LEN) // CMP_STRIDE + 1
    cidx = jnp.arange(n_cmp)[:, None] * CMP_STRIDE + jnp.arange(CMP_LEN)[None, :]

    def compress(t, pos_emb, w1, w2):
        blk = t[:, cidx] + pos_emb[None, None, :, None, :]
        blk = blk.transpose(0, 1, 3, 2, 4).reshape(B, n_cmp, KH, CMP_LEN * dh)
        return jax.nn.gelu(blk @ w1) @ w2

    k_cmp = compress(kc, ck_pos, ck_w1, ck_w2)
    v_cmp = compress(vc, cv_pos, cv_w1, cv_w2)
    cmp_start, cmp_end = cidx[:, 0], cidx[:, -1]

    n_sel = S // SEL_LEN
    top_n = min(SEL_TOPN, n_sel)
    sel_start = jnp.arange(n_sel) * SEL_LEN
    overlap = ((cmp_start[:, None] < sel_start[None, :] + SEL_LEN) &
               (cmp_end[:, None] >= sel_start[None, :])).astype(jnp.float32)
    ks_blk = ks.reshape(B, n_sel, SEL_LEN, KH, dh).transpose(0, 3, 1, 2, 4)
    vs_blk = vs.reshape(B, n_sel, SEL_LEN, KH, dh).transpose(0, 3, 1, 2, 4)
    kw_pad = jnp.pad(kw, ((0, 0), (NSA_WINDOW, 0), (0, 0), (0, 0)))
    vw_pad = jnp.pad(vw, ((0, 0), (NSA_WINDOW, 0), (0, 0), (0, 0)))
    qg = q.reshape(B, S, KH, G, dh)
    bi = jnp.arange(B)[:, None, None, None]
    hi = jnp.arange(KH)[None, None, :, None]
    blk_id = jnp.arange(n_sel)
    Q = NSA_QBLOCK

    def block(q0):
        t = q0 + jnp.arange(Q)
        qb = lax.dynamic_slice_in_dim(qg, q0, Q, axis=1)
        s_c = jnp.einsum('bqkgd,bckd->bqkgc', qb, k_cmp)
        m_c = (cmp_end[None, :] <= t[:, None])[None, :, None, None, :]
        p_c, _, _ = masked_softmax(s_c, m_c, -1)
        o_c = jnp.einsum('bqkgc,bckd->bqkgd', p_c.astype(v_cmp.dtype), v_cmp)
        imp = jnp.einsum('bqkgc,cn->bqkn', p_c, overlap)
        cur = t // SEL_LEN
        allowed = (blk_id[None, :] <= cur[:, None])[None, :, None, :]
        forced = ((blk_id[None, :] == 0) | (blk_id[None, :] == cur[:, None]) |
                  (blk_id[None, :] == cur[:, None] - 1))[None, :, None, :]
        score = jnp.where(allowed, imp + FORCE_BONUS * forced, -1.0)
        _, sel = lax.top_k(score, top_n)
        k_sel = ks_blk[bi, hi, sel]
        v_sel = vs_blk[bi, hi, sel]
        s_s = jnp.einsum('bqkgd,bqknld->bqkgnl', qb, k_sel)
        tok = sel[..., None] * SEL_LEN + jnp.arange(SEL_LEN)
        m_s = (tok <= t[None, :, None, None, None])[:, :, :, None]
        p_s, _, _ = masked_softmax(s_s, m_s, (-2, -1))
        o_s = jnp.einsum('bqkgnl,bqknld->bqkgd', p_s.astype(v_sel.dtype), v_sel)
        kwb = lax.dynamic_slice_in_dim(kw_pad, q0, Q + NSA_WINDOW, axis=1)
        vwb = lax.dynamic_slice_in_dim(vw_pad, q0, Q + NSA_WINDOW, axis=1)
        kpos = q0 - NSA_WINDOW + jnp.arange(Q + NSA_WINDOW)
        dist = t[:, None] - kpos[None, :]
        m_w = ((dist >= 0) & (dist < NSA_WINDOW) & (kpos[None, :] >= 0))[None, :, None, None, :]
        s_w = jnp.einsum('bqkgd,bjkd->bqkgj', qb, kwb)
        p_w, _, _ = masked_softmax(s_w, m_w, -1)
        o_w = jnp.einsum('bqkgj,bjkd->bqkgd', p_w.astype(vwb.dtype), vwb)
        g = lax.dynamic_slice_in_dim(gates, q0, Q, axis=1).reshape(B, Q, KH, G, 3)
        o = g[..., 0:1] * o_c + g[..., 1:2] * o_s + g[..., 2:3] * o_w
        return o.reshape(B, Q, H * dh).astype(x.dtype)

    outs = lax.map(block, jnp.arange(S // Q) * Q)
    o = outs.transpose(1, 0, 2, 3).reshape(B, S, H * dh)
    return o @ w_out


def gla_mixer(x, w_in, w_a2, b_a, norm_g, w_out):
    B, S, _ = x.shape
    H, dk, dv, C = GLA_HEADS, GLA_DK, GLA_DV, GLA_CHUNK
    cuts = [H * dk, 2 * H * dk, 2 * H * dk + H * dv, 2 * H * dk + H * dv + GLA_GATE_RANK]
    q, k, v, a_lr, r = jnp.split(x @ w_in, cuts, axis=-1)
    log_a = jax.nn.log_sigmoid((a_lr @ w_a2 + b_a).astype(jnp.float32)) / GLA_TAU
    N = S // C
    q = q.astype(jnp.float32).reshape(B, N, C, H, dk) * dk ** -0.5
    k = k.astype(jnp.float32).reshape(B, N, C, H, dk)
    v = v.astype(jnp.float32).reshape(B, N, C, H, dv)
    b = jnp.cumsum(log_a.reshape(B, N, C, H, dk), axis=2)
    b_last = b[:, :, -1]
    q_g = q * jnp.exp(b)
    k_g = k * jnp.exp(-b)
    k_d = k * jnp.exp(b_last[:, :, None] - b)
    causal = jnp.tril(jnp.ones((C, C), bool))
    A = jnp.where(causal, jnp.einsum('bnihd,bnjhd->bnhij', q_g, k_g), 0.0)
    o_intra = jnp.einsum('bnhij,bnjhe->bnihe', A, v)
    dS = jnp.einsum('bnchd,bnche->nbhde', k_d, v)
    decay = jnp.exp(b_last).transpose(1, 0, 2, 3)

    def step(state, inp):
        dec, ds = inp
        return dec[..., None] * state + ds, state

    _, s_prev = lax.scan(step, jnp.zeros((B, H, dk, dv), jnp.float32), (decay, dS))
    o_inter = jnp.einsum('bnihd,nbhde->bnihe', q_g, s_prev)
    o = (o_intra + o_inter).reshape(B, S, H, dv)
    o = o * lax.rsqrt(jnp.mean(o * o, -1, keepdims=True) + LN_EPS) * norm_g
    o = o.reshape(B, S, H * dv) * jax.nn.silu(r.astype(jnp.float32))
    return o.astype(x.dtype) @ w_out


def dilated_attention(q, k, v, window, dil):
    B, S, H, dh = q.shape
    L, W = S // dil, window // dil
    bq = math.gcd(L, DIL_QBLOCK)
    qr = q.reshape(B, L, dil, H, dh)
    pad = ((0, 0), (W, 0), (0, 0), (0, 0), (0, 0))
    kp = jnp.pad(k.reshape(B, L, dil, H, dh), pad)
    vp = jnp.pad(v.reshape(B, L, dil, H, dh), pad)

    def block(u0):
        u = u0 + jnp.arange(bq)
        qb = lax.dynamic_slice_in_dim(qr, u0, bq, axis=1)
        kb = lax.dynamic_slice_in_dim(kp, u0, bq + W, axis=1)
        vb = lax.dynamic_slice_in_dim(vp, u0, bq + W, axis=1)
        uk = u0 - W + jnp.arange(bq + W)
        dist = u[:, None] - uk[None, :]
        mask = ((dist >= 0) & (dist <= W) & (uk[None, :] >= 0))[None, None, None]
        s = jnp.einsum('bqrhd,bjrhd->brhqj', qb, kb)
        p, m, den = masked_softmax(s, mask, -1)
        o = jnp.einsum('brhqj,bjrhd->bqrhd', p.astype(vb.dtype), vb)
        lse = (m + jnp.log(den))[..., 0]
        return o, lse.transpose(0, 3, 1, 2)

    o, lse = lax.map(block, jnp.arange(L // bq) * bq)
    o = o.transpose(1, 0, 2, 3, 4, 5).reshape(B, S, H, dh)
    lse = lse.transpose(1, 0, 2, 3, 4).reshape(B, S, H)
    return o, lse


def dilated_mixer(x, w_in, w_out):
    B, S, _ = x.shape
    H, dh = DIL_HEADS, HEAD_DIM
    pos = jnp.arange(S)
    proj = (x @ w_in).reshape(B, S, len(DIL_CONFIGS), 3, H, dh)
    outs, lses = [], []
    for gi, (window, dil) in enumerate(DIL_CONFIGS):
        q = partial_rope(proj[:, :, gi, 0], pos) * dh ** -0.5
        k = partial_rope(proj[:, :, gi, 1], pos)
        o, lse = dilated_attention(q, k, proj[:, :, gi, 2], window, dil)
        outs.append(o.astype(jnp.float32))
        lses.append(lse)
    wts = jax.nn.softmax(jnp.stack(lses, 0), axis=0)
    o = jnp.sum(wts[..., None] * jnp.stack(outs, 0), axis=0)
    return o.reshape(B, S, H * dh).astype(x.dtype) @ w_out


def moe_ffn(x, router_w, router_b, w_in_e, w_out_e, w_in_s, w_out_s):
    B, S, D = x.shape
    N = B * S
    xf = x.reshape(N, D)
    s = jax.nn.sigmoid((xf @ router_w).astype(jnp.float32))
    sb = s + router_b.astype(jnp.float32)
    g_score = lax.top_k(sb.reshape(N, N_GROUPS, -1), 2)[0].sum(-1)
    _, g_idx = lax.top_k(g_score, TOPK_GROUPS)
    g_mask = jnp.any(g_idx[:, :, None] == jnp.arange(N_GROUPS)[None, None, :], axis=1)
    e_mask = jnp.repeat(g_mask, N_EXPERTS // N_GROUPS, axis=1)
    _, e_idx = lax.top_k(jnp.where(e_mask, sb, NEG_INF), TOP_K)
    gate = jnp.take_along_axis(s, e_idx, axis=1)
    gate = gate / jnp.sum(gate, -1, keepdims=True) * ROUTED_SCALE
    NK = N * TOP_K
    flat_e = e_idx.reshape(NK)
    order = jnp.argsort(flat_e)
    e_sorted = flat_e[order]
    counts = jnp.bincount(flat_e, length=N_EXPERTS)
    start = jnp.cumsum(counts) - counts
    padded = (counts + MOE_BLOCK - 1) // MOE_BLOCK * MOE_BLOCK
    pad_end = jnp.cumsum(padded)
    dest = (pad_end - padded)[e_sorted] + jnp.arange(NK) - start[e_sorted]
    n_blk = -(-NK // MOE_BLOCK) + N_EXPERTS
    P = n_blk * MOE_BLOCK
    row_tok = jnp.full((P,), N, jnp.int32).at[dest].set((order // TOP_K).astype(jnp.int32))
    row_gate = jnp.zeros((P,), jnp.float32).at[dest].set(gate.reshape(NK)[order])
    blk_e = jnp.minimum(jnp.searchsorted(pad_end, jnp.arange(n_blk) * MOE_BLOCK, side='right'), N_EXPERTS - 1)
    x_pad = jnp.concatenate([xf, jnp.zeros((1, D), xf.dtype)], 0)

    def expert_block(args):
        tok, e = args
        return swiglu(x_pad[tok], w_in_e[e], w_out_e[e])

    y = lax.map(expert_block, (row_tok.reshape(n_blk, MOE_BLOCK), blk_e))
    y = y.reshape(P, D).astype(jnp.float32) * row_gate[:, None]
    routed = jax.ops.segment_sum(y, row_tok, num_segments=N + 1)[:N]
    out = routed + swiglu(xf, w_in_s, w_out_s).astype(jnp.float32)
    return out.reshape(B, S, D).astype(x.dtype)


def setup_inputs(seed: int = 0) -> dict:
    key = jax.random.key(seed)
    keys = iter(jax.random.split(key, 128))
    D, F, E = D_MODEL, D_EXPERT, N_EXPERTS

    def nrm(shape, scale):
        return jax.random.normal(next(keys), shape, jnp.float32) * scale

    def gain(n):
        return 1.0 + 0.02 * jax.random.normal(next(keys), (n,), jnp.float32)

    inp = {'x': nrm((BATCH, SEQ, D), 1.0)}
    for i in range(DEPTH):
        p = 'l%d_' % i
        kind = i % N_MIXERS
        if kind == 0:
            width = NSA_HEADS * HEAD_DIM
            n_in = width + 6 * NSA_KV_HEADS * HEAD_DIM + 3 * NSA_HEADS
            inp[p + 'nsa_w_in'] = nrm((D, n_in), D ** -0.5)
            inp[p + 'nsa_w_out'] = nrm((width, D), DN_BETA * width ** -0.5)
            for t in ('ck', 'cv'):
                inp[p + 'nsa_' + t + '_pos'] = nrm((CMP_LEN, HEAD_DIM), 0.1)
                inp[p + 'nsa_' + t + '_w1'] = nrm((CMP_LEN * HEAD_DIM, CMP_HIDDEN), (CMP_LEN * HEAD_DIM) ** -0.5)
                inp[p + 'nsa_' + t + '_w2'] = nrm((CMP_HIDDEN, HEAD_DIM), CMP_HIDDEN ** -0.5)
        elif kind == 1:
            hk, hv = GLA_HEADS * GLA_DK, GLA_HEADS * GLA_DV
            inp[p + 'gla_w_in'] = nrm((D, 2 * hk + hv + GLA_GATE_RANK + D), D ** -0.5)
            inp[p + 'gla_w_a2'] = nrm((GLA_GATE_RANK, hk), GLA_GATE_RANK ** -0.5)
            inp[p + 'gla_b_a'] = nrm((hk,), 0.1)
            inp[p + 'gla_norm_g'] = gain(GLA_DV)
            inp[p + 'gla_w_out'] = nrm((hv, D), DN_BETA * hv ** -0.5)
        else:
            width = DIL_HEADS * HEAD_DIM
            inp[p + 'dil_w_in'] = nrm((D, 3 * len(DIL_CONFIGS) * width), D ** -0.5)
            inp[p + 'dil_w_out'] = nrm((width, D), DN_BETA * width ** -0.5)
        inp[p + 'ln1_g'] = gain(D)
        inp[p + 'ln1_b'] = nrm((D,), 0.02)
        inp[p + 'router_w'] = nrm((D, E), D ** -0.5)
        inp[p + 'router_b'] = nrm((E,), 0.01)
        inp[p + 'moe_w_in'] = nrm((E, D, 2 * F), D ** -0.5)
        inp[p + 'moe_w_out'] = nrm((E, F, D), DN_BETA * F ** -0.5)
        inp[p + 'shared_w_in'] = nrm((D, 2 * F), D ** -0.5)
        inp[p + 'shared_w_out'] = nrm((F, D), DN_BETA * F ** -0.5)
        inp[p + 'ln2_g'] = gain(D)
        inp[p + 'ln2_b'] = nrm((D,), 0.02)
    return inp


def reference(x,
              l0_nsa_w_in, l0_nsa_w_out, l0_nsa_ck_pos, l0_nsa_ck_w1, l0_nsa_ck_w2, l0_nsa_cv_pos, l0_nsa_cv_w1, l0_nsa_cv_w2,
              l0_ln1_g, l0_ln1_b, l0_router_w, l0_router_b, l0_moe_w_in, l0_moe_w_out, l0_shared_w_in, l0_shared_w_out, l0_ln2_g, l0_ln2_b,
              l1_gla_w_in, l1_gla_w_a2, l1_gla_b_a, l1_gla_norm_g, l1_gla_w_out,
              l1_ln1_g, l1_ln1_b, l1_router_w, l1_router_b, l1_moe_w_in, l1_moe_w_out, l1_shared_w_in, l1_shared_w_out, l1_ln2_g, l1_ln2_b,
              l2_dil_w_in, l2_dil_w_out,
              l2_ln1_g, l2_ln1_b, l2_router_w, l2_router_b, l2_moe_w_in, l2_moe_w_out, l2_shared_w_in, l2_shared_w_out, l2_ln2_g, l2_ln2_b,
              l3_nsa_w_in, l3_nsa_w_out, l3_nsa_ck_pos, l3_nsa_ck_w1, l3_nsa_ck_w2, l3_nsa_cv_pos, l3_nsa_cv_w1, l3_nsa_cv_w2,
              l3_ln1_g, l3_ln1_b, l3_router_w, l3_router_b, l3_moe_w_in, l3_moe_w_out, l3_shared_w_in, l3_shared_w_out, l3_ln2_g, l3_ln2_b):
    mixer_params = [
        (l0_nsa_w_in, l0_nsa_w_out, l0_nsa_ck_pos, l0_nsa_ck_w1, l0_nsa_ck_w2, l0_nsa_cv_pos, l0_nsa_cv_w1, l0_nsa_cv_w2),
        (l1_gla_w_in, l1_gla_w_a2, l1_gla_b_a, l1_gla_norm_g, l1_gla_w_out),
        (l2_dil_w_in, l2_dil_w_out),
        (l3_nsa_w_in, l3_nsa_w_out, l3_nsa_ck_pos, l3_nsa_ck_w1, l3_nsa_ck_w2, l3_nsa_cv_pos, l3_nsa_cv_w1, l3_nsa_cv_w2),
    ]
    ffn_params = [
        (l0_ln1_g, l0_ln1_b, l0_router_w, l0_router_b, l0_moe_w_in, l0_moe_w_out, l0_shared_w_in, l0_shared_w_out, l0_ln2_g, l0_ln2_b),
        (l1_ln1_g, l1_ln1_b, l1_router_w, l1_router_b, l1_moe_w_in, l1_moe_w_out, l1_shared_w_in, l1_shared_w_out, l1_ln2_g, l1_ln2_b),
        (l2_ln1_g, l2_ln1_b, l2_router_w, l2_router_b, l2_moe_w_in, l2_moe_w_out, l2_shared_w_in, l2_shared_w_out, l2_ln2_g, l2_ln2_b),
        (l3_ln1_g, l3_ln1_b, l3_router_w, l3_router_b, l3_moe_w_in, l3_moe_w_out, l3_shared_w_in, l3_shared_w_out, l3_ln2_g, l3_ln2_b),
    ]
    mixer_fns = (nsa_mixer, gla_mixer, dilated_mixer)
    for i in range(DEPTH):
        h = mixer_fns[i % N_MIXERS](x, *mixer_params[i])
        ln1_g, ln1_b, rw, rb, ewi, ewo, swi, swo, ln2_g, ln2_b = ffn_params[i]
        x = layer_norm(DN_ALPHA * x + h, ln1_g, ln1_b)
        x = layer_norm(DN_ALPHA * x + moe_ffn(x, rw, rb, ewi, ewo, swi, swo), ln2_g, ln2_b)
    return x
```

```python
import functools
import math

import jax
import jax.numpy as jnp
import numpy as np
from jax import lax
from jax.experimental import pallas as pl
from jax.experimental.pallas import tpu as pltpu

F32 = jnp.float32
BF16 = jnp.bfloat16

D_MODEL = 1024
DEPTH = 4
HEAD_DIM = 64
ROPE_THETA = 500000.0
ROPE_DIM = HEAD_DIM // 4
ROPE_HALF = ROPE_DIM // 2

N_EXPERTS = 64
TOP_K = 8
N_GROUPS = 8
TOPK_GROUPS = 4
GROUP_SIZE = N_EXPERTS // N_GROUPS
D_EXPERT = 256
ROUTED_SCALE = 2.5

DN_ALPHA = (2.0 * DEPTH) ** 0.25
LN_EPS = 1e-5
NEG_INF = -1e30

LANES = 128
VMEM_LIMIT = 48 * 1024 * 1024


def _cparams(sem, **kw):
    return pltpu.CompilerParams(dimension_semantics=sem, vmem_limit_bytes=VMEM_LIMIT, **kw)


def _pick(n, pref):
    t = min(pref, n)
    while n % t:
        t //= 2
    return t


def _mm_kernel(x_ref, w_ref, o_ref):
    o_ref[...] = jnp.dot(x_ref[...], w_ref[...], preferred_element_type=F32).astype(o_ref.dtype)


def _mm_rope_kernel(x_ref, w_ref, c_ref, sm_ref, sp_ref, o_ref):
    y = jnp.dot(x_ref[...], w_ref[...], preferred_element_type=F32)
    reps = y.shape[1] // LANES
    c = jnp.tile(c_ref[...], (1, reps))
    sm = jnp.tile(sm_ref[...], (1, reps))
    sp = jnp.tile(sp_ref[...], (1, reps))
    up = pltpu.roll(y, y.shape[1] - ROPE_HALF, 1)
    dn = pltpu.roll(y, ROPE_HALF, 1)
    o_ref[...] = (y * c + up * sm + dn * sp).astype(o_ref.dtype)


def matmul(x, w, out_dtype=F32, rope=None, tm=512, tn=512):
    M, K = x.shape
    N = w.shape[1]
    tm = _pick(M, tm)
    tn = _pick(N, tn)
    grid = (N // tn, M // tm)
    x_spec = pl.BlockSpec((tm, K), lambda j, i: (i, 0))
    w_spec = pl.BlockSpec((K, tn), lambda j, i: (0, j))
    o_spec = pl.BlockSpec((tm, tn), lambda j, i: (i, j))
    if rope is None:
        return pl.pallas_call(
            _mm_kernel, out_shape=jax.ShapeDtypeStruct((M, N), out_dtype), grid=grid,
            in_specs=[x_spec, w_spec], out_specs=o_spec,
            compiler_params=_cparams(("parallel", "parallel")), name="mm")(x, w)
    R = rope[0].shape[0]
    tm = _pick(R, tm)
    grid = (N // tn, M // tm)
    x_spec = pl.BlockSpec((tm, K), lambda j, i: (i, 0))
    o_spec = pl.BlockSpec((tm, tn), lambda j, i: (i, j))
    nr = R // tm
    t_spec = pl.BlockSpec((tm, LANES), lambda j, i: (i % nr, 0))
    return pl.pallas_call(
        _mm_rope_kernel, out_shape=jax.ShapeDtypeStruct((M, N), out_dtype), grid=grid,
        in_specs=[x_spec, w_spec, t_spec, t_spec, t_spec], out_specs=o_spec,
        compiler_params=_cparams(("parallel", "parallel")), name="mm_rope")(x, w, *rope)


def rope_tables(pos):
    inv = ROPE_THETA ** (-jnp.arange(ROPE_HALF, dtype=F32) * 2.0 / ROPE_DIM)
    ang = pos.astype(F32)[:, None] * inv[None, :]
    cos, sin = jnp.cos(ang), jnp.sin(ang)
    n = pos.shape[0]
    ones = jnp.ones((n, HEAD_DIM - ROPE_DIM), F32)
    zeros = jnp.zeros((n, HEAD_DIM - ROPE_DIM), F32)
    zh = jnp.zeros((n, ROPE_HALF), F32)
    c = jnp.concatenate([cos, cos, ones], 1)
    sm = jnp.concatenate([-sin, zh, zeros], 1)
    sp = jnp.concatenate([zh, sin, zeros], 1)
    return tuple(jnp.tile(t, (1, LANES // HEAD_DIM)) for t in (c, sm, sp))


def _ln(v, g, b):
    mu = jnp.mean(v, axis=-1, keepdims=True)
    d = v - mu
    var = jnp.mean(d * d, axis=-1, keepdims=True)
    return d * lax.rsqrt(var + LN_EPS) * g + b


def _mm_res_ln_kernel(a_ref, w_ref, x_ref, g_ref, b_ref, o_ref, ob_ref):
    h = jnp.dot(a_ref[...], w_ref[...], preferred_element_type=F32)
    y = _ln(DN_ALPHA * x_ref[...] + h, g_ref[...], b_ref[...])
    o_ref[...] = y
    ob_ref[...] = y.astype(BF16)


def matmul_res_ln(a, w, x, g, b, tm=256):
    M, K = a.shape
    D = w.shape[1]
    tm = _pick(M, tm)
    row = lambda i: (i, 0)
    fix = lambda i: (0, 0)
    return pl.pallas_call(
        _mm_res_ln_kernel,
        out_shape=(jax.ShapeDtypeStruct((M, D), F32), jax.ShapeDtypeStruct((M, D), BF16)),
        grid=(M // tm,),
        in_specs=[pl.BlockSpec((tm, K), row), pl.BlockSpec((K, D), fix), pl.BlockSpec((tm, D), row),
                  pl.BlockSpec((1, D), fix), pl.BlockSpec((1, D), fix)],
        out_specs=(pl.BlockSpec((tm, D), row), pl.BlockSpec((tm, D), row)),
        compiler_params=_cparams(("parallel",)), name="mm_res_ln")(a, w, x, g.reshape(1, D), b.reshape(1, D))


def _first_index_of_max(v, iota, axis, n):
    m = jnp.max(v, axis=axis, keepdims=True)
    idx = jnp.min(jnp.where(v == m, iota, n), axis=axis, keepdims=True)
    return m, idx


def _router_kernel(x_ref, wh_ref, wl_ref, rb_ref, tri_ref, eidx_ref, gate_ref, rank_ref, cnt_ref, carry_ref):
    i = pl.program_id(0)

    @pl.when(i == 0)
    def _():
        carry_ref[...] = jnp.zeros_like(carry_ref)

    x = x_ref[...]
    xh = x.astype(BF16)
    xl = (x - xh.astype(F32)).astype(BF16)
    dn = (((1,), (1,)), ((), ()))
    logits = (lax.dot_general(wh_ref[...], xh, dn, preferred_element_type=F32)
              + lax.dot_general(wh_ref[...], xl, dn, preferred_element_type=F32)
              + lax.dot_general(wl_ref[...], xh, dn, preferred_element_type=F32))
    tm = logits.shape[1]
    s = jax.nn.sigmoid(logits)
    sb = s + rb_ref[...]
    sb3 = sb.reshape(N_GROUPS, GROUP_SIZE, tm)
    io3 = lax.broadcasted_iota(jnp.int32, sb3.shape, 1)
    m1, i1 = _first_index_of_max(sb3, io3, 1, GROUP_SIZE)
    m2 = jnp.max(jnp.where(io3 == i1, -jnp.inf, sb3), axis=1, keepdims=True)
    gs = (m1 + m2).reshape(N_GROUPS, tm)
    iog = lax.broadcasted_iota(jnp.int32, gs.shape, 0)
    gmask = jnp.zeros(gs.shape, jnp.bool_)
    for _ in range(TOPK_GROUPS):
        _, gi = _first_index_of_max(gs, iog, 0, N_GROUPS)
        pick = iog == gi
        gmask = gmask | pick
        gs = jnp.where(pick, -jnp.inf, gs)
    emask = jnp.broadcast_to(gmask.reshape(N_GROUPS, 1, tm), sb3.shape).reshape(N_EXPERTS, tm)
    cand = jnp.where(emask, sb, NEG_INF)
    ioe = lax.broadcasted_iota(jnp.int32, cand.shape, 0)
    sel = jnp.zeros(cand.shape, F32)
    picks, eidx, gates = [], [], []
    for _ in range(TOP_K):
        _, ei = _first_index_of_max(cand, ioe, 0, N_EXPERTS)
        pick = ioe == ei
        picks.append(pick)
        eidx.append(ei)
        gates.append(jnp.sum(jnp.where(pick, s, 0.0), axis=0, keepdims=True))
        sel = jnp.where(pick, 1.0, sel)
        cand = jnp.where(pick, -jnp.inf, cand)
    g = jnp.concatenate(gates, axis=0)
    gate_ref[...] = g / jnp.sum(g, axis=0, keepdims=True) * ROUTED_SCALE
    eidx_ref[...] = jnp.concatenate(eidx, axis=0)
    before = jnp.dot(sel.astype(BF16), tri_ref[...], preferred_element_type=F32) + carry_ref[...]
    rank_ref[...] = jnp.concatenate(
        [jnp.sum(jnp.where(p, before, 0.0), axis=0, keepdims=True) for p in picks], axis=0).astype(jnp.int32)
    carry_ref[...] = carry_ref[...] + jnp.sum(sel, axis=1, keepdims=True)
    cnt_ref[...] = jnp.broadcast_to(carry_ref[...], cnt_ref.shape).astype(jnp.int32)


def moe_router(x, router_w, router_b, tm=512):
    N, D = x.shape
    tm = _pick(N, tm)
    wt = router_w.T
    wh = wt.astype(BF16)
    wl = (wt - wh.astype(F32)).astype(BF16)
    tri = (jnp.arange(tm)[:, None] < jnp.arange(tm)[None, :]).astype(BF16)
    fix = lambda i: (0, 0)
    col = lambda i: (0, i)
    eidx, gate, rank, cnt = pl.pallas_call(
        _router_kernel,
        out_shape=(jax.ShapeDtypeStruct((TOP_K, N), jnp.int32), jax.ShapeDtypeStruct((TOP_K, N), F32),
                   jax.ShapeDtypeStruct((TOP_K, N), jnp.int32), jax.ShapeDtypeStruct((N_EXPERTS, LANES), jnp.int32)),
        grid=(N // tm,),
        in_specs=[pl.BlockSpec((tm, D), lambda i: (i, 0)), pl.BlockSpec((N_EXPERTS, D), fix),
                  pl.BlockSpec((N_EXPERTS, D), fix), pl.BlockSpec((N_EXPERTS, 1), fix), pl.BlockSpec((tm, tm), fix)],
        out_specs=(pl.BlockSpec((TOP_K, tm), col), pl.BlockSpec((TOP_K, tm), col), pl.BlockSpec((TOP_K, tm), col),
                   pl.BlockSpec((N_EXPERTS, LANES), fix)),
        scratch_shapes=[pltpu.VMEM((N_EXPERTS, 1), F32)],
        compiler_params=_cparams(("arbitrary",)), name="moe_router",
    )(x, wh, wl, router_b.reshape(N_EXPERTS, 1).astype(F32), tri)
    return eidx, gate, rank, cnt[:, 0]


def _dispatch_kernel(dest_ref, x_ref, init_ref, xs_ref, sem):
    del init_ref
    tm = x_ref.shape[0]

    def body(t, c):
        for k in range(TOP_K):
            pltpu.make_async_copy(x_ref.at[pl.ds(t, 1)], xs_ref.at[pl.ds(dest_ref[k, t], 1)], sem).start()
        return c

    lax.fori_loop(0, tm, body, 0)

    def wbody(t, c):
        for k in range(TOP_K):
            pltpu.make_async_copy(x_ref.at[pl.ds(t, 1)], xs_ref.at[pl.ds(dest_ref[k, t], 1)], sem).wait()
        return c

    lax.fori_loop(0, tm, wbody, 0)


def moe_dispatch(x, dest, P, tm=512):
    N, D = x.shape
    tm = _pick(N, tm)
    init = jnp.zeros((P, D), x.dtype)
    return pl.pallas_call(
        _dispatch_kernel, out_shape=jax.ShapeDtypeStruct((P, D), x.dtype), grid=(N // tm,),
        in_specs=[pl.BlockSpec((TOP_K, tm), lambda i: (0, i), memory_space=pltpu.SMEM),
                  pl.BlockSpec((tm, D), lambda i: (i, 0)),
                  pl.BlockSpec(memory_space=pl.ANY)],
        out_specs=pl.BlockSpec(memory_space=pl.ANY),
        scratch_shapes=[pltpu.SemaphoreType.DMA(())],
        input_output_aliases={2: 0},
        compiler_params=_cparams(("arbitrary",), has_side_effects=True), name="moe_dispatch",
    )(dest, x, init)


def _swiglu(x_bf16, w_in, w_out):
    h = jnp.dot(x_bf16, w_in, preferred_element_type=F32)
    f = h.shape[1] // 2
    a = jax.nn.silu(h[:, :f]) * h[:, f:]
    return jnp.dot(a.astype(BF16), w_out, preferred_element_type=F32)


def _expert_kernel(blk_e_ref, xs_ref, wi_ref, wo_ref, y_ref):
    del blk_e_ref
    y_ref[...] = _swiglu(xs_ref[...].astype(BF16), wi_ref[0], wo_ref[0])


def moe_experts(xs, blk_e, w_in_e, w_out_e, tm):
    P, D = xs.shape
    F2 = w_in_e.shape[2]
    grid_spec = pltpu.PrefetchScalarGridSpec(
        num_scalar_prefetch=1, grid=(P // tm,),
        in_specs=[pl.BlockSpec((tm, D), lambda i, be: (i, 0)),
                  pl.BlockSpec((1, D, F2), lambda i, be: (be[i], 0, 0)),
                  pl.BlockSpec((1, F2 // 2, D), lambda i, be: (be[i], 0, 0))],
        out_specs=pl.BlockSpec((tm, D), lambda i, be: (i, 0)))
    return pl.pallas_call(
        _expert_kernel, out_shape=jax.ShapeDtypeStruct((P, D), F32), grid_spec=grid_spec,
        compiler_params=_cparams(("arbitrary",)), name="moe_experts")(blk_e, xs, w_in_e, w_out_e)


def _combine_kernel(dest_ref, x_ref, gate_ref, wi_ref, wo_ref, g_ref, b_ref, y_ref, o_ref, ob_ref, buf, sem):
    tm = x_ref.shape[0]

    def copy(t, k):
        return pltpu.make_async_copy(y_ref.at[pl.ds(dest_ref[k, t], 1)], buf.at[k, pl.ds(t, 1)], sem)

    def body(t, c):
        for k in range(TOP_K):
            copy(t, k).start()
        return c

    lax.fori_loop(0, tm, body, 0)
    x = x_ref[...]
    shared = _swiglu(x.astype(BF16), wi_ref[...], wo_ref[...])

    def wbody(t, c):
        for k in range(TOP_K):
            copy(t, k).wait()
        return c

    lax.fori_loop(0, tm, wbody, 0)
    gate = gate_ref[...]
    routed = gate[:, 0:1] * buf[0]
    for k in range(1, TOP_K):
        routed = routed + gate[:, k:k + 1] * buf[k]
    y = _ln(DN_ALPHA * x + (routed + shared), g_ref[...], b_ref[...])
    o_ref[...] = y
    ob_ref[...] = y.astype(BF16)


def moe_combine(x, y, dest, gate_t, w_in_s, w_out_s, g, b, tm=128):
    N, D = x.shape
    tm = _pick(N, tm)
    F2 = w_in_s.shape[1]
    row = lambda i: (i, 0)
    fix = lambda i: (0, 0)
    return pl.pallas_call(
        _combine_kernel,
        out_shape=(jax.ShapeDtypeStruct((N, D), F32), jax.ShapeDtypeStruct((N, D), BF16)),
        grid=(N // tm,),
        in_specs=[pl.BlockSpec((TOP_K, tm), lambda i: (0, i), memory_space=pltpu.SMEM),
                  pl.BlockSpec((tm, D), row), pl.BlockSpec((tm, TOP_K), row),
                  pl.BlockSpec((D, F2), fix), pl.BlockSpec((F2 // 2, D), fix),
                  pl.BlockSpec((1, D), fix), pl.BlockSpec((1, D), fix),
                  pl.BlockSpec(memory_space=pl.ANY)],
        out_specs=(pl.BlockSpec((tm, D), row), pl.BlockSpec((tm, D), row)),
        scratch_shapes=[pltpu.VMEM((TOP_K, tm, D), F32), pltpu.SemaphoreType.DMA(())],
        compiler_params=_cparams(("arbitrary",)), name="moe_combine",
    )(dest, x, gate_t, w_in_s, w_out_s, g.reshape(1, D), b.reshape(1, D), y)


def moe_layer(x, router_w, router_b, w_in_e, w_out_e, w_in_s, w_out_s, ln_g, ln_b, tm_e=256):
    N, D = x.shape
    eidx, gate, rank, counts = moe_router(x, router_w, router_b)
    padded = (counts + tm_e - 1) // tm_e * tm_e
    pad_end = jnp.cumsum(padded)
    pad_start = pad_end - padded
    dest = (pad_start[eidx] + rank).astype(jnp.int32)
    n_blk = (N * TOP_K) // tm_e + N_EXPERTS
    P = n_blk * tm_e
    blk_e = jnp.minimum(jnp.searchsorted(pad_end, jnp.arange(n_blk) * tm_e, side='right'),
                        N_EXPERTS - 1).astype(jnp.int32)
    xs = moe_dispatch(x, dest, P)
    ys = moe_experts(xs, blk_e, w_in_e.astype(BF16), w_out_e.astype(BF16), tm_e)
    return moe_combine(x, ys, dest, gate.T, w_in_s.astype(BF16), w_out_s.astype(BF16), ln_g, ln_b)


GLA_HEADS = 4
GLA_DK = D_MODEL // 2 // GLA_HEADS
GLA_DV = D_MODEL // GLA_HEADS
GLA_GATE_RANK = 16
GLA_TAU = 16.0
GLA_CHUNK = 64


def _split3(v):
    h1 = v.astype(BF16)
    r1 = v - h1.astype(F32)
    h2 = r1.astype(BF16)
    h3 = (r1 - h2.astype(F32)).astype(BF16)
    return h1, h2, h3


def _gla_kernel(qkvr_ref, a_ref, wah_ref, wal_ref, ba_ref, ng_ref, tri_ref, o_ref, st_ref):
    H, dk, dv, C = GLA_HEADS, GLA_DK, GLA_DV, GLA_CHUNK

    @pl.when(pl.program_id(1) == 0)
    def _():
        st_ref[...] = jnp.zeros_like(st_ref)

    T = qkvr_ref.shape[0]
    a = a_ref[...]
    ah = a.astype(BF16)
    al = (a - ah.astype(F32)).astype(BF16)
    glog = (jnp.dot(ah, wah_ref[...], preferred_element_type=F32)
            + jnp.dot(al, wah_ref[...], preferred_element_type=F32)
            + jnp.dot(ah, wal_ref[...], preferred_element_type=F32)) + ba_ref[...]
    log_a = jax.nn.log_sigmoid(glog) / GLA_TAU
    tri = tri_ref[...]
    rr = lax.broadcasted_iota(jnp.int32, (C, C), 0)
    cc = lax.broadcasted_iota(jnp.int32, (C, C), 1)
    causal = rr >= cc
    ng = ng_ref[...]
    ct = (((1,), (1,)), ((), ()))
    c0 = (((0,), (0,)), ((), ()))
    for c in range(T // C):
        rows = slice(c * C, (c + 1) * C)
        for h in range(H):
            la = log_a[rows, h * dk:(h + 1) * dk]
            p1, p2, p3 = _split3(la)
            b = (jnp.dot(tri, p1, preferred_element_type=F32) + jnp.dot(tri, p2, preferred_element_type=F32)
                 + jnp.dot(tri, p3, preferred_element_type=F32))
            b_last = b[C - 1:C, :]
            q = qkvr_ref[rows, h * dk:(h + 1) * dk].astype(F32)
            k = qkvr_ref[rows, H * dk + h * dk:H * dk + (h + 1) * dk].astype(F32)
            v = qkvr_ref[rows, 2 * H * dk + h * dv:2 * H * dk + (h + 1) * dv]
            r = qkvr_ref[rows, 2 * H * dk + H * dv + h * dv:2 * H * dk + H * dv + (h + 1) * dv].astype(F32)
            qg = (q * jnp.exp(b)).astype(BF16)
            kg = (k * jnp.exp(-b)).astype(BF16)
            kd = (k * jnp.exp(b_last - b)).astype(BF16)
            att = jnp.where(causal, lax.dot_general(qg, kg, ct, preferred_element_type=F32), 0.0)
            st = st_ref[h]
            o = (jnp.dot(att.astype(BF16), v, preferred_element_type=F32)
                 + lax.dot_general(qg, st.astype(BF16), ct, preferred_element_type=F32))
            st_ref[h] = jnp.exp(b_last) * st + lax.dot_general(v, kd, c0, preferred_element_type=F32)
            o = o * lax.rsqrt(jnp.mean(o * o, axis=-1, keepdims=True) + LN_EPS) * ng
            o_ref[rows, h * dv:(h + 1) * dv] = (o * jax.nn.silu(r)).astype(o_ref.dtype)


def gla_core(qkvr, a, w_a2, b_a, norm_g, B, S, tile=256):
    H, dk, dv, C = GLA_HEADS, GLA_DK, GLA_DV, GLA_CHUNK
    N, W = qkvr.shape
    tile = _pick(S, tile)
    nt = S // tile
    wa = jnp.zeros((LANES, H * dk), F32).at[:GLA_GATE_RANK].set(w_a2)
    wah = wa.astype(BF16)
    wal = (wa - wah.astype(F32)).astype(BF16)
    tri = (jnp.arange(C)[:, None] >= jnp.arange(C)[None, :]).astype(BF16)
    row = lambda b, t: (b * nt + t, 0)
    fix = lambda b, t: (0, 0)
    return pl.pallas_call(
        _gla_kernel, out_shape=jax.ShapeDtypeStruct((N, H * dv), BF16), grid=(B, nt),
        in_specs=[pl.BlockSpec((tile, W), row), pl.BlockSpec((tile, LANES), row),
                  pl.BlockSpec((LANES, H * dk), fix), pl.BlockSpec((LANES, H * dk), fix),
                  pl.BlockSpec((1, H * dk), fix), pl.BlockSpec((1, dv), fix), pl.BlockSpec((C, C), fix)],
        out_specs=pl.BlockSpec((tile, H * dv), row),
        scratch_shapes=[pltpu.VMEM((H, dv, dk), F32)],
        compiler_params=_cparams(("parallel", "arbitrary")), name="gla_core",
    )(qkvr, a, wah, wal, b_a.reshape(1, H * dk), norm_g.reshape(1, dv), tri)


def gla_mixer(xb, w_in, w_a2, b_a, norm_g, B, S):
    H, dk, dv = GLA_HEADS, GLA_DK, GLA_DV
    hk, hv = H * dk, H * dv
    wq, wk, wv, wa, wr = jnp.split(w_in, [hk, 2 * hk, 2 * hk + hv, 2 * hk + hv + GLA_GATE_RANK], axis=1)
    w_main = jnp.concatenate([wq * dk ** -0.5, wk, wv, wr], axis=1).astype(BF16)
    w_gate = jnp.zeros((w_in.shape[0], LANES), F32).at[:, :GLA_GATE_RANK].set(wa).astype(BF16)
    qkvr = matmul(xb, w_main, out_dtype=BF16)
    a = matmul(xb, w_gate, out_dtype=F32)
    return gla_core(qkvr, a, w_a2, b_a, norm_g, B, S)


DIL_HEADS = D_MODEL // HEAD_DIM
DIL_CONFIGS = ((128, 1), (512, 4), (2048, 16))
DIL_TQ = 128


def _dil_attn_kernel(q_ref, kc_ref, kp_ref, vc_ref, vp_ref, o_ref, lse_ref, *, tiles_per_seq, window):
    i = pl.program_id(0)
    tq = q_ref.shape[0]
    first = (i % tiles_per_seq) == 0
    qpos = lax.broadcasted_iota(jnp.int32, (tq, 2 * tq), 0) + tq
    kpos = lax.broadcasted_iota(jnp.int32, (tq, 2 * tq), 1)
    dist = qpos - kpos
    mask = (dist >= 0) & (dist <= window) & ((kpos >= tq) | jnp.logical_not(first))
    lane = lax.broadcasted_iota(jnp.int32, (tq, LANES), 1)
    lo = lane < HEAD_DIM
    ct = (((1,), (1,)), ((), ()))
    for p in range(q_ref.shape[1] // LANES):
        cols = slice(p * LANES, (p + 1) * LANES)
        q2 = q_ref[:, cols]
        k2 = jnp.concatenate([kp_ref[:, cols], kc_ref[:, cols]], axis=0)
        v2 = jnp.concatenate([vp_ref[:, cols], vc_ref[:, cols]], axis=0)
        outs, lses = [], []
        for half in (lo, jnp.logical_not(lo)):
            qm = jnp.where(half, q2, jnp.zeros_like(q2))
            s = jnp.where(mask, lax.dot_general(qm, k2, ct, preferred_element_type=F32), NEG_INF)
            m = jnp.max(s, axis=-1, keepdims=True)
            e = jnp.where(mask, jnp.exp(s - m), 0.0)
            den = jnp.sum(e, axis=-1, keepdims=True)
            pv = jnp.dot(e.astype(BF16), v2, preferred_element_type=F32)
            outs.append(pv / jnp.maximum(den, 1e-30))
            lses.append(m + jnp.log(den))
        o_ref[:, cols] = jnp.where(lo, outs[0], outs[1]).astype(o_ref.dtype)
        lse_ref[:, cols] = jnp.where(lo, lses[0], lses[1])


def dil_attention(qk, v, seq_len, window):
    N, W = v.shape
    tq = DIL_TQ
    assert window == tq and seq_len % tq == 0
    nwb = 1
    cur = lambda i: (i, 0)
    prev = lambda i: (jnp.maximum(i - 1, 0), 0)
    return pl.pallas_call(
        functools.partial(_dil_attn_kernel, tiles_per_seq=seq_len // tq, window=window),
        out_shape=(jax.ShapeDtypeStruct((N, W), BF16), jax.ShapeDtypeStruct((N, W), F32)),
        grid=(N // tq,),
        in_specs=[pl.BlockSpec((tq, W), cur),
                  pl.BlockSpec((tq, W), lambda i: (i, nwb)),
                  pl.BlockSpec((tq, W), lambda i: (jnp.maximum(i - 1, 0), nwb)),
                  pl.BlockSpec((tq, W), cur), pl.BlockSpec((tq, W), prev)],
        out_specs=(pl.BlockSpec((tq, W), cur), pl.BlockSpec((tq, W), cur)),
        compiler_params=_cparams(("parallel",)), name="dil_attn",
    )(qk, qk, qk, v, v)


def _dil_merge_kernel(o0, o1, o2, l0, l1, l2, w_ref, x_ref, g_ref, b_ref, out_ref, outb_ref):
    la, lb, lc = l0[...], l1[...], l2[...]
    m = jnp.maximum(jnp.maximum(la, lb), lc)
    ea, eb, ec = jnp.exp(la - m), jnp.exp(lb - m), jnp.exp(lc - m)
    tot = ea + eb + ec
    o = (ea / tot) * o0[...].astype(F32) + (eb / tot) * o1[...].astype(F32) + (ec / tot) * o2[...].astype(F32)
    h = jnp.dot(o.astype(BF16), w_ref[...], preferred_element_type=F32)
    y = _ln(DN_ALPHA * x_ref[...] + h, g_ref[...], b_ref[...])
    out_ref[...] = y
    outb_ref[...] = y.astype(BF16)


def dil_merge_out(os_, lses, w_out, x, g, b, tm=256):
    N, D = x.shape
    W = w_out.shape[0]
    tm = _pick(N, tm)
    row = lambda i: (i, 0)
    fix = lambda i: (0, 0)
    rs = pl.BlockSpec((tm, W), row)
    return pl.pallas_call(
        _dil_merge_kernel,
        out_shape=(jax.ShapeDtypeStruct((N, D), F32), jax.ShapeDtypeStruct((N, D), BF16)),
        grid=(N // tm,),
        in_specs=[rs, rs, rs, rs, rs, rs, pl.BlockSpec((W, D), fix), pl.BlockSpec((tm, D), row),
                  pl.BlockSpec((1, D), fix), pl.BlockSpec((1, D), fix)],
        out_specs=(pl.BlockSpec((tm, D), row), pl.BlockSpec((tm, D), row)),
        compiler_params=_cparams(("parallel",)), name="dil_merge",
    )(*os_, *lses, w_out, x, g.reshape(1, D), b.reshape(1, D))


def dilated_layer(x, xb, w_in, w_out, ln_g, ln_b, B, S):
    N, D = x.shape
    H, dh = DIL_HEADS, HEAD_DIM
    W = H * dh
    w6 = w_in.reshape(D, len(DIL_CONFIGS), 3, W)
    pos = jnp.arange(S)
    os_, lses = [], []
    for gi, (window, dil) in enumerate(DIL_CONFIGS):
        L = S // dil
        wqk = jnp.concatenate([w6[:, gi, 0] * dh ** -0.5, w6[:, gi, 1]], axis=1).astype(BF16)
        wv = w6[:, gi, 2].astype(BF16)
        xp = xb.reshape(B, L, dil, D).transpose(0, 2, 1, 3).reshape(N, D) if dil > 1 else xb
        ppos = pos.reshape(L, dil).T.reshape(S)
        qk = matmul(xp, wqk, out_dtype=BF16, rope=rope_tables(ppos))
        v = matmul(xp, wv, out_dtype=BF16)
        o, lse = dil_attention(qk, v, L, window // dil)
        if dil > 1:
            o = o.reshape(B, dil, L, W).transpose(0, 2, 1, 3).reshape(N, W)
            lse = lse.reshape(B, dil, L, W).transpose(0, 2, 1, 3).reshape(N, W)
        os_.append(o)
        lses.append(lse)
    return dil_merge_out(os_, lses, w_out.astype(BF16), x, ln_g, ln_b)


NSA_HEADS = D_MODEL // HEAD_DIM
NSA_KV_HEADS = 4
NSA_GROUP = NSA_HEADS // NSA_KV_HEADS
CMP_LEN = 32
CMP_STRIDE = 16
CMP_HIDDEN = 256
SEL_LEN = 64
SEL_TOPN = 16
NSA_WINDOW = 512
FORCE_BONUS = 100.0
NSA_TQ = 128
NSA_TK = 128


def _compress_kernel(x_ref, p_ref, w1a_ref, w1b_ref, w2_ref, o_ref):
    x = x_ref[0].astype(F32)
    n = x.shape[0]
    first = jnp.dot((x + p_ref[0:1, :]).astype(BF16), w1a_ref[...], preferred_element_type=F32)
    second = jnp.dot((x + p_ref[1:2, :]).astype(BF16), w1b_ref[...], preferred_element_type=F32)
    hid = first + pltpu.roll(second, n - 1, 0)
    o_ref[0] = jnp.dot(jax.nn.gelu(hid).astype(BF16), w2_ref[...], preferred_element_type=F32).astype(o_ref.dtype)


def nsa_compress(t, pos_emb, w1, w2):
    BK, n, W = t.shape
    half = CMP_STRIDE * HEAD_DIM
    p = pos_emb.reshape(2, half).astype(F32)
    fix = lambda i: (0, 0)
    return pl.pallas_call(
        _compress_kernel, out_shape=jax.ShapeDtypeStruct((BK, n, HEAD_DIM), BF16), grid=(BK,),
        in_specs=[pl.BlockSpec((1, n, W), lambda i: (i, 0, 0)), pl.BlockSpec((2, half), fix),
                  pl.BlockSpec((half, CMP_HIDDEN), fix), pl.BlockSpec((half, CMP_HIDDEN), fix),
                  pl.BlockSpec((CMP_HIDDEN, HEAD_DIM), fix)],
        out_specs=pl.BlockSpec((1, n, HEAD_DIM), lambda i: (i, 0, 0)),
        compiler_params=_cparams(("parallel",)), name="nsa_compress",
    )(t, p, w1[:half].astype(BF16), w1[half:].astype(BF16), w2.astype(BF16))


def _online_step(carry, s, valid, v):
    m, l, acc = carry
    m_new = jnp.maximum(m, jnp.max(s, axis=-1, keepdims=True))
    alpha = jnp.exp(m - m_new)
    p = jnp.exp(s - m_new)
    if valid is not None:
        p = jnp.where(valid, p, 0.0)
    l = alpha * l + jnp.sum(p, axis=-1, keepdims=True)
    acc = alpha * acc + jnp.dot(p.astype(BF16), v, preferred_element_type=F32)
    return m_new, l, acc


def _nsa_attn_kernel(q_ref, kc_ref, vc_ref, ks_ref, vs_ref, kw_ref, vw_ref, gl_ref, ovl_ref, o_ref):
    i = pl.program_id(2)
    G, tq, dh = q_ref.shape[2], q_ref.shape[3], q_ref.shape[4]
    M = G * tq
    tk = NSA_TK
    t0 = i * tq
    ct = (((1,), (1,)), ((), ()))
    q = q_ref[0, 0].reshape(M, dh)

    def tpos(shape):
        return t0 + (lax.broadcasted_iota(jnp.int32, shape, 0) & (tq - 1))

    n_cmp = kc_ref.shape[2]
    s_c = lax.dot_general(q, kc_ref[0, 0], ct, preferred_element_type=F32)
    cend = lax.broadcasted_iota(jnp.int32, (M, n_cmp), 1) * CMP_STRIDE + (CMP_LEN - 1)
    vis = cend <= tpos((M, n_cmp))
    s_c = jnp.where(vis, s_c, NEG_INF)
    e_c = jnp.where(vis, jnp.exp(s_c - jnp.max(s_c, axis=-1, keepdims=True)), 0.0)
    p_c = e_c / jnp.maximum(jnp.sum(e_c, axis=-1, keepdims=True), 1e-30)
    o_c = jnp.dot(p_c.astype(BF16), vc_ref[0, 0], preferred_element_type=F32)

    psum = jnp.sum(p_c.reshape(G, tq, n_cmp), axis=0)
    ovl = ovl_ref[...]
    imp = sum(jnp.dot(piece, ovl, preferred_element_type=F32) for piece in _split3(psum))
    n_sel = imp.shape[1]
    blk = lax.broadcasted_iota(jnp.int32, (tq, n_sel), 1)
    cur = tpos((tq, n_sel)) // SEL_LEN
    forced = (blk == 0) | (blk == cur) | (blk == cur - 1)
    score = jnp.where(blk <= cur, imp + jnp.where(forced, FORCE_BONUS, 0.0), -1.0)
    chosen = jnp.zeros((tq, n_sel), jnp.bool_)
    for _ in range(min(SEL_TOPN, n_sel)):
        mx, idx = _first_index_of_max(score, blk, 1, n_sel)
        pick = (blk == idx) & (mx >= 0.0)
        chosen = chosen | pick
        score = jnp.where(blk == idx, -2.0, score)
    bias = jnp.where(chosen, 0.0, NEG_INF).astype(BF16)
    q_aug = jnp.concatenate([q, jnp.concatenate([bias] * G, axis=0)], axis=1)

    init = (jnp.full((M, 1), NEG_INF, F32), jnp.zeros((M, 1), F32), jnp.zeros((M, dh), F32))

    def sel_tile(jt, carry, diagonal):
        rows = pl.ds(pl.multiple_of(jt * tk, tk), tk)
        s = lax.dot_general(q_aug, ks_ref[0, 0, rows, :], ct, preferred_element_type=F32)
        valid = None
        if diagonal:
            kpos = jt * tk + lax.broadcasted_iota(jnp.int32, (M, tk), 1)
            valid = kpos <= tpos((M, tk))
            s = jnp.where(valid, s, NEG_INF)
        return _online_step(carry, s, valid, vs_ref[0, 0, rows, :])

    carry = lax.fori_loop(0, i, lambda jt, c: sel_tile(jt, c, False), init)
    _, l_s, acc_s = sel_tile(i, carry, True)
    o_s = acc_s / jnp.maximum(l_s, 1e-30)

    def win_tile(jt, carry):
        rows = pl.ds(pl.multiple_of(jt * tk, tk), tk)
        s = lax.dot_general(q, kw_ref[0, 0, rows, :], ct, preferred_element_type=F32)
        dist = tpos((M, tk)) - (jt * tk + lax.broadcasted_iota(jnp.int32, (M, tk), 1))
        valid = (dist >= 0) & (dist < NSA_WINDOW)
        return _online_step(carry, jnp.where(valid, s, NEG_INF), valid, vw_ref[0, 0, rows, :])

    _, l_w, acc_w = lax.fori_loop(jnp.maximum(i - NSA_WINDOW // tk, 0), i + 1, win_tile, init)
    o_w = acc_w / jnp.maximum(l_w, 1e-30)

    gates = jax.nn.sigmoid(gl_ref[0, 0].reshape(M, 3))
    o = gates[:, 0:1] * o_c + gates[:, 1:2] * o_s + gates[:, 2:3] * o_w
    o_ref[0, 0] = o.reshape(G, tq, dh).astype(o_ref.dtype)


def nsa_attention(q, k_cmp, v_cmp, ks_aug, vs, kw, vw, gl):
    B, KH, G, S, dh = q.shape
    tq = NSA_TQ
    assert tq == NSA_TK and S % tq == 0
    n_cmp = k_cmp.shape[2]
    n_sel = S // SEL_LEN
    c0 = np.arange(n_cmp)[:, None] * CMP_STRIDE
    s0 = np.arange(n_sel)[None, :] * SEL_LEN
    ovl = jnp.asarray((c0 < s0 + SEL_LEN) & (c0 + CMP_LEN - 1 >= s0), BF16)
    qs = pl.BlockSpec((1, 1, G, tq, dh), lambda b, h, i: (b, h, 0, i, 0))
    full = lambda a: pl.BlockSpec((1, 1) + a.shape[2:], lambda b, h, i: (b, h, 0, 0))
    return pl.pallas_call(
        _nsa_attn_kernel, out_shape=jax.ShapeDtypeStruct((B, KH, G, S, dh), BF16), grid=(B, KH, S // tq),
        in_specs=[qs, full(k_cmp), full(v_cmp), full(ks_aug), full(vs), full(kw), full(vw),
                  pl.BlockSpec((1, 1, G, tq, 3), lambda b, h, i: (b, h, 0, i, 0)),
                  pl.BlockSpec((n_cmp, n_sel), lambda b, h, i: (0, 0))],
        out_specs=qs,
        compiler_params=_cparams(("parallel", "parallel", "arbitrary")), name="nsa_attn",
    )(q, k_cmp, v_cmp, ks_aug, vs, kw, vw, gl, ovl)


def nsa_mixer(xb, w_in, ck_pos, ck_w1, ck_w2, cv_pos, cv_w1, cv_w2, B, S):
    H, KH, G, dh = NSA_HEADS, NSA_KV_HEADS, NSA_GROUP, HEAD_DIM
    kvw = KH * dh
    cuts = np.cumsum([H * dh, kvw, kvw, kvw, kvw, kvw, kvw]).tolist()
    wq, wkc, wvc, wks, wvs, wkw, wvw, wgl = jnp.split(w_in, cuts, axis=1)
    w_rope = jnp.concatenate([wq * dh ** -0.5, wkc, wks, wkw], axis=1).astype(BF16)
    w_val = jnp.concatenate([wvc, wvs, wvw], axis=1).astype(BF16)
    w_gate = jnp.zeros((w_in.shape[0], LANES), F32).at[:, :3 * H].set(wgl).astype(BF16)
    roped = matmul(xb, w_rope, out_dtype=BF16, rope=rope_tables(jnp.arange(S)), tn=256)
    vals = matmul(xb, w_val, out_dtype=BF16, tn=256)
    gl = matmul(xb, w_gate, out_dtype=F32)[:, :3 * H]

    def heads(t):
        return t.reshape(B, S, KH, dh).transpose(0, 2, 1, 3)

    q = roped[:, :H * dh].reshape(B, S, KH, G, dh).transpose(0, 2, 3, 1, 4)
    kc, ks, kw = (heads(roped[:, H * dh + j * kvw:H * dh + (j + 1) * kvw]) for j in range(3))
    vc, vs, vw = (heads(vals[:, j * kvw:(j + 1) * kvw]) for j in range(3))
    n16 = S // CMP_STRIDE
    k_cmp = nsa_compress(kc.reshape(B * KH, n16, CMP_STRIDE * dh), ck_pos, ck_w1, ck_w2).reshape(B, KH, n16, dh)
    v_cmp = nsa_compress(vc.reshape(B * KH, n16, CMP_STRIDE * dh), cv_pos, cv_w1, cv_w2).reshape(B, KH, n16, dh)
    n_sel = S // SEL_LEN
    onehot = (jnp.arange(S)[:, None] // SEL_LEN == jnp.arange(n_sel)[None, :]).astype(BF16)
    ks_aug = jnp.concatenate([ks, jnp.broadcast_to(onehot, (B, KH, S, n_sel))], axis=-1)
    glh = gl.reshape(B, S, KH, G, 3).transpose(0, 2, 3, 1, 4)
    o = nsa_attention(q, k_cmp, v_cmp, ks_aug, vs, kw, vw, glh)
    return o.transpose(0, 3, 1, 2, 4).reshape(B * S, H * dh)


def kernel(x, l0_nsa_w_in, l0_nsa_w_out, l0_nsa_ck_pos, l0_nsa_ck_w1, l0_nsa_ck_w2, l0_nsa_cv_pos, l0_nsa_cv_w1, l0_nsa_cv_w2, l0_ln1_g, l0_ln1_b, l0_router_w, l0_router_b, l0_moe_w_in, l0_moe_w_out, l0_shared_w_in, l0_shared_w_out, l0_ln2_g, l0_ln2_b, l1_gla_w_in, l1_gla_w_a2, l1_gla_b_a, l1_gla_norm_g, l1_gla_w_out, l1_ln1_g, l1_ln1_b, l1_router_w, l1_router_b, l1_moe_w_in, l1_moe_w_out, l1_shared_w_in, l1_shared_w_out, l1_ln2_g, l1_ln2_b, l2_dil_w_in, l2_dil_w_out, l2_ln1_g, l2_ln1_b, l2_router_w, l2_router_b, l2_moe_w_in, l2_moe_w_out, l2_shared_w_in, l2_shared_w_out, l2_ln2_g, l2_ln2_b, l3_nsa_w_in, l3_nsa_w_out, l3_nsa_ck_pos, l3_nsa_ck_w1, l3_nsa_ck_w2, l3_nsa_cv_pos, l3_nsa_cv_w1, l3_nsa_cv_w2, l3_ln1_g, l3_ln1_b, l3_router_w, l3_router_b, l3_moe_w_in, l3_moe_w_out, l3_shared_w_in, l3_shared_w_out, l3_ln2_g, l3_ln2_b):
    B, S, D = x.shape
    xf = x.reshape(B * S, D)
    xb = xf.astype(BF16)

    h = nsa_mixer(xb, l0_nsa_w_in, l0_nsa_ck_pos, l0_nsa_ck_w1, l0_nsa_ck_w2, l0_nsa_cv_pos, l0_nsa_cv_w1, l0_nsa_cv_w2, B, S)
    xf, xb = matmul_res_ln(h, l0_nsa_w_out.astype(BF16), xf, l0_ln1_g, l0_ln1_b)
    xf, xb = moe_layer(xf, l0_router_w, l0_router_b, l0_moe_w_in, l0_moe_w_out, l0_shared_w_in, l0_shared_w_out, l0_ln2_g, l0_ln2_b)

    h = gla_mixer(xb, l1_gla_w_in, l1_gla_w_a2, l1_gla_b_a, l1_gla_norm_g, B, S)
    xf, xb = matmul_res_ln(h, l1_gla_w_out.astype(BF16), xf, l1_ln1_g, l1_ln1_b)
    xf, xb = moe_layer(xf, l1_router_w, l1_router_b, l1_moe_w_in, l1_moe_w_out, l1_shared_w_in, l1_shared_w_out, l1_ln2_g, l1_ln2_b)

    xf, xb = dilated_layer(xf, xb, l2_dil_w_in, l2_dil_w_out, l2_ln1_g, l2_ln1_b, B, S)
    xf, xb = moe_layer(xf, l2_router_w, l2_router_b, l2_moe_w_in, l2_moe_w_out, l2_shared_w_in, l2_shared_w_out, l2_ln2_g, l2_ln2_b)

    h = nsa_mixer(xb, l3_nsa_w_in, l3_nsa_ck_pos, l3_nsa_ck_w1, l3_nsa_ck_w2, l3_nsa_cv_pos, l3_nsa_cv_w1, l3_nsa_cv_w2, B, S)
    xf, xb = matmul_res_ln(h, l3_nsa_w_out.astype(BF16), xf, l3_ln1_g, l3_ln1_b)
    xf, xb = moe_layer(xf, l3_router_w, l3_router_b, l3_moe_w_in, l3_moe_w_out, l3_shared_w_in, l3_shared_w_out, l3_ln2_g, l3_ln2_b)
    return xf.reshape(B, S, D)
```

```python
import functools
import math

import jax
import jax.numpy as jnp
import numpy as np
from jax import lax
from jax.experimental import pallas as pl
from jax.experimental.pallas import tpu as pltpu

F32 = jnp.float32
BF16 = jnp.bfloat16

D_MODEL = 1024
DEPTH = 4
HEAD_DIM = 64
ROPE_THETA = 500000.0
ROPE_DIM = HEAD_DIM // 4
ROPE_HALF = ROPE_DIM // 2

N_EXPERTS = 64
TOP_K = 8
N_GROUPS = 8
TOPK_GROUPS = 4
GROUP_SIZE = N_EXPERTS // N_GROUPS
D_EXPERT = 256
ROUTED_SCALE = 2.5

DN_ALPHA = (2.0 * DEPTH) ** 0.25
LN_EPS = 1e-5
NEG_INF = -1e30

LANES = 128
VMEM_LIMIT = 48 * 1024 * 1024


def _cparams(sem, **kw):
    return pltpu.CompilerParams(dimension_semantics=sem, vmem_limit_bytes=VMEM_LIMIT, **kw)


def _pick(n, pref):
    t = min(pref, n)
    while n % t:
        t //= 2
    return t


def _mm_kernel(x_ref, w_ref, o_ref):
    o_ref[...] = jnp.dot(x_ref[...], w_ref[...], preferred_element_type=F32).astype(o_ref.dtype)


def _mm_rope_kernel(x_ref, w_ref, c_ref, sm_ref, sp_ref, o_ref):
    y = jnp.dot(x_ref[...], w_ref[...], preferred_element_type=F32)
    reps = y.shape[1] // LANES
    c = jnp.tile(c_ref[...], (1, reps))
    sm = jnp.tile(sm_ref[...], (1, reps))
    sp = jnp.tile(sp_ref[...], (1, reps))
    up = pltpu.roll(y, y.shape[1] - ROPE_HALF, 1)
    dn = pltpu.roll(y, ROPE_HALF, 1)
    o_ref[...] = (y * c + up * sm + dn * sp).astype(o_ref.dtype)


def matmul(x, w, out_dtype=F32, rope=None, tm=512, tn=512):
    M, K = x.shape
    N = w.shape[1]
    tm = _pick(M, tm)
    tn = _pick(N, tn)
    grid = (N // tn, M // tm)
    x_spec = pl.BlockSpec((tm, K), lambda j, i: (i, 0))
    w_spec = pl.BlockSpec((K, tn), lambda j, i: (0, j))
    o_spec = pl.BlockSpec((tm, tn), lambda j, i: (i, j))
    if rope is None:
        return pl.pallas_call(
            _mm_kernel, out_shape=jax.ShapeDtypeStruct((M, N), out_dtype), grid=grid,
            in_specs=[x_spec, w_spec], out_specs=o_spec,
            compiler_params=_cparams(("parallel", "parallel")), name="mm")(x, w)
    R = rope[0].shape[0]
    tm = _pick(R, tm)
    grid = (N // tn, M // tm)
    x_spec = pl.BlockSpec((tm, K), lambda j, i: (i, 0))
    o_spec = pl.BlockSpec((tm, tn), lambda j, i: (i, j))
    nr = R // tm
    t_spec = pl.BlockSpec((tm, LANES), lambda j, i: (i % nr, 0))
    return pl.pallas_call(
        _mm_rope_kernel, out_shape=jax.ShapeDtypeStruct((M, N), out_dtype), grid=grid,
        in_specs=[x_spec, w_spec, t_spec, t_spec, t_spec], out_specs=o_spec,
        compiler_params=_cparams(("parallel", "parallel")), name="mm_rope")(x, w, *rope)


def rope_tables(pos):
    inv = ROPE_THETA ** (-jnp.arange(ROPE_HALF, dtype=F32) * 2.0 / ROPE_DIM)
    ang = pos.astype(F32)[:, None] * inv[None, :]
    cos, sin = jnp.cos(ang), jnp.sin(ang)
    n = pos.shape[0]
    ones = jnp.ones((n, HEAD_DIM - ROPE_DIM), F32)
    zeros = jnp.zeros((n, HEAD_DIM - ROPE_DIM), F32)
    zh = jnp.zeros((n, ROPE_HALF), F32)
    c = jnp.concatenate([cos, cos, ones], 1)
    sm = jnp.concatenate([-sin, zh, zeros], 1)
    sp = jnp.concatenate([zh, sin, zeros], 1)
    return tuple(jnp.tile(t, (1, LANES // HEAD_DIM)) for t in (c, sm, sp))


def _ln(v, g, b):
    mu = jnp.mean(v, axis=-1, keepdims=True)
    d = v - mu
    var = jnp.mean(d * d, axis=-1, keepdims=True)
    return d * lax.rsqrt(var + LN_EPS) * g + b


def _mm_res_ln_kernel(a_ref, w_ref, x_ref, g_ref, b_ref, o_ref, ob_ref):
    h = jnp.dot(a_ref[...], w_ref[...], preferred_element_type=F32)
    y = _ln(DN_ALPHA * x_ref[...] + h, g_ref[...], b_ref[...])
    o_ref[...] = y
    ob_ref[...] = y.astype(BF16)


def matmul_res_ln(a, w, x, g, b, tm=256):
    M, K = a.shape
    D = w.shape[1]
    tm = _pick(M, tm)
    row = lambda i: (i, 0)
    fix = lambda i: (0, 0)
    return pl.pallas_call(
        _mm_res_ln_kernel,
        out_shape=(jax.ShapeDtypeStruct((M, D), F32), jax.ShapeDtypeStruct((M, D), BF16)),
        grid=(M // tm,),
        in_specs=[pl.BlockSpec((tm, K), row), pl.BlockSpec((K, D), fix), pl.BlockSpec((tm, D), row),
                  pl.BlockSpec((1, D), fix), pl.BlockSpec((1, D), fix)],
        out_specs=(pl.BlockSpec((tm, D), row), pl.BlockSpec((tm, D), row)),
        compiler_params=_cparams(("parallel",)), name="mm_res_ln")(a, w, x, g.reshape(1, D), b.reshape(1, D))


def _first_index_of_max(v, iota, axis, n):
    m = jnp.max(v, axis=axis, keepdims=True)
    idx = jnp.min(jnp.where(v == m, iota, n), axis=axis, keepdims=True)
    return m, idx


def _router_kernel(x_ref, wh_ref, wl_ref, rb_ref, tri_ref, eidx_ref, gate_ref, rank_ref, cnt_ref, carry_ref):
    i = pl.program_id(0)

    @pl.when(i == 0)
    def _():
        carry_ref[...] = jnp.zeros_like(carry_ref)

    x = x_ref[...]
    xh = x.astype(BF16)
    xl = (x - xh.astype(F32)).astype(BF16)
    dn = (((1,), (1,)), ((), ()))
    logits = (lax.dot_general(wh_ref[...], xh, dn, preferred_element_type=F32)
              + lax.dot_general(wh_ref[...], xl, dn, preferred_element_type=F32)
              + lax.dot_general(wl_ref[...], xh, dn, preferred_element_type=F32))
    tm = logits.shape[1]
    s = jax.nn.sigmoid(logits)
    sb = s + rb_ref[...]
    sb3 = sb.reshape(N_GROUPS, GROUP_SIZE, tm)
    io3 = lax.broadcasted_iota(jnp.int32, sb3.shape, 1)
    m1, i1 = _first_index_of_max(sb3, io3, 1, GROUP_SIZE)
    m2 = jnp.max(jnp.where(io3 == i1, -jnp.inf, sb3), axis=1, keepdims=True)
    gs = (m1 + m2).reshape(N_GROUPS, tm)
    iog = lax.broadcasted_iota(jnp.int32, gs.shape, 0)
    gmask = jnp.zeros(gs.shape, jnp.bool_)
    for _ in range(TOPK_GROUPS):
        _, gi = _first_index_of_max(gs, iog, 0, N_GROUPS)
        pick = iog == gi
        gmask = gmask | pick
        gs = jnp.where(pick, -jnp.inf, gs)
    emask = jnp.broadcast_to(gmask.reshape(N_GROUPS, 1, tm), sb3.shape).reshape(N_EXPERTS, tm)
    cand = jnp.where(emask, sb, NEG_INF)
    ioe = lax.broadcasted_iota(jnp.int32, cand.shape, 0)
    sel = jnp.zeros(cand.shape, F32)
    picks, eidx, gates = [], [], []
    for _ in range(TOP_K):
        _, ei = _first_index_of_max(cand, ioe, 0, N_EXPERTS)
        pick = ioe == ei
        picks.append(pick)
        eidx.append(ei)
        gates.append(jnp.sum(jnp.where(pick, s, 0.0), axis=0, keepdims=True))
        sel = jnp.where(pick, 1.0, sel)
        cand = jnp.where(pick, -jnp.inf, cand)
    g = jnp.concatenate(gates, axis=0)
    gate_ref[...] = g / jnp.sum(g, axis=0, keepdims=True) * ROUTED_SCALE
    eidx_ref[...] = jnp.concatenate(eidx, axis=0)
    before = jnp.dot(sel.astype(BF16), tri_ref[...], preferred_element_type=F32) + carry_ref[...]
    rank_ref[...] = jnp.concatenate(
        [jnp.sum(jnp.where(p, before, 0.0), axis=0, keepdims=True) for p in picks], axis=0).astype(jnp.int32)
    carry_ref[...] = carry_ref[...] + jnp.sum(sel, axis=1, keepdims=True)
    cnt_ref[...] = jnp.broadcast_to(carry_ref[...], cnt_ref.shape).astype(jnp.int32)


def moe_router(x, router_w, router_b, tm=512):
    N, D = x.shape
    tm = _pick(N, tm)
    wt = router_w.T
    wh = wt.astype(BF16)
    wl = (wt - wh.astype(F32)).astype(BF16)
    tri = (jnp.arange(tm)[:, None] < jnp.arange(tm)[None, :]).astype(BF16)
    fix = lambda i: (0, 0)
    col = lambda i: (0, i)
    eidx, gate, rank, cnt = pl.pallas_call(
        _router_kernel,
        out_shape=(jax.ShapeDtypeStruct((TOP_K, N), jnp.int32), jax.ShapeDtypeStruct((TOP_K, N), F32),
                   jax.ShapeDtypeStruct((TOP_K, N), jnp.int32), jax.ShapeDtypeStruct((N_EXPERTS, LANES), jnp.int32)),
        grid=(N // tm,),
        in_specs=[pl.BlockSpec((tm, D), lambda i: (i, 0)), pl.BlockSpec((N_EXPERTS, D), fix),
                  pl.BlockSpec((N_EXPERTS, D), fix), pl.BlockSpec((N_EXPERTS, 1), fix), pl.BlockSpec((tm, tm), fix)],
        out_specs=(pl.BlockSpec((TOP_K, tm), col), pl.BlockSpec((TOP_K, tm), col), pl.BlockSpec((TOP_K, tm), col),
                   pl.BlockSpec((N_EXPERTS, LANES), fix)),
        scratch_shapes=[pltpu.VMEM((N_EXPERTS, 1), F32)],
        compiler_params=_cparams(("arbitrary",)), name="moe_router",
    )(x, wh, wl, router_b.reshape(N_EXPERTS, 1).astype(F32), tri)
    return eidx, gate, rank, cnt[:, 0]


def _dispatch_kernel(zrow_ref, dest_ref, x_ref, xs_ref, zbuf, sem, zsem):
    tm = x_ref.shape[0]
    tz = zbuf.shape[0]

    @pl.when(pl.program_id(0) == 0)
    def _():
        zbuf[...] = jnp.zeros_like(zbuf)

        def zcopy(j):
            return pltpu.make_async_copy(zbuf, xs_ref.at[pl.ds(pl.multiple_of(zrow_ref[j], tz), tz)], zsem)

        def zstart(j, c):
            @pl.when(zrow_ref[j] >= 0)
            def _():
                zcopy(j).start()
            return c

        def zwait(j, c):
            @pl.when(zrow_ref[j] >= 0)
            def _():
                zcopy(j).wait()
            return c

        lax.fori_loop(0, zrow_ref.shape[0], zstart, 0)
        lax.fori_loop(0, zrow_ref.shape[0], zwait, 0)

    def body(t, c):
        for k in range(TOP_K):
            pltpu.make_async_copy(x_ref.at[pl.ds(t, 1)], xs_ref.at[pl.ds(dest_ref[k, t], 1)], sem).start()
        return c

    lax.fori_loop(0, tm, body, 0)

    def wbody(t, c):
        for k in range(TOP_K):
            pltpu.make_async_copy(x_ref.at[pl.ds(t, 1)], xs_ref.at[pl.ds(dest_ref[k, t], 1)], sem).wait()
        return c

    lax.fori_loop(0, tm, wbody, 0)


def moe_dispatch(x, dest, zrow, P, tz, tm=512):
    N, D = x.shape
    tm = _pick(N, tm)
    grid_spec = pltpu.PrefetchScalarGridSpec(
        num_scalar_prefetch=1, grid=(N // tm,),
        in_specs=[pl.BlockSpec((TOP_K, tm), lambda i, z: (0, i), memory_space=pltpu.SMEM),
                  pl.BlockSpec((tm, D), lambda i, z: (i, 0))],
        out_specs=pl.BlockSpec(memory_space=pl.ANY),
        scratch_shapes=[pltpu.VMEM((tz, D), x.dtype), pltpu.SemaphoreType.DMA(()), pltpu.SemaphoreType.DMA(())])
    return pl.pallas_call(
        _dispatch_kernel, out_shape=jax.ShapeDtypeStruct((P, D), x.dtype), grid_spec=grid_spec,
        compiler_params=_cparams(("arbitrary",), has_side_effects=True), name="moe_dispatch",
    )(zrow, dest, x)


def _swiglu(x_bf16, w_in, w_out):
    h = jnp.dot(x_bf16, w_in, preferred_element_type=F32)
    f = h.shape[1] // 2
    a = jax.nn.silu(h[:, :f]) * h[:, f:]
    return jnp.dot(a.astype(BF16), w_out, preferred_element_type=F32)


def _expert_kernel(blk_e_ref, xs_ref, wi_ref, wo_ref, y_ref):
    del blk_e_ref
    y_ref[...] = _swiglu(xs_ref[...].astype(BF16), wi_ref[0], wo_ref[0])


def moe_experts(xs, blk_e, w_in_e, w_out_e, tm):
    P, D = xs.shape
    F2 = w_in_e.shape[2]
    grid_spec = pltpu.PrefetchScalarGridSpec(
        num_scalar_prefetch=1, grid=(P // tm,),
        in_specs=[pl.BlockSpec((tm, D), lambda i, be: (i, 0)),
                  pl.BlockSpec((1, D, F2), lambda i, be: (be[i], 0, 0)),
                  pl.BlockSpec((1, F2 // 2, D), lambda i, be: (be[i], 0, 0))],
        out_specs=pl.BlockSpec((tm, D), lambda i, be: (i, 0)))
    return pl.pallas_call(
        _expert_kernel, out_shape=jax.ShapeDtypeStruct((P, D), F32), grid_spec=grid_spec,
        compiler_params=_cparams(("arbitrary",)), name="moe_experts")(blk_e, xs, w_in_e, w_out_e)


def _combine_kernel(dest_ref, x_ref, gate_ref, wi_ref, wo_ref, g_ref, b_ref, y_ref, o_ref, ob_ref, buf, sem):
    tm = x_ref.shape[0]

    def copy(t, k):
        return pltpu.make_async_copy(y_ref.at[pl.ds(dest_ref[k, t], 1)], buf.at[k, pl.ds(t, 1)], sem)

    def body(t, c):
        for k in range(TOP_K):
            copy(t, k).start()
        return c

    lax.fori_loop(0, tm, body, 0)
    x = x_ref[...]
    shared = _swiglu(x.astype(BF16), wi_ref[...], wo_ref[...])

    def wbody(t, c):
        for k in range(TOP_K):
            copy(t, k).wait()
        return c

    lax.fori_loop(0, tm, wbody, 0)
    gate = gate_ref[...]
    routed = gate[:, 0:1] * buf[0]
    for k in range(1, TOP_K):
        routed = routed + gate[:, k:k + 1] * buf[k]
    y = _ln(DN_ALPHA * x + (routed + shared), g_ref[...], b_ref[...])
    o_ref[...] = y
    ob_ref[...] = y.astype(BF16)


def moe_combine(x, y, dest, gate_t, w_in_s, w_out_s, g, b, tm=128):
    N, D = x.shape
    tm = _pick(N, tm)
    F2 = w_in_s.shape[1]
    row = lambda i: (i, 0)
    fix = lambda i: (0, 0)
    return pl.pallas_call(
        _combine_kernel,
        out_shape=(jax.ShapeDtypeStruct((N, D), F32), jax.ShapeDtypeStruct((N, D), BF16)),
        grid=(N // tm,),
        in_specs=[pl.BlockSpec((TOP_K, tm), lambda i: (0, i), memory_space=pltpu.SMEM),
                  pl.BlockSpec((tm, D), row), pl.BlockSpec((tm, TOP_K), row),
                  pl.BlockSpec((D, F2), fix), pl.BlockSpec((F2 // 2, D), fix),
                  pl.BlockSpec((1, D), fix), pl.BlockSpec((1, D), fix),
                  pl.BlockSpec(memory_space=pl.ANY)],
        out_specs=(pl.BlockSpec((tm, D), row), pl.BlockSpec((tm, D), row)),
        scratch_shapes=[pltpu.VMEM((TOP_K, tm, D), F32), pltpu.SemaphoreType.DMA(())],
        compiler_params=_cparams(("arbitrary",)), name="moe_combine",
    )(dest, x, gate_t, w_in_s, w_out_s, g.reshape(1, D), b.reshape(1, D), y)


def moe_layer(x, router_w, router_b, w_in_e, w_out_e, w_in_s, w_out_s, ln_g, ln_b, tm_e=256):
    N, D = x.shape
    eidx, gate, rank, counts = moe_router(x, router_w, router_b)
    padded = (counts + tm_e - 1) // tm_e * tm_e
    pad_end = jnp.cumsum(padded)
    pad_start = pad_end - padded
    experts = jnp.arange(N_EXPERTS, dtype=jnp.int32)
    start_of = jnp.sum(jnp.where(eidx[:, :, None] == experts, pad_start, 0), axis=-1)
    dest = (start_of + rank).astype(jnp.int32)
    n_blk = (N * TOP_K) // tm_e + N_EXPERTS
    P = n_blk * tm_e
    blk_row = jnp.arange(n_blk, dtype=jnp.int32) * tm_e
    blk_e = jnp.minimum(jnp.sum(pad_end[None, :] <= blk_row[:, None], axis=1), N_EXPERTS - 1).astype(jnp.int32)
    tail = pad_end[-1] + experts * tm_e
    zrow = jnp.concatenate([jnp.where(padded > 0, pad_end - tm_e, -1),
                            jnp.where(tail < P, tail, -1)]).astype(jnp.int32)
    xs = moe_dispatch(x, dest, zrow, P, tm_e)
    ys = moe_experts(xs, blk_e, w_in_e.astype(BF16), w_out_e.astype(BF16), tm_e)
    return moe_combine(x, ys, dest, gate.T, w_in_s.astype(BF16), w_out_s.astype(BF16), ln_g, ln_b)


GLA_HEADS = 4
GLA_DK = D_MODEL // 2 // GLA_HEADS
GLA_DV = D_MODEL // GLA_HEADS
GLA_GATE_RANK = 16
GLA_TAU = 16.0
GLA_CHUNK = 64


def _split3(v):
    h1 = v.astype(BF16)
    r1 = v - h1.astype(F32)
    h2 = r1.astype(BF16)
    h3 = (r1 - h2.astype(F32)).astype(BF16)
    return h1, h2, h3


def _gla_kernel(qkvr_ref, a_ref, wah_ref, wal_ref, ba_ref, ng_ref, tri_ref, o_ref, st_ref):
    H, dk, dv, C = GLA_HEADS, GLA_DK, GLA_DV, GLA_CHUNK

    @pl.when(pl.program_id(1) == 0)
    def _():
        st_ref[...] = jnp.zeros_like(st_ref)

    T = qkvr_ref.shape[0]
    a = a_ref[...]
    ah = a.astype(BF16)
    al = (a - ah.astype(F32)).astype(BF16)
    glog = (jnp.dot(ah, wah_ref[...], preferred_element_type=F32)
            + jnp.dot(al, wah_ref[...], preferred_element_type=F32)
            + jnp.dot(ah, wal_ref[...], preferred_element_type=F32)) + ba_ref[...]
    log_a = jax.nn.log_sigmoid(glog) / GLA_TAU
    tri = tri_ref[...]
    rr = lax.broadcasted_iota(jnp.int32, (C, C), 0)
    cc = lax.broadcasted_iota(jnp.int32, (C, C), 1)
    causal = rr >= cc
    ng = ng_ref[...]
    ct = (((1,), (1,)), ((), ()))
    c0 = (((0,), (0,)), ((), ()))
    for c in range(T // C):
        rows = slice(c * C, (c + 1) * C)
        for h in range(H):
            la = log_a[rows, h * dk:(h + 1) * dk]
            p1, p2, p3 = _split3(la)
            b = (jnp.dot(tri, p1, preferred_element_type=F32) + jnp.dot(tri, p2, preferred_element_type=F32)
                 + jnp.dot(tri, p3, preferred_element_type=F32))
            b_last = b[C - 1:C, :]
            q = qkvr_ref[rows, h * dk:(h + 1) * dk].astype(F32)
            k = qkvr_ref[rows, H * dk + h * dk:H * dk + (h + 1) * dk].astype(F32)
            v = qkvr_ref[rows, 2 * H * dk + h * dv:2 * H * dk + (h + 1) * dv]
            r = qkvr_ref[rows, 2 * H * dk + H * dv + h * dv:2 * H * dk + H * dv + (h + 1) * dv].astype(F32)
            qg = (q * jnp.exp(b)).astype(BF16)
            kg = (k * jnp.exp(-b)).astype(BF16)
            kd = (k * jnp.exp(b_last - b)).astype(BF16)
            att = jnp.where(causal, lax.dot_general(qg, kg, ct, preferred_element_type=F32), 0.0)
            st = st_ref[h]
            o = (jnp.dot(att.astype(BF16), v, preferred_element_type=F32)
                 + lax.dot_general(qg, st.astype(BF16), ct, preferred_element_type=F32))
            st_ref[h] = jnp.exp(b_last) * st + lax.dot_general(v, kd, c0, preferred_element_type=F32)
            o = o * lax.rsqrt(jnp.mean(o * o, axis=-1, keepdims=True) + LN_EPS) * ng
            o_ref[rows, h * dv:(h + 1) * dv] = (o * jax.nn.silu(r)).astype(o_ref.dtype)


def gla_core(qkvr, a, w_a2, b_a, norm_g, B, S, tile=256):
    H, dk, dv, C = GLA_HEADS, GLA_DK, GLA_DV, GLA_CHUNK
    N, W = qkvr.shape
    tile = _pick(S, tile)
    nt = S // tile
    wa = jnp.zeros((LANES, H * dk), F32).at[:GLA_GATE_RANK].set(w_a2)
    wah = wa.astype(BF16)
    wal = (wa - wah.astype(F32)).astype(BF16)
    tri = (jnp.arange(C)[:, None] >= jnp.arange(C)[None, :]).astype(BF16)
    row = lambda b, t: (b * nt + t, 0)
    fix = lambda b, t: (0, 0)
    return pl.pallas_call(
        _gla_kernel, out_shape=jax.ShapeDtypeStruct((N, H * dv), BF16), grid=(B, nt),
        in_specs=[pl.BlockSpec((tile, W), row), pl.BlockSpec((tile, LANES), row),
                  pl.BlockSpec((LANES, H * dk), fix), pl.BlockSpec((LANES, H * dk), fix),
                  pl.BlockSpec((1, H * dk), fix), pl.BlockSpec((1, dv), fix), pl.BlockSpec((C, C), fix)],
        out_specs=pl.BlockSpec((tile, H * dv), row),
        scratch_shapes=[pltpu.VMEM((H, dv, dk), F32)],
        compiler_params=_cparams(("parallel", "arbitrary")), name="gla_core",
    )(qkvr, a, wah, wal, b_a.reshape(1, H * dk), norm_g.reshape(1, dv), tri)


def gla_mixer(xb, w_in, w_a2, b_a, norm_g, B, S):
    H, dk, dv = GLA_HEADS, GLA_DK, GLA_DV
    hk, hv = H * dk, H * dv
    wq, wk, wv, wa, wr = jnp.split(w_in, [hk, 2 * hk, 2 * hk + hv, 2 * hk + hv + GLA_GATE_RANK], axis=1)
    w_main = jnp.concatenate([wq * dk ** -0.5, wk, wv, wr], axis=1).astype(BF16)
    w_gate = jnp.zeros((w_in.shape[0], LANES), F32).at[:, :GLA_GATE_RANK].set(wa).astype(BF16)
    qkvr = matmul(xb, w_main, out_dtype=BF16)
    a = matmul(xb, w_gate, out_dtype=F32)
    return gla_core(qkvr, a, w_a2, b_a, norm_g, B, S)


DIL_HEADS = D_MODEL // HEAD_DIM
DIL_CONFIGS = ((128, 1), (512, 4), (2048, 16))
DIL_TQ = 128


def _dil_attn_kernel(q_ref, kc_ref, kp_ref, vc_ref, vp_ref, o_ref, lse_ref, *, tiles_per_seq, window):
    i = pl.program_id(0)
    tq = q_ref.shape[0]
    first = (i % tiles_per_seq) == 0
    qpos = lax.broadcasted_iota(jnp.int32, (tq, 2 * tq), 0) + tq
    kpos = lax.broadcasted_iota(jnp.int32, (tq, 2 * tq), 1)
    dist = qpos - kpos
    mask = (dist >= 0) & (dist <= window) & ((kpos >= tq) | jnp.logical_not(first))
    lane = lax.broadcasted_iota(jnp.int32, (tq, LANES), 1)
    lo = lane < HEAD_DIM
    ct = (((1,), (1,)), ((), ()))
    for p in range(q_ref.shape[1] // LANES):
        cols = slice(p * LANES, (p + 1) * LANES)
        q2 = q_ref[:, cols]
        k2 = jnp.concatenate([kp_ref[:, cols], kc_ref[:, cols]], axis=0)
        v2 = jnp.concatenate([vp_ref[:, cols], vc_ref[:, cols]], axis=0)
        outs, lses = [], []
        for half in (lo, jnp.logical_not(lo)):
            qm = jnp.where(half, q2, jnp.zeros_like(q2))
            s = jnp.where(mask, lax.dot_general(qm, k2, ct, preferred_element_type=F32), NEG_INF)
            m = jnp.max(s, axis=-1, keepdims=True)
            e = jnp.where(mask, jnp.exp(s - m), 0.0)
            den = jnp.sum(e, axis=-1, keepdims=True)
            pv = jnp.dot(e.astype(BF16), v2, preferred_element_type=F32)
            outs.append(pv / jnp.maximum(den, 1e-30))
            lses.append(m + jnp.log(den))
        o_ref[:, cols] = jnp.where(lo, outs[0], outs[1]).astype(o_ref.dtype)
        lse_ref[:, cols] = jnp.where(lo, lses[0], lses[1])


def dil_attention(qk, v, seq_len, window):
    N, W = v.shape
    tq = DIL_TQ
    assert window == tq and seq_len % tq == 0
    nwb = 1
    cur = lambda i: (i, 0)
    prev = lambda i: (jnp.maximum(i - 1, 0), 0)
    return pl.pallas_call(
        functools.partial(_dil_attn_kernel, tiles_per_seq=seq_len // tq, window=window),
        out_shape=(jax.ShapeDtypeStruct((N, W), BF16), jax.ShapeDtypeStruct((N, W), F32)),
        grid=(N // tq,),
        in_specs=[pl.BlockSpec((tq, W), cur),
                  pl.BlockSpec((tq, W), lambda i: (i, nwb)),
                  pl.BlockSpec((tq, W), lambda i: (jnp.maximum(i - 1, 0), nwb)),
                  pl.BlockSpec((tq, W), cur), pl.BlockSpec((tq, W), prev)],
        out_specs=(pl.BlockSpec((tq, W), cur), pl.BlockSpec((tq, W), cur)),
        compiler_params=_cparams(("parallel",)), name="dil_attn",
    )(qk, qk, qk, v, v)


def _dil_merge_kernel(o0, o1, o2, l0, l1, l2, w_ref, x_ref, g_ref, b_ref, out_ref, outb_ref):
    la, lb, lc = l0[...], l1[...], l2[...]
    m = jnp.maximum(jnp.maximum(la, lb), lc)
    ea, eb, ec = jnp.exp(la - m), jnp.exp(lb - m), jnp.exp(lc - m)
    tot = ea + eb + ec
    o = (ea / tot) * o0[...].astype(F32) + (eb / tot) * o1[...].astype(F32) + (ec / tot) * o2[...].astype(F32)
    h = jnp.dot(o.astype(BF16), w_ref[...], preferred_element_type=F32)
    y = _ln(DN_ALPHA * x_ref[...] + h, g_ref[...], b_ref[...])
    out_ref[...] = y
    outb_ref[...] = y.astype(BF16)


def dil_merge_out(os_, lses, w_out, x, g, b, tm=256):
    N, D = x.shape
    W = w_out.shape[0]
    tm = _pick(N, tm)
    row = lambda i: (i, 0)
    fix = lambda i: (0, 0)
    rs = pl.BlockSpec((tm, W), row)
    return pl.pallas_call(
        _dil_merge_kernel,
        out_shape=(jax.ShapeDtypeStruct((N, D), F32), jax.ShapeDtypeStruct((N, D), BF16)),
        grid=(N // tm,),
        in_specs=[rs, rs, rs, rs, rs, rs, pl.BlockSpec((W, D), fix), pl.BlockSpec((tm, D), row),
                  pl.BlockSpec((1, D), fix), pl.BlockSpec((1, D), fix)],
        out_specs=(pl.BlockSpec((tm, D), row), pl.BlockSpec((tm, D), row)),
        compiler_params=_cparams(("parallel",)), name="dil_merge",
    )(*os_, *lses, w_out, x, g.reshape(1, D), b.reshape(1, D))


def dilated_layer(x, xb, w_in, w_out, ln_g, ln_b, B, S):
    N, D = x.shape
    H, dh = DIL_HEADS, HEAD_DIM
    W = H * dh
    w6 = w_in.reshape(D, len(DIL_CONFIGS), 3, W)
    pos = jnp.arange(S)
    os_, lses = [], []
    for gi, (window, dil) in enumerate(DIL_CONFIGS):
        L = S // dil
        wqk = jnp.concatenate([w6[:, gi, 0] * dh ** -0.5, w6[:, gi, 1]], axis=1).astype(BF16)
        wv = w6[:, gi, 2].astype(BF16)
        xp = xb.reshape(B, L, dil, D).transpose(0, 2, 1, 3).reshape(N, D) if dil > 1 else xb
        ppos = pos.reshape(L, dil).T.reshape(S)
        qk = matmul(xp, wqk, out_dtype=BF16, rope=rope_tables(ppos))
        v = matmul(xp, wv, out_dtype=BF16)
        o, lse = dil_attention(qk, v, L, window // dil)
        if dil > 1:
            o = o.reshape(B, dil, L, W).transpose(0, 2, 1, 3).reshape(N, W)
            lse = lse.reshape(B, dil, L, W).transpose(0, 2, 1, 3).reshape(N, W)
        os_.append(o)
        lses.append(lse)
    return dil_merge_out(os_, lses, w_out.astype(BF16), x, ln_g, ln_b)


NSA_HEADS = D_MODEL // HEAD_DIM
NSA_KV_HEADS = 4
NSA_GROUP = NSA_HEADS // NSA_KV_HEADS
CMP_LEN = 32
CMP_STRIDE = 16
CMP_HIDDEN = 256
SEL_LEN = 64
SEL_TOPN = 16
NSA_WINDOW = 512
FORCE_BONUS = 100.0
NSA_TQ = 128
NSA_TK = 128
NSA_CHUNK = 512


def _compress_kernel(x_ref, p_ref, w1a_ref, w1b_ref, w2_ref, o_ref, *, transpose_out):
    x = x_ref[0].astype(F32)
    n = x.shape[0]
    first = jnp.dot((x + p_ref[0:1, :]).astype(BF16), w1a_ref[...], preferred_element_type=F32)
    second = jnp.dot((x + p_ref[1:2, :]).astype(BF16), w1b_ref[...], preferred_element_type=F32)
    hid = first + pltpu.roll(second, n - 1, 0)
    out = jnp.dot(jax.nn.gelu(hid).astype(BF16), w2_ref[...], preferred_element_type=F32)
    if transpose_out:
        o_ref[0] = out.T[:HEAD_DIM, :].astype(o_ref.dtype)
    else:
        o_ref[0] = out[:, :HEAD_DIM].astype(o_ref.dtype)


def nsa_compress(t, pos_emb, w1, w2, transpose_out):
    BK, n, W = t.shape
    half = CMP_STRIDE * HEAD_DIM
    p = pos_emb.reshape(2, half).astype(F32)
    w2p = jnp.zeros((CMP_HIDDEN, LANES), F32).at[:, :HEAD_DIM].set(w2).astype(BF16)
    oshape = (BK, HEAD_DIM, n) if transpose_out else (BK, n, HEAD_DIM)
    fix = lambda i: (0, 0)
    return pl.pallas_call(
        functools.partial(_compress_kernel, transpose_out=transpose_out),
        out_shape=jax.ShapeDtypeStruct(oshape, BF16), grid=(BK,),
        in_specs=[pl.BlockSpec((1, n, W), lambda i: (i, 0, 0)), pl.BlockSpec((2, half), fix),
                  pl.BlockSpec((half, CMP_HIDDEN), fix), pl.BlockSpec((half, CMP_HIDDEN), fix),
                  pl.BlockSpec((CMP_HIDDEN, LANES), fix)],
        out_specs=pl.BlockSpec((1,) + oshape[1:], lambda i: (i, 0, 0)),
        compiler_params=_cparams(("parallel",)), name="nsa_compress",
    )(t, p, w1[:half].astype(BF16), w1[half:].astype(BF16), w2p)


def _col_softmax_step(carry, s, valid, vt):
    m, l, acc = carry
    m_new = jnp.maximum(m, jnp.max(s, axis=0, keepdims=True))
    alpha = jnp.exp(m - m_new)
    p = jnp.exp(s - m_new)
    if valid is not None:
        p = jnp.where(valid, p, 0.0)
    l = alpha * l + jnp.sum(p, axis=0, keepdims=True)
    acc = alpha * acc + jnp.dot(vt, p.astype(BF16), preferred_element_type=F32)
    return m_new, l, acc


def _nsa_attn_t_kernel(q_ref, kc_ref, vct_ref, ks_ref, vst_ref, kw_ref, vwt_ref, oh_ref, gl_ref, ovl_ref, o_ref):
    kh = pl.program_id(1)
    i = pl.program_id(2)
    tq, dh, G = NSA_TQ, HEAD_DIM, NSA_GROUP
    tk = NSA_TK
    M = G * tq
    t0 = i * tq
    slot = kh % 2
    vrows = pl.ds(pl.multiple_of(slot * dh, dh), dh)

    def tpos(shape):
        return t0 + (lax.broadcasted_iota(jnp.int32, shape, 1) & (tq - 1))

    qn = q_ref[...].astype(F32)
    qt_pairs = [qn[:, c * LANES:(c + 1) * LANES].T for c in range(G * dh // LANES)]
    qt = jnp.concatenate([p[h * dh:(h + 1) * dh] for p in qt_pairs for h in range(LANES // dh)], axis=1)
    qt = qt.astype(BF16)

    n_cmp = kc_ref.shape[1]
    s_c = jnp.dot(kc_ref[0], qt, preferred_element_type=F32)
    cend = lax.broadcasted_iota(jnp.int32, (n_cmp, M), 0) * CMP_STRIDE + (CMP_LEN - 1)
    vis = cend <= tpos((n_cmp, M))
    s_c = jnp.where(vis, s_c, NEG_INF)
    e_c = jnp.where(vis, jnp.exp(s_c - jnp.max(s_c, axis=0, keepdims=True)), 0.0)
    p_c = e_c / jnp.maximum(jnp.sum(e_c, axis=0, keepdims=True), 1e-30)
    o_c = jnp.dot(vct_ref[0], p_c.astype(BF16), preferred_element_type=F32)

    psum = p_c[:, 0:tq]
    for g in range(1, G):
        psum = psum + p_c[:, g * tq:(g + 1) * tq]
    ovl = ovl_ref[...]
    imp = sum(jnp.dot(ovl, piece, preferred_element_type=F32) for piece in _split3(psum))
    n_sel = imp.shape[0]
    blk = lax.broadcasted_iota(jnp.int32, (n_sel, tq), 0)
    cur = tpos((n_sel, tq)) // SEL_LEN
    forced = (blk == 0) | (blk == cur) | (blk == cur - 1)
    score = jnp.where(blk <= cur, imp + jnp.where(forced, FORCE_BONUS, 0.0), -1.0)
    chosen = jnp.zeros((n_sel, tq), jnp.bool_)
    for _ in range(min(SEL_TOPN, n_sel)):
        mx, idx = _first_index_of_max(score, blk, 0, n_sel)
        hit = blk == idx
        chosen = chosen | (hit & (mx >= 0.0))
        score = jnp.where(hit, -2.0, score)
    bias = jnp.where(chosen, 0.0, NEG_INF).astype(BF16)
    zero = jnp.zeros_like(qt)
    q_pair = jnp.concatenate([jnp.where(slot == 0, qt, zero), jnp.where(slot == 1, qt, zero)], axis=0)
    pad = jnp.zeros((LANES - n_sel, M), BF16)
    q_aug = jnp.concatenate([q_pair, jnp.concatenate([bias] * G, axis=1), pad], axis=0)

    init = (jnp.full((1, M), NEG_INF, F32), jnp.zeros((1, M), F32), jnp.zeros((dh, M), F32))

    def vt_cat(ref, first_tile, n):
        return jnp.concatenate([ref[0, 0, first_tile + j, vrows, :] for j in range(n)], axis=1)

    ch = NSA_CHUNK
    per_chunk = ch // tk

    def sel_chunk(c, carry, diagonal):
        rows = pl.ds(pl.multiple_of(c * ch, ch), ch)
        k_aug = jnp.concatenate([ks_ref[rows, :], oh_ref[rows, :]], axis=1)
        s = jnp.dot(k_aug, q_aug, preferred_element_type=F32)
        valid = None
        if diagonal:
            kpos = c * ch + lax.broadcasted_iota(jnp.int32, (ch, M), 0)
            valid = kpos <= tpos((ch, M))
            s = jnp.where(valid, s, NEG_INF)
        return _col_softmax_step(carry, s, valid, vt_cat(vst_ref, c * per_chunk, per_chunk))

    n_full = t0 // ch
    carry = lax.fori_loop(0, n_full, lambda c, cr: sel_chunk(c, cr, False), init)
    _, l_s, acc_s = sel_chunk(n_full, carry, True)
    o_s = acc_s / jnp.maximum(l_s, 1e-30)

    n_wt = NSA_WINDOW // tk + 1
    wt0 = jnp.maximum(i + 1 - n_wt, 0)
    wrows = pl.ds(pl.multiple_of(wt0 * tk, tk), n_wt * tk)
    s_w = jnp.dot(kw_ref[wrows, :], q_pair, preferred_element_type=F32)
    dist = tpos(s_w.shape) - (wt0 * tk + lax.broadcasted_iota(jnp.int32, s_w.shape, 0))
    near = (dist >= 0) & (dist < NSA_WINDOW)
    s_w = jnp.where(near, s_w, NEG_INF)
    e_w = jnp.where(near, jnp.exp(s_w - jnp.max(s_w, axis=0, keepdims=True)), 0.0)
    l_w = jnp.sum(e_w, axis=0, keepdims=True)
    o_w = jnp.dot(vt_cat(vwt_ref, wt0, n_wt), e_w.astype(BF16), preferred_element_type=F32) / jnp.maximum(l_w, 1e-30)

    gates = jax.nn.sigmoid(gl_ref[...].T)
    outs = []
    for g in range(G):
        cols = slice(g * tq, (g + 1) * tq)
        outs.append(gates[g:g + 1] * o_c[:, cols] + gates[G + g:G + g + 1] * o_s[:, cols]
                    + gates[2 * G + g:2 * G + g + 1] * o_w[:, cols])
    per = LANES // dh
    o_ref[...] = jnp.concatenate(
        [jnp.concatenate(outs[c * per:(c + 1) * per], axis=0).T for c in range(G // per)], axis=1).astype(o_ref.dtype)


def nsa_attention_t(roped, vst, vwt, k_cmp, v_cmpt, gl, B, S):
    H, KH, G, dh = NSA_HEADS, NSA_KV_HEADS, NSA_GROUP, HEAD_DIM
    tq, tk = NSA_TQ, NSA_TK
    assert tq == tk and S % NSA_CHUNK == 0 and NSA_CHUNK % tk == 0 and S >= NSA_WINDOW + tq and G * dh == 2 * LANES
    nt = S // tq
    n_cmp = k_cmp.shape[1]
    n_sel = S // SEL_LEN
    c0 = np.arange(n_cmp)[None, :] * CMP_STRIDE
    s0 = np.arange(n_sel)[:, None] * SEL_LEN
    ovl = jnp.asarray((c0 < s0 + SEL_LEN) & (c0 + CMP_LEN - 1 >= s0), BF16)
    onehot = jnp.asarray(np.arange(S)[:, None] // SEL_LEN == np.arange(LANES)[None, :], BF16)
    ks_col = (H * dh + KH * dh) // LANES
    kw_col = (H * dh + 2 * KH * dh) // LANES
    qspec = pl.BlockSpec((tq, G * dh), lambda b, h, i: (b * nt + i, h))
    vspec = pl.BlockSpec((1, 1, nt, LANES, tk), lambda b, h, i: (b, h // 2, 0, 0, 0))
    return pl.pallas_call(
        _nsa_attn_t_kernel, out_shape=jax.ShapeDtypeStruct((B * S, H * dh), BF16), grid=(B, KH, nt),
        in_specs=[qspec,
                  pl.BlockSpec((1, n_cmp, dh), lambda b, h, i: (b * KH + h, 0, 0)),
                  pl.BlockSpec((1, dh, n_cmp), lambda b, h, i: (b * KH + h, 0, 0)),
                  pl.BlockSpec((S, LANES), lambda b, h, i: (b, ks_col + h // 2)), vspec,
                  pl.BlockSpec((S, LANES), lambda b, h, i: (b, kw_col + h // 2)), vspec,
                  pl.BlockSpec((S, LANES), lambda b, h, i: (0, 0)),
                  pl.BlockSpec((tq, LANES), lambda b, h, i: (b * nt + i, h)),
                  pl.BlockSpec((n_sel, n_cmp), lambda b, h, i: (0, 0))],
        out_specs=qspec,
        compiler_params=_cparams(("parallel", "parallel", "arbitrary")), name="nsa_attn",
    )(roped, k_cmp, v_cmpt, roped, vst, roped, vwt, onehot, gl, ovl)


def nsa_mixer(xb, w_in, ck_pos, ck_w1, ck_w2, cv_pos, cv_w1, cv_w2, B, S):
    H, KH, G, dh = NSA_HEADS, NSA_KV_HEADS, NSA_GROUP, HEAD_DIM
    kvw = KH * dh
    cuts = np.cumsum([H * dh, kvw, kvw, kvw, kvw, kvw, kvw]).tolist()
    wq, wkc, wvc, wks, wvs, wkw, wvw, wgl = jnp.split(w_in, cuts, axis=1)
    w_rope = jnp.concatenate([wq * dh ** -0.5, wkc, wks, wkw], axis=1).astype(BF16)
    w_val = jnp.concatenate([wvc, wvs, wvw], axis=1).astype(BF16)
    w_gate = jnp.zeros((w_in.shape[0], KH, LANES), F32).at[:, :, :3 * G].set(
        wgl.reshape(-1, KH, G, 3).transpose(0, 1, 3, 2).reshape(-1, KH, 3 * G)).reshape(-1, KH * LANES).astype(BF16)
    roped = matmul(xb, w_rope, out_dtype=BF16, rope=rope_tables(jnp.arange(S)), tn=256)
    vals = matmul(xb, w_val, out_dtype=BF16, tn=256)
    gl = matmul(xb, w_gate, out_dtype=F32)

    n16 = S // CMP_STRIDE

    def blocks16(t):
        return t.reshape(B, n16, CMP_STRIDE, KH, dh).transpose(0, 3, 1, 2, 4).reshape(B * KH, n16, CMP_STRIDE * dh)

    def vt_tiles(t):
        return t.reshape(B, S // NSA_TK, NSA_TK, KH * dh // LANES, LANES).transpose(0, 3, 1, 4, 2)

    k_cmp = nsa_compress(blocks16(roped[:, H * dh:H * dh + kvw]), ck_pos, ck_w1, ck_w2, False)
    v_cmpt = nsa_compress(blocks16(vals[:, :kvw]), cv_pos, cv_w1, cv_w2, True)
    return nsa_attention_t(roped, vt_tiles(vals[:, kvw:2 * kvw]), vt_tiles(vals[:, 2 * kvw:]), k_cmp, v_cmpt, gl, B, S)


def kernel(x, l0_nsa_w_in, l0_nsa_w_out, l0_nsa_ck_pos, l0_nsa_ck_w1, l0_nsa_ck_w2, l0_nsa_cv_pos, l0_nsa_cv_w1, l0_nsa_cv_w2, l0_ln1_g, l0_ln1_b, l0_router_w, l0_router_b, l0_moe_w_in, l0_moe_w_out, l0_shared_w_in, l0_shared_w_out, l0_ln2_g, l0_ln2_b, l1_gla_w_in, l1_gla_w_a2, l1_gla_b_a, l1_gla_norm_g, l1_gla_w_out, l1_ln1_g, l1_ln1_b, l1_router_w, l1_router_b, l1_moe_w_in, l1_moe_w_out, l1_shared_w_in, l1_shared_w_out, l1_ln2_g, l1_ln2_b, l2_dil_w_in, l2_dil_w_out, l2_ln1_g, l2_ln1_b, l2_router_w, l2_router_b, l2_moe_w_in, l2_moe_w_out, l2_shared_w_in, l2_shared_w_out, l2_ln2_g, l2_ln2_b, l3_nsa_w_in, l3_nsa_w_out, l3_nsa_ck_pos, l3_nsa_ck_w1, l3_nsa_ck_w2, l3_nsa_cv_pos, l3_nsa_cv_w1, l3_nsa_cv_w2, l3_ln1_g, l3_ln1_b, l3_router_w, l3_router_b, l3_moe_w_in, l3_moe_w_out, l3_shared_w_in, l3_shared_w_out, l3_ln2_g, l3_ln2_b):
    B, S, D = x.shape
    xf = x.reshape(B * S, D)
    xb = xf.astype(BF16)

    h = nsa_mixer(xb, l0_nsa_w_in, l0_nsa_ck_pos, l0_nsa_ck_w1, l0_nsa_ck_w2, l0_nsa_cv_pos, l0_nsa_cv_w1, l0_nsa_cv_w2, B, S)
    xf, xb = matmul_res_ln(h, l0_nsa_w_out.astype(BF16), xf, l0_ln1_g, l0_ln1_b)
    xf, xb = moe_layer(xf, l0_router_w, l0_router_b, l0_moe_w_in, l0_moe_w_out, l0_shared_w_in, l0_shared_w_out, l0_ln2_g, l0_ln2_b)

    h = gla_mixer(xb, l1_gla_w_in, l1_gla_w_a2, l1_gla_b_a, l1_gla_norm_g, B, S)
    xf, xb = matmul_res_ln(h, l1_gla_w_out.astype(BF16), xf, l1_ln1_g, l1_ln1_b)
    xf, xb = moe_layer(xf, l1_router_w, l1_router_b, l1_moe_w_in, l1_moe_w_out, l1_shared_w_in, l1_shared_w_out, l1_ln2_g, l1_ln2_b)

    xf, xb = dilated_layer(xf, xb, l2_dil_w_in, l2_dil_w_out, l2_ln1_g, l2_ln1_b, B, S)
    xf, xb = moe_layer(xf, l2_router_w, l2_router_b, l2_moe_w_in, l2_moe_w_out, l2_shared_w_in, l2_shared_w_out, l2_ln2_g, l2_ln2_b)

    h = nsa_mixer(xb, l3_nsa_w_in, l3_nsa_ck_pos, l3_nsa_ck_w1, l3_nsa_ck_w2, l3_nsa_cv_pos, l3_nsa_cv_w1, l3_nsa_cv_w2, B, S)
    xf, xb = matmul_res_ln(h, l3_nsa_w_out.astype(BF16), xf, l3_ln1_g, l3_ln1_b)
    xf, xb = moe_layer(xf, l3_router_w, l3_router_b, l3_moe_w_in, l3_moe_w_out, l3_shared_w_in, l3_shared_w_out, l3_ln2_g, l3_ln2_b)
    return xf.reshape(B, S, D)
```

```python
import functools
import math

import jax
import jax.numpy as jnp
import numpy as np
from jax import lax
from jax.experimental import pallas as pl
from jax.experimental.pallas import tpu as pltpu

F32 = jnp.float32
BF16 = jnp.bfloat16

D_MODEL = 1024
DEPTH = 4
HEAD_DIM = 64
ROPE_THETA = 500000.0
ROPE_DIM = HEAD_DIM // 4
ROPE_HALF = ROPE_DIM // 2

N_EXPERTS = 64
TOP_K = 8
N_GROUPS = 8
TOPK_GROUPS = 4
GROUP_SIZE = N_EXPERTS // N_GROUPS
D_EXPERT = 256
ROUTED_SCALE = 2.5

DN_ALPHA = (2.0 * DEPTH) ** 0.25
LN_EPS = 1e-5
NEG_INF = -1e30

LANES = 128
VMEM_LIMIT = 48 * 1024 * 1024


def _cparams(sem, **kw):
    return pltpu.CompilerParams(dimension_semantics=sem, vmem_limit_bytes=VMEM_LIMIT, **kw)


def _pick(n, pref):
    t = min(pref, n)
    while n % t:
        t //= 2
    return t


def _mm_kernel(x_ref, w_ref, o_ref):
    o_ref[...] = jnp.dot(x_ref[...], w_ref[...], preferred_element_type=F32).astype(o_ref.dtype)


def _mm_rope_kernel(x_ref, w_ref, c_ref, sm_ref, sp_ref, o_ref):
    y = jnp.dot(x_ref[...], w_ref[...], preferred_element_type=F32)
    reps = y.shape[1] // LANES
    c = jnp.tile(c_ref[...], (1, reps))
    sm = jnp.tile(sm_ref[...], (1, reps))
    sp = jnp.tile(sp_ref[...], (1, reps))
    up = pltpu.roll(y, y.shape[1] - ROPE_HALF, 1)
    dn = pltpu.roll(y, ROPE_HALF, 1)
    o_ref[...] = (y * c + up * sm + dn * sp).astype(o_ref.dtype)


def matmul(x, w, out_dtype=F32, rope=None, tm=None, tn=1024):
    M, K = x.shape
    N = w.shape[1]
    tm = _pick(M, tm or (1024 if rope is None else 512))
    tn = _pick(N, tn)
    grid = (N // tn, M // tm)
    x_spec = pl.BlockSpec((tm, K), lambda j, i: (i, 0))
    w_spec = pl.BlockSpec((K, tn), lambda j, i: (0, j))
    o_spec = pl.BlockSpec((tm, tn), lambda j, i: (i, j))
    if rope is None:
        return pl.pallas_call(
            _mm_kernel, out_shape=jax.ShapeDtypeStruct((M, N), out_dtype), grid=grid,
            in_specs=[x_spec, w_spec], out_specs=o_spec,
            compiler_params=_cparams(("parallel", "parallel")), name="mm")(x, w)
    R = rope[0].shape[0]
    tm = _pick(R, tm)
    grid = (N // tn, M // tm)
    x_spec = pl.BlockSpec((tm, K), lambda j, i: (i, 0))
    o_spec = pl.BlockSpec((tm, tn), lambda j, i: (i, j))
    nr = R // tm
    t_spec = pl.BlockSpec((tm, LANES), lambda j, i: (i % nr, 0))
    return pl.pallas_call(
        _mm_rope_kernel, out_shape=jax.ShapeDtypeStruct((M, N), out_dtype), grid=grid,
        in_specs=[x_spec, w_spec, t_spec, t_spec, t_spec], out_specs=o_spec,
        compiler_params=_cparams(("parallel", "parallel")), name="mm_rope")(x, w, *rope)


def rope_tables(pos):
    inv = ROPE_THETA ** (-jnp.arange(ROPE_HALF, dtype=F32) * 2.0 / ROPE_DIM)
    ang = pos.astype(F32)[:, None] * inv[None, :]
    cos, sin = jnp.cos(ang), jnp.sin(ang)
    n = pos.shape[0]
    ones = jnp.ones((n, HEAD_DIM - ROPE_DIM), F32)
    zeros = jnp.zeros((n, HEAD_DIM - ROPE_DIM), F32)
    zh = jnp.zeros((n, ROPE_HALF), F32)
    c = jnp.concatenate([cos, cos, ones], 1)
    sm = jnp.concatenate([-sin, zh, zeros], 1)
    sp = jnp.concatenate([zh, sin, zeros], 1)
    return tuple(jnp.tile(t, (1, LANES // HEAD_DIM)) for t in (c, sm, sp))


def _ln(v, g, b):
    mu = jnp.mean(v, axis=-1, keepdims=True)
    d = v - mu
    var = jnp.mean(d * d, axis=-1, keepdims=True)
    return d * lax.rsqrt(var + LN_EPS) * g + b


def _mm_res_ln_kernel(a_ref, w_ref, x_ref, g_ref, b_ref, o_ref, ob_ref):
    h = jnp.dot(a_ref[...], w_ref[...], preferred_element_type=F32)
    y = _ln(DN_ALPHA * x_ref[...] + h, g_ref[...], b_ref[...])
    o_ref[...] = y
    ob_ref[...] = y.astype(BF16)


def matmul_res_ln(a, w, x, g, b, tm=512):
    M, K = a.shape
    D = w.shape[1]
    tm = _pick(M, tm)
    row = lambda i: (i, 0)
    fix = lambda i: (0, 0)
    return pl.pallas_call(
        _mm_res_ln_kernel,
        out_shape=(jax.ShapeDtypeStruct((M, D), F32), jax.ShapeDtypeStruct((M, D), BF16)),
        grid=(M // tm,),
        in_specs=[pl.BlockSpec((tm, K), row), pl.BlockSpec((K, D), fix), pl.BlockSpec((tm, D), row),
                  pl.BlockSpec((1, D), fix), pl.BlockSpec((1, D), fix)],
        out_specs=(pl.BlockSpec((tm, D), row), pl.BlockSpec((tm, D), row)),
        compiler_params=_cparams(("parallel",)), name="mm_res_ln")(a, w, x, g.reshape(1, D), b.reshape(1, D))


def _first_index_of_max(v, iota, axis, n):
    m = jnp.max(v, axis=axis, keepdims=True)
    idx = jnp.min(jnp.where(v == m, iota, n), axis=axis, keepdims=True)
    return m, idx


def _router_kernel(x_ref, wh_ref, wl_ref, rb_ref, tri_ref, eidx_ref, gate_ref, rank_ref, cnt_ref, carry_ref):
    i = pl.program_id(0)

    @pl.when(i == 0)
    def _():
        carry_ref[...] = jnp.zeros_like(carry_ref)

    x = x_ref[...]
    xh = x.astype(BF16)
    xl = (x - xh.astype(F32)).astype(BF16)
    dn = (((1,), (1,)), ((), ()))
    logits = (lax.dot_general(wh_ref[...], xh, dn, preferred_element_type=F32)
              + lax.dot_general(wh_ref[...], xl, dn, preferred_element_type=F32)
              + lax.dot_general(wl_ref[...], xh, dn, preferred_element_type=F32))
    tm = logits.shape[1]
    s = jax.nn.sigmoid(logits)
    sb = s + rb_ref[...]
    sb3 = sb.reshape(N_GROUPS, GROUP_SIZE, tm)
    io3 = lax.broadcasted_iota(jnp.int32, sb3.shape, 1)
    m1, i1 = _first_index_of_max(sb3, io3, 1, GROUP_SIZE)
    m2 = jnp.max(jnp.where(io3 == i1, -jnp.inf, sb3), axis=1, keepdims=True)
    gs = (m1 + m2).reshape(N_GROUPS, tm)
    iog = lax.broadcasted_iota(jnp.int32, gs.shape, 0)
    gmask = jnp.zeros(gs.shape, jnp.bool_)
    for _ in range(TOPK_GROUPS):
        _, gi = _first_index_of_max(gs, iog, 0, N_GROUPS)
        pick = iog == gi
        gmask = gmask | pick
        gs = jnp.where(pick, -jnp.inf, gs)
    emask = jnp.broadcast_to(gmask.reshape(N_GROUPS, 1, tm), sb3.shape).reshape(N_EXPERTS, tm)
    cand = jnp.where(emask, sb, NEG_INF)
    ioe = lax.broadcasted_iota(jnp.int32, cand.shape, 0)
    sel = jnp.zeros(cand.shape, F32)
    picks, eidx, gates = [], [], []
    for _ in range(TOP_K):
        _, ei = _first_index_of_max(cand, ioe, 0, N_EXPERTS)
        pick = ioe == ei
        picks.append(pick)
        eidx.append(ei)
        gates.append(jnp.sum(jnp.where(pick, s, 0.0), axis=0, keepdims=True))
        sel = jnp.where(pick, 1.0, sel)
        cand = jnp.where(pick, -jnp.inf, cand)
    g = jnp.concatenate(gates, axis=0)
    gate_ref[...] = g / jnp.sum(g, axis=0, keepdims=True) * ROUTED_SCALE
    eidx_ref[...] = jnp.concatenate(eidx, axis=0)
    before = jnp.dot(sel.astype(BF16), tri_ref[...], preferred_element_type=F32) + carry_ref[...]
    rank_ref[...] = jnp.concatenate(
        [jnp.sum(jnp.where(p, before, 0.0), axis=0, keepdims=True) for p in picks], axis=0).astype(jnp.int32)
    carry_ref[...] = carry_ref[...] + jnp.sum(sel, axis=1, keepdims=True)
    cnt_ref[...] = jnp.broadcast_to(carry_ref[...], cnt_ref.shape).astype(jnp.int32)


def moe_router(x, router_w, router_b, tm=512):
    N, D = x.shape
    tm = _pick(N, tm)
    wt = router_w.T
    wh = wt.astype(BF16)
    wl = (wt - wh.astype(F32)).astype(BF16)
    tri = (jnp.arange(tm)[:, None] < jnp.arange(tm)[None, :]).astype(BF16)
    fix = lambda i: (0, 0)
    col = lambda i: (0, i)
    eidx, gate, rank, cnt = pl.pallas_call(
        _router_kernel,
        out_shape=(jax.ShapeDtypeStruct((TOP_K, N), jnp.int32), jax.ShapeDtypeStruct((TOP_K, N), F32),
                   jax.ShapeDtypeStruct((TOP_K, N), jnp.int32), jax.ShapeDtypeStruct((N_EXPERTS, LANES), jnp.int32)),
        grid=(N // tm,),
        in_specs=[pl.BlockSpec((tm, D), lambda i: (i, 0)), pl.BlockSpec((N_EXPERTS, D), fix),
                  pl.BlockSpec((N_EXPERTS, D), fix), pl.BlockSpec((N_EXPERTS, 1), fix), pl.BlockSpec((tm, tm), fix)],
        out_specs=(pl.BlockSpec((TOP_K, tm), col), pl.BlockSpec((TOP_K, tm), col), pl.BlockSpec((TOP_K, tm), col),
                   pl.BlockSpec((N_EXPERTS, LANES), fix)),
        scratch_shapes=[pltpu.VMEM((N_EXPERTS, 1), F32)],
        compiler_params=_cparams(("arbitrary",)), name="moe_router",
    )(x, wh, wl, router_b.reshape(N_EXPERTS, 1).astype(F32), tri)
    return eidx, gate, rank, cnt[:, 0]


def _dispatch_kernel(zrow_ref, dest_ref, x_ref, xs_ref, zbuf, sem, zsem):
    tm = x_ref.shape[0]
    tz = zbuf.shape[0]

    @pl.when(pl.program_id(0) == 0)
    def _():
        zbuf[...] = jnp.zeros_like(zbuf)

        def zcopy(j):
            return pltpu.make_async_copy(zbuf, xs_ref.at[pl.ds(pl.multiple_of(zrow_ref[j], tz), tz)], zsem)

        def zstart(j, c):
            @pl.when(zrow_ref[j] >= 0)
            def _():
                zcopy(j).start()
            return c

        def zwait(j, c):
            @pl.when(zrow_ref[j] >= 0)
            def _():
                zcopy(j).wait()
            return c

        lax.fori_loop(0, zrow_ref.shape[0], zstart, 0)
        lax.fori_loop(0, zrow_ref.shape[0], zwait, 0)

    def body(t, c):
        for k in range(TOP_K):
            pltpu.make_async_copy(x_ref.at[pl.ds(t, 1)], xs_ref.at[pl.ds(dest_ref[k, t], 1)], sem).start()
        return c

    lax.fori_loop(0, tm, body, 0)

    def wbody(t, c):
        for k in range(TOP_K):
            pltpu.make_async_copy(x_ref.at[pl.ds(t, 1)], xs_ref.at[pl.ds(dest_ref[k, t], 1)], sem).wait()
        return c

    lax.fori_loop(0, tm, wbody, 0)


def moe_dispatch(x, dest, zrow, P, tz, tm=512):
    N, D = x.shape
    tm = _pick(N, tm)
    grid_spec = pltpu.PrefetchScalarGridSpec(
        num_scalar_prefetch=1, grid=(N // tm,),
        in_specs=[pl.BlockSpec((TOP_K, tm), lambda i, z: (0, i), memory_space=pltpu.SMEM),
                  pl.BlockSpec((tm, D), lambda i, z: (i, 0))],
        out_specs=pl.BlockSpec(memory_space=pl.ANY),
        scratch_shapes=[pltpu.VMEM((tz, D), x.dtype), pltpu.SemaphoreType.DMA(()), pltpu.SemaphoreType.DMA(())])
    return pl.pallas_call(
        _dispatch_kernel, out_shape=jax.ShapeDtypeStruct((P, D), x.dtype), grid_spec=grid_spec,
        compiler_params=_cparams(("arbitrary",), has_side_effects=True), name="moe_dispatch",
    )(zrow, dest, x)


def _swiglu(x_bf16, w_in, w_out):
    h = jnp.dot(x_bf16, w_in, preferred_element_type=F32)
    f = h.shape[1] // 2
    a = jax.nn.silu(h[:, :f]) * h[:, f:]
    return jnp.dot(a.astype(BF16), w_out, preferred_element_type=F32)


def _expert_kernel(blk_e_ref, xs_ref, wi_ref, wo_ref, y_ref):
    del blk_e_ref
    y_ref[...] = _swiglu(xs_ref[...].astype(BF16), wi_ref[0], wo_ref[0])


def moe_experts(xs, blk_e, w_in_e, w_out_e, tm):
    P, D = xs.shape
    F2 = w_in_e.shape[2]
    grid_spec = pltpu.PrefetchScalarGridSpec(
        num_scalar_prefetch=1, grid=(P // tm,),
        in_specs=[pl.BlockSpec((tm, D), lambda i, be: (i, 0)),
                  pl.BlockSpec((1, D, F2), lambda i, be: (be[i], 0, 0)),
                  pl.BlockSpec((1, F2 // 2, D), lambda i, be: (be[i], 0, 0))],
        out_specs=pl.BlockSpec((tm, D), lambda i, be: (i, 0)))
    return pl.pallas_call(
        _expert_kernel, out_shape=jax.ShapeDtypeStruct((P, D), F32), grid_spec=grid_spec,
        compiler_params=_cparams(("arbitrary",)), name="moe_experts")(blk_e, xs, w_in_e, w_out_e)


def _combine_kernel(dest_ref, x_ref, gate_ref, wi_ref, wo_ref, g_ref, b_ref, y_ref, o_ref, ob_ref, buf, sem):
    tm = x_ref.shape[0]

    def copy(t, k):
        return pltpu.make_async_copy(y_ref.at[pl.ds(dest_ref[k, t], 1)], buf.at[k, pl.ds(t, 1)], sem)

    def body(t, c):
        for k in range(TOP_K):
            copy(t, k).start()
        return c

    lax.fori_loop(0, tm, body, 0)
    x = x_ref[...]
    shared = _swiglu(x.astype(BF16), wi_ref[...], wo_ref[...])

    def wbody(t, c):
        for k in range(TOP_K):
            copy(t, k).wait()
        return c

    lax.fori_loop(0, tm, wbody, 0)
    gate = gate_ref[...]
    routed = gate[:, 0:1] * buf[0]
    for k in range(1, TOP_K):
        routed = routed + gate[:, k:k + 1] * buf[k]
    y = _ln(DN_ALPHA * x + (routed + shared), g_ref[...], b_ref[...])
    o_ref[...] = y
    ob_ref[...] = y.astype(BF16)


def moe_combine(x, y, dest, gate_t, w_in_s, w_out_s, g, b, tm=128):
    N, D = x.shape
    tm = _pick(N, tm)
    F2 = w_in_s.shape[1]
    row = lambda i: (i, 0)
    fix = lambda i: (0, 0)
    return pl.pallas_call(
        _combine_kernel,
        out_shape=(jax.ShapeDtypeStruct((N, D), F32), jax.ShapeDtypeStruct((N, D), BF16)),
        grid=(N // tm,),
        in_specs=[pl.BlockSpec((TOP_K, tm), lambda i: (0, i), memory_space=pltpu.SMEM),
                  pl.BlockSpec((tm, D), row), pl.BlockSpec((tm, TOP_K), row),
                  pl.BlockSpec((D, F2), fix), pl.BlockSpec((F2 // 2, D), fix),
                  pl.BlockSpec((1, D), fix), pl.BlockSpec((1, D), fix),
                  pl.BlockSpec(memory_space=pl.ANY)],
        out_specs=(pl.BlockSpec((tm, D), row), pl.BlockSpec((tm, D), row)),
        scratch_shapes=[pltpu.VMEM((TOP_K, tm, D), F32), pltpu.SemaphoreType.DMA(())],
        compiler_params=_cparams(("arbitrary",)), name="moe_combine",
    )(dest, x, gate_t, w_in_s, w_out_s, g.reshape(1, D), b.reshape(1, D), y)


def moe_layer(x, router_w, router_b, w_in_e, w_out_e, w_in_s, w_out_s, ln_g, ln_b, tm_e=256):
    N, D = x.shape
    eidx, gate, rank, counts = moe_router(x, router_w, router_b)
    padded = (counts + tm_e - 1) // tm_e * tm_e
    pad_end = jnp.cumsum(padded)
    pad_start = pad_end - padded
    experts = jnp.arange(N_EXPERTS, dtype=jnp.int32)
    start_of = jnp.sum(jnp.where(eidx[:, :, None] == experts, pad_start, 0), axis=-1)
    dest = (start_of + rank).astype(jnp.int32)
    n_blk = (N * TOP_K) // tm_e + N_EXPERTS
    P = n_blk * tm_e
    blk_row = jnp.arange(n_blk, dtype=jnp.int32) * tm_e
    blk_e = jnp.minimum(jnp.sum(pad_end[None, :] <= blk_row[:, None], axis=1), N_EXPERTS - 1).astype(jnp.int32)
    tail = pad_end[-1] + experts * tm_e
    zrow = jnp.concatenate([jnp.where(padded > 0, pad_end - tm_e, -1),
                            jnp.where(tail < P, tail, -1)]).astype(jnp.int32)
    xs = moe_dispatch(x, dest, zrow, P, tm_e)
    ys = moe_experts(xs, blk_e, w_in_e.astype(BF16), w_out_e.astype(BF16), tm_e)
    return moe_combine(x, ys, dest, gate.T, w_in_s.astype(BF16), w_out_s.astype(BF16), ln_g, ln_b)


GLA_HEADS = 4
GLA_DK = D_MODEL // 2 // GLA_HEADS
GLA_DV = D_MODEL // GLA_HEADS
GLA_GATE_RANK = 16
GLA_TAU = 16.0
GLA_CHUNK = 64


def _split3(v):
    h1 = v.astype(BF16)
    r1 = v - h1.astype(F32)
    h2 = r1.astype(BF16)
    h3 = (r1 - h2.astype(F32)).astype(BF16)
    return h1, h2, h3


def _gla_kernel(qkvr_ref, a_ref, wah_ref, wal_ref, ba_ref, ng_ref, tri_ref, o_ref, st_ref):
    H, dk, dv, C = GLA_HEADS, GLA_DK, GLA_DV, GLA_CHUNK

    @pl.when(pl.program_id(1) == 0)
    def _():
        st_ref[...] = jnp.zeros_like(st_ref)

    T = qkvr_ref.shape[0]
    a = a_ref[...]
    ah = a.astype(BF16)
    al = (a - ah.astype(F32)).astype(BF16)
    glog = (jnp.dot(ah, wah_ref[...], preferred_element_type=F32)
            + jnp.dot(al, wah_ref[...], preferred_element_type=F32)
            + jnp.dot(ah, wal_ref[...], preferred_element_type=F32)) + ba_ref[...]
    log_a = jax.nn.log_sigmoid(glog) / GLA_TAU
    tri = tri_ref[...]
    rr = lax.broadcasted_iota(jnp.int32, (C, C), 0)
    cc = lax.broadcasted_iota(jnp.int32, (C, C), 1)
    causal = rr >= cc
    ng = ng_ref[...]
    ct = (((1,), (1,)), ((), ()))
    c0 = (((0,), (0,)), ((), ()))
    for c in range(T // C):
        rows = slice(c * C, (c + 1) * C)
        for h in range(H):
            la = log_a[rows, h * dk:(h + 1) * dk]
            p1, p2, p3 = _split3(la)
            b = (jnp.dot(tri, p1, preferred_element_type=F32) + jnp.dot(tri, p2, preferred_element_type=F32)
                 + jnp.dot(tri, p3, preferred_element_type=F32))
            b_last = b[C - 1:C, :]
            q = qkvr_ref[rows, h * dk:(h + 1) * dk].astype(F32)
            k = qkvr_ref[rows, H * dk + h * dk:H * dk + (h + 1) * dk].astype(F32)
            v = qkvr_ref[rows, 2 * H * dk + h * dv:2 * H * dk + (h + 1) * dv]
            r = qkvr_ref[rows, 2 * H * dk + H * dv + h * dv:2 * H * dk + H * dv + (h + 1) * dv].astype(F32)
            qg = (q * jnp.exp(b)).astype(BF16)
            kg = (k * jnp.exp(-b)).astype(BF16)
            kd = (k * jnp.exp(b_last - b)).astype(BF16)
            att = jnp.where(causal, lax.dot_general(qg, kg, ct, preferred_element_type=F32), 0.0)
            st = st_ref[h]
            o = (jnp.dot(att.astype(BF16), v, preferred_element_type=F32)
                 + lax.dot_general(qg, st.astype(BF16), ct, preferred_element_type=F32))
            st_ref[h] = jnp.exp(b_last) * st + lax.dot_general(v, kd, c0, preferred_element_type=F32)
            o = o * lax.rsqrt(jnp.mean(o * o, axis=-1, keepdims=True) + LN_EPS) * ng
            o_ref[rows, h * dv:(h + 1) * dv] = (o * jax.nn.silu(r)).astype(o_ref.dtype)


def gla_core(qkvr, a, w_a2, b_a, norm_g, B, S, tile=256):
    H, dk, dv, C = GLA_HEADS, GLA_DK, GLA_DV, GLA_CHUNK
    N, W = qkvr.shape
    tile = _pick(S, tile)
    nt = S // tile
    wa = jnp.zeros((LANES, H * dk), F32).at[:GLA_GATE_RANK].set(w_a2)
    wah = wa.astype(BF16)
    wal = (wa - wah.astype(F32)).astype(BF16)
    tri = (jnp.arange(C)[:, None] >= jnp.arange(C)[None, :]).astype(BF16)
    row = lambda b, t: (b * nt + t, 0)
    fix = lambda b, t: (0, 0)
    return pl.pallas_call(
        _gla_kernel, out_shape=jax.ShapeDtypeStruct((N, H * dv), BF16), grid=(B, nt),
        in_specs=[pl.BlockSpec((tile, W), row), pl.BlockSpec((tile, LANES), row),
                  pl.BlockSpec((LANES, H * dk), fix), pl.BlockSpec((LANES, H * dk), fix),
                  pl.BlockSpec((1, H * dk), fix), pl.BlockSpec((1, dv), fix), pl.BlockSpec((C, C), fix)],
        out_specs=pl.BlockSpec((tile, H * dv), row),
        scratch_shapes=[pltpu.VMEM((H, dv, dk), F32)],
        compiler_params=_cparams(("parallel", "arbitrary")), name="gla_core",
    )(qkvr, a, wah, wal, b_a.reshape(1, H * dk), norm_g.reshape(1, dv), tri)


def gla_mixer(xb, w_in, w_a2, b_a, norm_g, B, S):
    H, dk, dv = GLA_HEADS, GLA_DK, GLA_DV
    hk, hv = H * dk, H * dv
    wq, wk, wv, wa, wr = jnp.split(w_in, [hk, 2 * hk, 2 * hk + hv, 2 * hk + hv + GLA_GATE_RANK], axis=1)
    w_main = jnp.concatenate([wq * dk ** -0.5, wk, wv, wr], axis=1).astype(BF16)
    w_gate = jnp.zeros((w_in.shape[0], LANES), F32).at[:, :GLA_GATE_RANK].set(wa).astype(BF16)
    qkvr = matmul(xb, w_main, out_dtype=BF16)
    a = matmul(xb, w_gate, out_dtype=F32)
    return gla_core(qkvr, a, w_a2, b_a, norm_g, B, S)


DIL_HEADS = D_MODEL // HEAD_DIM
DIL_CONFIGS = ((128, 1), (512, 4), (2048, 16))
DIL_TQ = 128


def _dil_attn_kernel(q_ref, kc_ref, kp_ref, vc_ref, vp_ref, o_ref, lse_ref, *, tiles_per_seq, window):
    i = pl.program_id(0)
    tq = q_ref.shape[0]
    first = (i % tiles_per_seq) == 0
    qpos = lax.broadcasted_iota(jnp.int32, (tq, 2 * tq), 0) + tq
    kpos = lax.broadcasted_iota(jnp.int32, (tq, 2 * tq), 1)
    dist = qpos - kpos
    mask = (dist >= 0) & (dist <= window) & ((kpos >= tq) | jnp.logical_not(first))
    lane = lax.broadcasted_iota(jnp.int32, (tq, LANES), 1)
    lo = lane < HEAD_DIM
    ct = (((1,), (1,)), ((), ()))
    for p in range(q_ref.shape[1] // LANES):
        cols = slice(p * LANES, (p + 1) * LANES)
        q2 = q_ref[:, cols]
        k2 = jnp.concatenate([kp_ref[:, cols], kc_ref[:, cols]], axis=0)
        v2 = jnp.concatenate([vp_ref[:, cols], vc_ref[:, cols]], axis=0)
        outs, lses = [], []
        for half in (lo, jnp.logical_not(lo)):
            qm = jnp.where(half, q2, jnp.zeros_like(q2))
            s = jnp.where(mask, lax.dot_general(qm, k2, ct, preferred_element_type=F32), NEG_INF)
            m = jnp.max(s, axis=-1, keepdims=True)
            e = jnp.where(mask, jnp.exp(s - m), 0.0)
            den = jnp.sum(e, axis=-1, keepdims=True)
            pv = jnp.dot(e.astype(BF16), v2, preferred_element_type=F32)
            outs.append(pv / jnp.maximum(den, 1e-30))
            lses.append(m + jnp.log(den))
        o_ref[:, cols] = jnp.where(lo, outs[0], outs[1]).astype(o_ref.dtype)
        lse_ref[:, cols] = jnp.where(lo, lses[0], lses[1])


def dil_attention(qk, v, seq_len, window):
    N, W = v.shape
    tq = DIL_TQ
    assert window == tq and seq_len % tq == 0
    nwb = 1
    cur = lambda i: (i, 0)
    prev = lambda i: (jnp.maximum(i - 1, 0), 0)
    return pl.pallas_call(
        functools.partial(_dil_attn_kernel, tiles_per_seq=seq_len // tq, window=window),
        out_shape=(jax.ShapeDtypeStruct((N, W), BF16), jax.ShapeDtypeStruct((N, W), F32)),
        grid=(N // tq,),
        in_specs=[pl.BlockSpec((tq, W), cur),
                  pl.BlockSpec((tq, W), lambda i: (i, nwb)),
                  pl.BlockSpec((tq, W), lambda i: (jnp.maximum(i - 1, 0), nwb)),
                  pl.BlockSpec((tq, W), cur), pl.BlockSpec((tq, W), prev)],
        out_specs=(pl.BlockSpec((tq, W), cur), pl.BlockSpec((tq, W), cur)),
        compiler_params=_cparams(("parallel",)), name="dil_attn",
    )(qk, qk, qk, v, v)


def _dil_merge_kernel(o0, o1, o2, l0, l1, l2, w_ref, x_ref, g_ref, b_ref, out_ref, outb_ref):
    la, lb, lc = l0[...], l1[...], l2[...]
    m = jnp.maximum(jnp.maximum(la, lb), lc)
    ea, eb, ec = jnp.exp(la - m), jnp.exp(lb - m), jnp.exp(lc - m)
    tot = ea + eb + ec
    o = (ea / tot) * o0[...].astype(F32) + (eb / tot) * o1[...].astype(F32) + (ec / tot) * o2[...].astype(F32)
    h = jnp.dot(o.astype(BF16), w_ref[...], preferred_element_type=F32)
    y = _ln(DN_ALPHA * x_ref[...] + h, g_ref[...], b_ref[...])
    out_ref[...] = y
    outb_ref[...] = y.astype(BF16)


def dil_merge_out(os_, lses, w_out, x, g, b, tm=256):
    N, D = x.shape
    W = w_out.shape[0]
    tm = _pick(N, tm)
    row = lambda i: (i, 0)
    fix = lambda i: (0, 0)
    rs = pl.BlockSpec((tm, W), row)
    return pl.pallas_call(
        _dil_merge_kernel,
        out_shape=(jax.ShapeDtypeStruct((N, D), F32), jax.ShapeDtypeStruct((N, D), BF16)),
        grid=(N // tm,),
        in_specs=[rs, rs, rs, rs, rs, rs, pl.BlockSpec((W, D), fix), pl.BlockSpec((tm, D), row),
                  pl.BlockSpec((1, D), fix), pl.BlockSpec((1, D), fix)],
        out_specs=(pl.BlockSpec((tm, D), row), pl.BlockSpec((tm, D), row)),
        compiler_params=_cparams(("parallel",)), name="dil_merge",
    )(*os_, *lses, w_out, x, g.reshape(1, D), b.reshape(1, D))


def dilated_layer(x, xb, w_in, w_out, ln_g, ln_b, B, S):
    N, D = x.shape
    H, dh = DIL_HEADS, HEAD_DIM
    W = H * dh
    w6 = w_in.reshape(D, len(DIL_CONFIGS), 3, W)
    pos = jnp.arange(S)
    os_, lses = [], []
    for gi, (window, dil) in enumerate(DIL_CONFIGS):
        L = S // dil
        wqk = jnp.concatenate([w6[:, gi, 0] * dh ** -0.5, w6[:, gi, 1]], axis=1).astype(BF16)
        wv = w6[:, gi, 2].astype(BF16)
        xp = xb.reshape(B, L, dil, D).transpose(0, 2, 1, 3).reshape(N, D) if dil > 1 else xb
        ppos = pos.reshape(L, dil).T.reshape(S)
        qk = matmul(xp, wqk, out_dtype=BF16, rope=rope_tables(ppos))
        v = matmul(xp, wv, out_dtype=BF16)
        o, lse = dil_attention(qk, v, L, window // dil)
        if dil > 1:
            o = o.reshape(B, dil, L, W).transpose(0, 2, 1, 3).reshape(N, W)
            lse = lse.reshape(B, dil, L, W).transpose(0, 2, 1, 3).reshape(N, W)
        os_.append(o)
        lses.append(lse)
    return dil_merge_out(os_, lses, w_out.astype(BF16), x, ln_g, ln_b)


NSA_HEADS = D_MODEL // HEAD_DIM
NSA_KV_HEADS = 4
NSA_GROUP = NSA_HEADS // NSA_KV_HEADS
CMP_LEN = 32
CMP_STRIDE = 16
CMP_HIDDEN = 256
SEL_LEN = 64
SEL_TOPN = 16
NSA_WINDOW = 512
FORCE_BONUS = 100.0
NSA_TQ = 128
NSA_TK = 128
NSA_CHUNK = 512


def _compress_kernel(x_ref, p_ref, w1a_ref, w1b_ref, w2_ref, o_ref, *, transpose_out):
    x = x_ref[0].astype(F32)
    n = x.shape[0]
    first = jnp.dot((x + p_ref[0:1, :]).astype(BF16), w1a_ref[...], preferred_element_type=F32)
    second = jnp.dot((x + p_ref[1:2, :]).astype(BF16), w1b_ref[...], preferred_element_type=F32)
    hid = first + pltpu.roll(second, n - 1, 0)
    out = jnp.dot(jax.nn.gelu(hid).astype(BF16), w2_ref[...], preferred_element_type=F32)
    if transpose_out:
        o_ref[0] = out.T[:HEAD_DIM, :].astype(o_ref.dtype)
    else:
        o_ref[0] = out[:, :HEAD_DIM].astype(o_ref.dtype)


def nsa_compress(t, pos_emb, w1, w2, transpose_out):
    BK, n, W = t.shape
    half = CMP_STRIDE * HEAD_DIM
    p = pos_emb.reshape(2, half).astype(F32)
    w2p = jnp.zeros((CMP_HIDDEN, LANES), F32).at[:, :HEAD_DIM].set(w2).astype(BF16)
    oshape = (BK, HEAD_DIM, n) if transpose_out else (BK, n, HEAD_DIM)
    fix = lambda i: (0, 0)
    return pl.pallas_call(
        functools.partial(_compress_kernel, transpose_out=transpose_out),
        out_shape=jax.ShapeDtypeStruct(oshape, BF16), grid=(BK,),
        in_specs=[pl.BlockSpec((1, n, W), lambda i: (i, 0, 0)), pl.BlockSpec((2, half), fix),
                  pl.BlockSpec((half, CMP_HIDDEN), fix), pl.BlockSpec((half, CMP_HIDDEN), fix),
                  pl.BlockSpec((CMP_HIDDEN, LANES), fix)],
        out_specs=pl.BlockSpec((1,) + oshape[1:], lambda i: (i, 0, 0)),
        compiler_params=_cparams(("parallel",)), name="nsa_compress",
    )(t, p, w1[:half].astype(BF16), w1[half:].astype(BF16), w2p)


def _col_softmax_step(carry, s, valid, vt):
    m, l, acc = carry
    m_new = jnp.maximum(m, jnp.max(s, axis=0, keepdims=True))
    alpha = jnp.exp(m - m_new)
    p = jnp.exp(s - m_new)
    if valid is not None:
        p = jnp.where(valid, p, 0.0)
    l = alpha * l + jnp.sum(p, axis=0, keepdims=True)
    acc = alpha * acc + jnp.dot(vt, p.astype(BF16), preferred_element_type=F32)
    return m_new, l, acc


def _nsa_attn_t_kernel(q_ref, kc_ref, vct_ref, ks_ref, vst_ref, kw_ref, vwt_ref, oh_ref, gl_ref, ovl_ref, o_ref):
    kh = pl.program_id(1)
    i = pl.program_id(2)
    tq, dh, G = NSA_TQ, HEAD_DIM, NSA_GROUP
    tk = NSA_TK
    M = G * tq
    t0 = i * tq
    slot = kh % 2
    vrows = pl.ds(pl.multiple_of(slot * dh, dh), dh)

    def tpos(shape):
        return t0 + (lax.broadcasted_iota(jnp.int32, shape, 1) & (tq - 1))

    qn = q_ref[...].astype(F32)
    qt_pairs = [qn[:, c * LANES:(c + 1) * LANES].T for c in range(G * dh // LANES)]
    qt = jnp.concatenate([p[h * dh:(h + 1) * dh] for p in qt_pairs for h in range(LANES // dh)], axis=1)
    qt = qt.astype(BF16)

    n_cmp = kc_ref.shape[1]
    s_c = jnp.dot(kc_ref[0], qt, preferred_element_type=F32)
    cend = lax.broadcasted_iota(jnp.int32, (n_cmp, M), 0) * CMP_STRIDE + (CMP_LEN - 1)
    vis = cend <= tpos((n_cmp, M))
    s_c = jnp.where(vis, s_c, NEG_INF)
    e_c = jnp.where(vis, jnp.exp(s_c - jnp.max(s_c, axis=0, keepdims=True)), 0.0)
    p_c = e_c / jnp.maximum(jnp.sum(e_c, axis=0, keepdims=True), 1e-30)
    o_c = jnp.dot(vct_ref[0], p_c.astype(BF16), preferred_element_type=F32)

    psum = p_c[:, 0:tq]
    for g in range(1, G):
        psum = psum + p_c[:, g * tq:(g + 1) * tq]
    ovl = ovl_ref[...]
    imp = sum(jnp.dot(ovl, piece, preferred_element_type=F32) for piece in _split3(psum))
    n_sel = imp.shape[0]
    blk = lax.broadcasted_iota(jnp.int32, (n_sel, tq), 0)
    cur = tpos((n_sel, tq)) // SEL_LEN
    forced = (blk == 0) | (blk == cur) | (blk == cur - 1)
    score = jnp.where(blk <= cur, imp + jnp.where(forced, FORCE_BONUS, 0.0), -1.0)
    chosen = jnp.zeros((n_sel, tq), jnp.bool_)
    for _ in range(min(SEL_TOPN, n_sel)):
        mx, idx = _first_index_of_max(score, blk, 0, n_sel)
        hit = blk == idx
        chosen = chosen | (hit & (mx >= 0.0))
        score = jnp.where(hit, -2.0, score)
    bias = jnp.where(chosen, 0.0, NEG_INF).astype(BF16)
    zero = jnp.zeros_like(qt)
    q_pair = jnp.concatenate([jnp.where(slot == 0, qt, zero), jnp.where(slot == 1, qt, zero)], axis=0)
    pad = jnp.zeros((LANES - n_sel, M), BF16)
    q_aug = jnp.concatenate([q_pair, jnp.concatenate([bias] * G, axis=1), pad], axis=0)

    init = (jnp.full((1, M), NEG_INF, F32), jnp.zeros((1, M), F32), jnp.zeros((dh, M), F32))

    def vt_cat(ref, first_tile, n):
        return jnp.concatenate([ref[0, 0, first_tile + j, vrows, :] for j in range(n)], axis=1)

    ch = NSA_CHUNK
    per_chunk = ch // tk

    def sel_chunk(c, carry, diagonal):
        rows = pl.ds(pl.multiple_of(c * ch, ch), ch)
        k_aug = jnp.concatenate([ks_ref[rows, :], oh_ref[rows, :]], axis=1)
        s = jnp.dot(k_aug, q_aug, preferred_element_type=F32)
        if diagonal:
            kpos = c * ch + lax.broadcasted_iota(jnp.int32, (ch, M), 0)
            s = jnp.where(kpos <= tpos((ch, M)), s, NEG_INF)
        return _col_softmax_step(carry, s, None, vt_cat(vst_ref, c * per_chunk, per_chunk))

    n_full = t0 // ch
    carry = lax.fori_loop(0, n_full, lambda c, cr: sel_chunk(c, cr, False), init)
    _, l_s, acc_s = sel_chunk(n_full, carry, True)
    o_s = acc_s / jnp.maximum(l_s, 1e-30)

    n_wt = NSA_WINDOW // tk + 1
    wt0 = jnp.maximum(i + 1 - n_wt, 0)
    wrows = pl.ds(pl.multiple_of(wt0 * tk, tk), n_wt * tk)
    s_w = jnp.dot(kw_ref[wrows, :], q_pair, preferred_element_type=F32)
    dist = tpos(s_w.shape) - (wt0 * tk + lax.broadcasted_iota(jnp.int32, s_w.shape, 0))
    near = (dist >= 0) & (dist < NSA_WINDOW)
    s_w = jnp.where(near, s_w, NEG_INF)
    e_w = jnp.exp(s_w - jnp.max(s_w, axis=0, keepdims=True))
    l_w = jnp.sum(e_w, axis=0, keepdims=True)
    o_w = jnp.dot(vt_cat(vwt_ref, wt0, n_wt), e_w.astype(BF16), preferred_element_type=F32) / jnp.maximum(l_w, 1e-30)

    gates = jax.nn.sigmoid(gl_ref[...].T)
    outs = []
    for g in range(G):
        cols = slice(g * tq, (g + 1) * tq)
        outs.append(gates[g:g + 1] * o_c[:, cols] + gates[G + g:G + g + 1] * o_s[:, cols]
                    + gates[2 * G + g:2 * G + g + 1] * o_w[:, cols])
    per = LANES // dh
    o_ref[...] = jnp.concatenate(
        [jnp.concatenate(outs[c * per:(c + 1) * per], axis=0).T for c in range(G // per)], axis=1).astype(o_ref.dtype)


def nsa_attention_t(roped, vst, vwt, k_cmp, v_cmpt, gl, B, S):
    H, KH, G, dh = NSA_HEADS, NSA_KV_HEADS, NSA_GROUP, HEAD_DIM
    tq, tk = NSA_TQ, NSA_TK
    assert tq == tk and S % NSA_CHUNK == 0 and NSA_CHUNK % tk == 0 and S >= NSA_WINDOW + tq and G * dh == 2 * LANES
    nt = S // tq
    n_cmp = k_cmp.shape[1]
    n_sel = S // SEL_LEN
    c0 = np.arange(n_cmp)[None, :] * CMP_STRIDE
    s0 = np.arange(n_sel)[:, None] * SEL_LEN
    ovl = jnp.asarray((c0 < s0 + SEL_LEN) & (c0 + CMP_LEN - 1 >= s0), BF16)
    onehot = jnp.asarray(np.arange(S)[:, None] // SEL_LEN == np.arange(LANES)[None, :], BF16)
    ks_col = (H * dh + KH * dh) // LANES
    kw_col = (H * dh + 2 * KH * dh) // LANES
    qspec = pl.BlockSpec((tq, G * dh), lambda b, h, i: (b * nt + i, h))
    vspec = pl.BlockSpec((1, 1, nt, LANES, tk), lambda b, h, i: (b, h // 2, 0, 0, 0))
    return pl.pallas_call(
        _nsa_attn_t_kernel, out_shape=jax.ShapeDtypeStruct((B * S, H * dh), BF16), grid=(B, KH, nt),
        in_specs=[qspec,
                  pl.BlockSpec((1, n_cmp, dh), lambda b, h, i: (b * KH + h, 0, 0)),
                  pl.BlockSpec((1, dh, n_cmp), lambda b, h, i: (b * KH + h, 0, 0)),
                  pl.BlockSpec((S, LANES), lambda b, h, i: (b, ks_col + h // 2)), vspec,
                  pl.BlockSpec((S, LANES), lambda b, h, i: (b, kw_col + h // 2)), vspec,
                  pl.BlockSpec((S, LANES), lambda b, h, i: (0, 0)),
                  pl.BlockSpec((tq, LANES), lambda b, h, i: (b * nt + i, h)),
                  pl.BlockSpec((n_sel, n_cmp), lambda b, h, i: (0, 0))],
        out_specs=qspec,
        compiler_params=_cparams(("parallel", "parallel", "arbitrary")), name="nsa_attn",
    )(roped, k_cmp, v_cmpt, roped, vst, roped, vwt, onehot, gl, ovl)


def nsa_mixer(xb, w_in, ck_pos, ck_w1, ck_w2, cv_pos, cv_w1, cv_w2, B, S):
    H, KH, G, dh = NSA_HEADS, NSA_KV_HEADS, NSA_GROUP, HEAD_DIM
    kvw = KH * dh
    cuts = np.cumsum([H * dh, kvw, kvw, kvw, kvw, kvw, kvw]).tolist()
    wq, wkc, wvc, wks, wvs, wkw, wvw, wgl = jnp.split(w_in, cuts, axis=1)
    w_rope = jnp.concatenate([wq * dh ** -0.5, wkc, wks, wkw], axis=1).astype(BF16)
    w_val = jnp.concatenate([wvc, wvs, wvw], axis=1).astype(BF16)
    w_gate = jnp.zeros((w_in.shape[0], KH, LANES), F32).at[:, :, :3 * G].set(
        wgl.reshape(-1, KH, G, 3).transpose(0, 1, 3, 2).reshape(-1, KH, 3 * G)).reshape(-1, KH * LANES).astype(BF16)
    roped = matmul(xb, w_rope, out_dtype=BF16, rope=rope_tables(jnp.arange(S)), tn=w_rope.shape[1] // 2)
    vals = matmul(xb, w_val, out_dtype=BF16)
    gl = matmul(xb, w_gate, out_dtype=F32)

    n16 = S // CMP_STRIDE

    def blocks16(t):
        return t.reshape(B, n16, CMP_STRIDE, KH, dh).transpose(0, 3, 1, 2, 4).reshape(B * KH, n16, CMP_STRIDE * dh)

    def vt_tiles(t):
        return t.reshape(B, S // NSA_TK, NSA_TK, KH * dh // LANES, LANES).transpose(0, 3, 1, 4, 2)

    k_cmp = nsa_compress(blocks16(roped[:, H * dh:H * dh + kvw]), ck_pos, ck_w1, ck_w2, False)
    v_cmpt = nsa_compress(blocks16(vals[:, :kvw]), cv_pos, cv_w1, cv_w2, True)
    return nsa_attention_t(roped, vt_tiles(vals[:, kvw:2 * kvw]), vt_tiles(vals[:, 2 * kvw:]), k_cmp, v_cmpt, gl, B, S)


def kernel(x, l0_nsa_w_in, l0_nsa_w_out, l0_nsa_ck_pos, l0_nsa_ck_w1, l0_nsa_ck_w2, l0_nsa_cv_pos, l0_nsa_cv_w1, l0_nsa_cv_w2, l0_ln1_g, l0_ln1_b, l0_router_w, l0_router_b, l0_moe_w_in, l0_moe_w_out, l0_shared_w_in, l0_shared_w_out, l0_ln2_g, l0_ln2_b, l1_gla_w_in, l1_gla_w_a2, l1_gla_b_a, l1_gla_norm_g, l1_gla_w_out, l1_ln1_g, l1_ln1_b, l1_router_w, l1_router_b, l1_moe_w_in, l1_moe_w_out, l1_shared_w_in, l1_shared_w_out, l1_ln2_g, l1_ln2_b, l2_dil_w_in, l2_dil_w_out, l2_ln1_g, l2_ln1_b, l2_router_w, l2_router_b, l2_moe_w_in, l2_moe_w_out, l2_shared_w_in, l2_shared_w_out, l2_ln2_g, l2_ln2_b, l3_nsa_w_in, l3_nsa_w_out, l3_nsa_ck_pos, l3_nsa_ck_w1, l3_nsa_ck_w2, l3_nsa_cv_pos, l3_nsa_cv_w1, l3_nsa_cv_w2, l3_ln1_g, l3_ln1_b, l3_router_w, l3_router_b, l3_moe_w_in, l3_moe_w_out, l3_shared_w_in, l3_shared_w_out, l3_ln2_g, l3_ln2_b):
    B, S, D = x.shape
    xf = x.reshape(B * S, D)
    xb = xf.astype(BF16)

    h = nsa_mixer(xb, l0_nsa_w_in, l0_nsa_ck_pos, l0_nsa_ck_w1, l0_nsa_ck_w2, l0_nsa_cv_pos, l0_nsa_cv_w1, l0_nsa_cv_w2, B, S)
    xf, xb = matmul_res_ln(h, l0_nsa_w_out.astype(BF16), xf, l0_ln1_g, l0_ln1_b)
    xf, xb = moe_layer(xf, l0_router_w, l0_router_b, l0_moe_w_in, l0_moe_w_out, l0_shared_w_in, l0_shared_w_out, l0_ln2_g, l0_ln2_b)

    h = gla_mixer(xb, l1_gla_w_in, l1_gla_w_a2, l1_gla_b_a, l1_gla_norm_g, B, S)
    xf, xb = matmul_res_ln(h, l1_gla_w_out.astype(BF16), xf, l1_ln1_g, l1_ln1_b)
    xf, xb = moe_layer(xf, l1_router_w, l1_router_b, l1_moe_w_in, l1_moe_w_out, l1_shared_w_in, l1_shared_w_out, l1_ln2_g, l1_ln2_b)

    xf, xb = dilated_layer(xf, xb, l2_dil_w_in, l2_dil_w_out, l2_ln1_g, l2_ln1_b, B, S)
    xf, xb = moe_layer(xf, l2_router_w, l2_router_b, l2_moe_w_in, l2_moe_w_out, l2_shared_w_in, l2_shared_w_out, l2_ln2_g, l2_ln2_b)

    h = nsa_mixer(xb, l3_nsa_w_in, l3_nsa_ck_pos, l3_nsa_ck_w1, l3_nsa_ck_w2, l3_nsa_cv_pos, l3_nsa_cv_w1, l3_nsa_cv_w2, B, S)
    xf, xb = matmul_res_ln(h, l3_nsa_w_out.astype(BF16), xf, l3_ln1_g, l3_ln1_b)
    xf, xb = moe_layer(xf, l3_router_w, l3_router_b, l3_moe_w_in, l3_moe_w_out, l3_shared_w_in, l3_shared_w_out, l3_ln2_g, l3_ln2_b)
    return xf.reshape(B, S, D)
```

```python
import functools
import math

import jax
import jax.numpy as jnp
import numpy as np
from jax import lax
from jax.experimental import pallas as pl
from jax.experimental.pallas import tpu as pltpu

F32 = jnp.float32
BF16 = jnp.bfloat16

D_MODEL = 1024
DEPTH = 4
HEAD_DIM = 64
ROPE_THETA = 500000.0
ROPE_DIM = HEAD_DIM // 4
ROPE_HALF = ROPE_DIM // 2

N_EXPERTS = 64
TOP_K = 8
N_GROUPS = 8
TOPK_GROUPS = 4
GROUP_SIZE = N_EXPERTS // N_GROUPS
D_EXPERT = 256
ROUTED_SCALE = 2.5

DN_ALPHA = (2.0 * DEPTH) ** 0.25
LN_EPS = 1e-5
NEG_INF = -1e30

LANES = 128
VMEM_LIMIT = 48 * 1024 * 1024


def _cparams(sem, **kw):
    return pltpu.CompilerParams(dimension_semantics=sem, vmem_limit_bytes=VMEM_LIMIT, **kw)


def _pick(n, pref):
    t = min(pref, n)
    while n % t:
        t //= 2
    return t


def _mm_kernel(x_ref, w_ref, o_ref):
    o_ref[...] = jnp.dot(x_ref[...], w_ref[...], preferred_element_type=F32).astype(o_ref.dtype)


def _mm_rope_kernel(x_ref, w_ref, c_ref, sm_ref, sp_ref, o_ref):
    y = jnp.dot(x_ref[...], w_ref[...], preferred_element_type=F32)
    reps = y.shape[1] // LANES
    c = jnp.tile(c_ref[...], (1, reps))
    sm = jnp.tile(sm_ref[...], (1, reps))
    sp = jnp.tile(sp_ref[...], (1, reps))
    up = pltpu.roll(y, y.shape[1] - ROPE_HALF, 1)
    dn = pltpu.roll(y, ROPE_HALF, 1)
    o_ref[...] = (y * c + up * sm + dn * sp).astype(o_ref.dtype)


def matmul(x, w, out_dtype=F32, rope=None, tm=None, tn=1024):
    M, K = x.shape
    N = w.shape[1]
    tm = _pick(M, tm or (1024 if rope is None else 512))
    tn = _pick(N, tn)
    grid = (N // tn, M // tm)
    x_spec = pl.BlockSpec((tm, K), lambda j, i: (i, 0))
    w_spec = pl.BlockSpec((K, tn), lambda j, i: (0, j))
    o_spec = pl.BlockSpec((tm, tn), lambda j, i: (i, j))
    if rope is None:
        return pl.pallas_call(
            _mm_kernel, out_shape=jax.ShapeDtypeStruct((M, N), out_dtype), grid=grid,
            in_specs=[x_spec, w_spec], out_specs=o_spec,
            compiler_params=_cparams(("parallel", "parallel")), name="mm")(x, w)
    R = rope[0].shape[0]
    tm = _pick(R, tm)
    grid = (N // tn, M // tm)
    x_spec = pl.BlockSpec((tm, K), lambda j, i: (i, 0))
    o_spec = pl.BlockSpec((tm, tn), lambda j, i: (i, j))
    nr = R // tm
    t_spec = pl.BlockSpec((tm, LANES), lambda j, i: (i % nr, 0))
    return pl.pallas_call(
        _mm_rope_kernel, out_shape=jax.ShapeDtypeStruct((M, N), out_dtype), grid=grid,
        in_specs=[x_spec, w_spec, t_spec, t_spec, t_spec], out_specs=o_spec,
        compiler_params=_cparams(("parallel", "parallel")), name="mm_rope")(x, w, *rope)


def rope_tables(pos):
    inv = ROPE_THETA ** (-jnp.arange(ROPE_HALF, dtype=F32) * 2.0 / ROPE_DIM)
    ang = pos.astype(F32)[:, None] * inv[None, :]
    cos, sin = jnp.cos(ang), jnp.sin(ang)
    n = pos.shape[0]
    ones = jnp.ones((n, HEAD_DIM - ROPE_DIM), F32)
    zeros = jnp.zeros((n, HEAD_DIM - ROPE_DIM), F32)
    zh = jnp.zeros((n, ROPE_HALF), F32)
    c = jnp.concatenate([cos, cos, ones], 1)
    sm = jnp.concatenate([-sin, zh, zeros], 1)
    sp = jnp.concatenate([zh, sin, zeros], 1)
    return tuple(jnp.tile(t, (1, LANES // HEAD_DIM)) for t in (c, sm, sp))


def _ln(v, g, b):
    mu = jnp.mean(v, axis=-1, keepdims=True)
    d = v - mu
    var = jnp.mean(d * d, axis=-1, keepdims=True)
    return d * lax.rsqrt(var + LN_EPS) * g + b


def _mm_res_ln_kernel(a_ref, w_ref, x_ref, g_ref, b_ref, o_ref, ob_ref):
    h = jnp.dot(a_ref[...], w_ref[...], preferred_element_type=F32)
    y = _ln(DN_ALPHA * x_ref[...] + h, g_ref[...], b_ref[...])
    o_ref[...] = y
    ob_ref[...] = y.astype(BF16)


def matmul_res_ln(a, w, x, g, b, tm=512):
    M, K = a.shape
    D = w.shape[1]
    tm = _pick(M, tm)
    row = lambda i: (i, 0)
    fix = lambda i: (0, 0)
    return pl.pallas_call(
        _mm_res_ln_kernel,
        out_shape=(jax.ShapeDtypeStruct((M, D), F32), jax.ShapeDtypeStruct((M, D), BF16)),
        grid=(M // tm,),
        in_specs=[pl.BlockSpec((tm, K), row), pl.BlockSpec((K, D), fix), pl.BlockSpec((tm, D), row),
                  pl.BlockSpec((1, D), fix), pl.BlockSpec((1, D), fix)],
        out_specs=(pl.BlockSpec((tm, D), row), pl.BlockSpec((tm, D), row)),
        compiler_params=_cparams(("parallel",)), name="mm_res_ln")(a, w, x, g.reshape(1, D), b.reshape(1, D))


def _first_index_of_max(v, iota, axis, n):
    m = jnp.max(v, axis=axis, keepdims=True)
    idx = jnp.min(jnp.where(v == m, iota, n), axis=axis, keepdims=True)
    return m, idx


def _router_kernel(x_ref, wh_ref, wl_ref, rb_ref, tri_ref, slot_ref, gatew_ref, cnt_ref):
    x = x_ref[...]
    xh = x.astype(BF16)
    xl = (x - xh.astype(F32)).astype(BF16)
    dn = (((1,), (1,)), ((), ()))
    logits = (lax.dot_general(wh_ref[...], xh, dn, preferred_element_type=F32)
              + lax.dot_general(wh_ref[...], xl, dn, preferred_element_type=F32)
              + lax.dot_general(wl_ref[...], xh, dn, preferred_element_type=F32))
    tm = logits.shape[1]
    s = jax.nn.sigmoid(logits)
    sb = s + rb_ref[...]
    sb3 = sb.reshape(N_GROUPS, GROUP_SIZE, tm)
    io3 = lax.broadcasted_iota(jnp.int32, sb3.shape, 1)
    m1, i1 = _first_index_of_max(sb3, io3, 1, GROUP_SIZE)
    m2 = jnp.max(jnp.where(io3 == i1, -jnp.inf, sb3), axis=1, keepdims=True)
    gs = (m1 + m2).reshape(N_GROUPS, tm)
    iog = lax.broadcasted_iota(jnp.int32, gs.shape, 0)
    gmask = jnp.zeros(gs.shape, jnp.bool_)
    for _ in range(TOPK_GROUPS):
        _, gi = _first_index_of_max(gs, iog, 0, N_GROUPS)
        pick = iog == gi
        gmask = gmask | pick
        gs = jnp.where(pick, -jnp.inf, gs)
    emask = jnp.broadcast_to(gmask.reshape(N_GROUPS, 1, tm), sb3.shape).reshape(N_EXPERTS, tm)
    cand = jnp.where(emask, sb, NEG_INF)
    ioe = lax.broadcasted_iota(jnp.int32, cand.shape, 0)
    sel = jnp.zeros(cand.shape, jnp.bool_)
    picked = []
    for _ in range(TOP_K):
        _, ei = _first_index_of_max(cand, ioe, 0, N_EXPERTS)
        pick = ioe == ei
        picked.append(jnp.sum(jnp.where(pick, s, 0.0), axis=0, keepdims=True))
        sel = sel | pick
        cand = jnp.where(pick, -jnp.inf, cand)
    total = picked[0]
    for g in picked[1:]:
        total = total + g
    gatew_ref[...] = jnp.where(sel, s / total * ROUTED_SCALE, 0.0)
    routed = jnp.where(sel, 1.0, 0.0)
    before = jnp.dot(routed.astype(BF16), tri_ref[...], preferred_element_type=F32)
    slot_ref[...] = jnp.where(sel, before, -1.0).astype(jnp.int32)
    cnt_ref[0] = jnp.broadcast_to(jnp.sum(routed, axis=1, keepdims=True), cnt_ref.shape[1:]).astype(jnp.int32)


def moe_router(x, router_w, router_b):
    N, D = x.shape
    tm = MOE_TILE
    assert N % tm == 0
    wt = router_w.T
    wh = wt.astype(BF16)
    wl = (wt - wh.astype(F32)).astype(BF16)
    tri = (jnp.arange(tm)[:, None] < jnp.arange(tm)[None, :]).astype(BF16)
    fix = lambda i: (0, 0)
    col = lambda i: (0, i)
    slot, gatew, cnt = pl.pallas_call(
        _router_kernel,
        out_shape=(jax.ShapeDtypeStruct((N_EXPERTS, N), jnp.int32), jax.ShapeDtypeStruct((N_EXPERTS, N), F32),
                   jax.ShapeDtypeStruct((N // tm, N_EXPERTS, LANES), jnp.int32)),
        grid=(N // tm,),
        in_specs=[pl.BlockSpec((tm, D), lambda i: (i, 0)), pl.BlockSpec((N_EXPERTS, D), fix),
                  pl.BlockSpec((N_EXPERTS, D), fix), pl.BlockSpec((N_EXPERTS, 1), fix), pl.BlockSpec((tm, tm), fix)],
        out_specs=(pl.BlockSpec((N_EXPERTS, tm), col), pl.BlockSpec((N_EXPERTS, tm), col),
                   pl.BlockSpec((1, N_EXPERTS, LANES), lambda i: (i, 0, 0))),
        compiler_params=_cparams(("arbitrary",)), name="moe_router",
    )(x, wh, wl, router_b.reshape(N_EXPERTS, 1).astype(F32), tri)
    return slot, gatew, cnt[:, :, 0]


MOE_TILE = 512
MOE_CAP = 96
MOE_EGROUP = 16
MOE_BLOCK = 256
MOE_ALIGN = 16


def _slot_matrix(slot_rows, first, weights=None):
    n_e, T = slot_rows.shape
    r = lax.broadcasted_iota(jnp.int32, (MOE_CAP, T), 0) + first
    pieces = []
    for e in range(n_e):
        hit = slot_rows[e:e + 1, :] == r
        w = 1.0 if weights is None else weights[e:e + 1, :]
        pieces.append(jnp.where(hit, w, 0.0).astype(BF16))
    return pieces[0] if n_e == 1 else jnp.concatenate(pieces, axis=0)


def _n_windows(n_rows):
    return (n_rows + MOE_CAP - 1) // MOE_CAP


def _dispatch_kernel(base_ref, cnt_ref, zrow_ref, slot_ref, xb_ref, xs_ref, obuf, ovbuf, zbuf, sems, ovsem, zsem):
    t = pl.program_id(0)
    C, EG, E = MOE_CAP, MOE_EGROUP, N_EXPERTS
    n_groups = E // EG
    tz = zbuf.shape[0]

    @pl.when(t == 0)
    def _():
        zbuf[...] = jnp.zeros_like(zbuf)

        def zcopy(j):
            return pltpu.make_async_copy(zbuf, xs_ref.at[pl.ds(pl.multiple_of(zrow_ref[j], tz), tz)], zsem)

        def zstart(j, c):
            @pl.when(zrow_ref[j] >= 0)
            def _():
                zcopy(j).start()
            return c

        def zwait(j, c):
            @pl.when(zrow_ref[j] >= 0)
            def _():
                zcopy(j).wait()
            return c

        lax.fori_loop(0, zrow_ref.shape[0], zstart, 0)
        lax.fori_loop(0, zrow_ref.shape[0], zwait, 0)

    xb = xb_ref[...]

    def window(e, extra):
        return xs_ref.at[pl.ds(pl.multiple_of(base_ref[t * E + e] + extra, MOE_ALIGN), C)]

    def copies(g):
        return [pltpu.make_async_copy(obuf.at[g % 2, pl.ds(e * C, C)], window(g * EG + e, 0), sems.at[g % 2])
                for e in range(EG)]

    for g in range(n_groups):
        if g >= 2:
            for cp in copies(g - 2):
                cp.wait()
        onehot = _slot_matrix(slot_ref[g * EG:(g + 1) * EG, :], 0)
        obuf[g % 2] = jnp.dot(onehot, xb, preferred_element_type=F32).astype(BF16)
        for cp in copies(g):
            cp.start()
    for g in range(max(n_groups - 2, 0), n_groups):
        for cp in copies(g):
            cp.wait()

    def overflow(e, c):
        def chunk(j, c2):
            onehot = _slot_matrix(slot_ref[pl.ds(e, 1), :], j * C)
            ovbuf[...] = jnp.dot(onehot, xb, preferred_element_type=F32).astype(BF16)
            cp = pltpu.make_async_copy(ovbuf, window(e, j * C), ovsem)
            cp.start()
            cp.wait()
            return c2

        return lax.fori_loop(1, _n_windows(cnt_ref[t * E + e]), chunk, c)

    lax.fori_loop(0, E, overflow, 0)


def moe_dispatch(xb, slot, base, cnt, zrow, P):
    N, D = xb.shape
    T, C, EG = MOE_TILE, MOE_CAP, MOE_EGROUP
    grid_spec = pltpu.PrefetchScalarGridSpec(
        num_scalar_prefetch=3, grid=(N // T,),
        in_specs=[pl.BlockSpec((N_EXPERTS, T), lambda i, *_: (0, i)),
                  pl.BlockSpec((T, D), lambda i, *_: (i, 0))],
        out_specs=pl.BlockSpec(memory_space=pl.ANY),
        scratch_shapes=[pltpu.VMEM((2, EG * C, D), BF16), pltpu.VMEM((C, D), BF16), pltpu.VMEM((MOE_BLOCK, D), BF16),
                        pltpu.SemaphoreType.DMA((2,)), pltpu.SemaphoreType.DMA(()), pltpu.SemaphoreType.DMA(())])
    return pl.pallas_call(
        _dispatch_kernel, out_shape=jax.ShapeDtypeStruct((P, D), BF16), grid_spec=grid_spec,
        compiler_params=_cparams(("arbitrary",), has_side_effects=True), name="moe_dispatch",
    )(base.reshape(-1), cnt.reshape(-1), zrow, slot, xb)


def _swiglu(x_bf16, w_in, w_out):
    h = jnp.dot(x_bf16, w_in, preferred_element_type=F32)
    f = h.shape[1] // 2
    a = jax.nn.silu(h[:, :f]) * h[:, f:]
    return jnp.dot(a.astype(BF16), w_out, preferred_element_type=F32)


def _expert_kernel(blk_e_ref, used_ref, xs_ref, wi_ref, wo_ref, y_ref):
    del blk_e_ref
    live = pl.program_id(0) < used_ref[0]

    @pl.when(live)
    def _():
        y_ref[...] = _swiglu(xs_ref[...], wi_ref[0], wo_ref[0]).astype(y_ref.dtype)

    @pl.when(jnp.logical_not(live))
    def _():
        y_ref[...] = jnp.zeros_like(y_ref)


def moe_experts(xs, blk_e, n_used, w_in_e, w_out_e):
    P, D = xs.shape
    tm = MOE_BLOCK
    F2 = w_in_e.shape[2]
    spare = P // tm - 1
    rows = lambda i, be, nu: (jnp.where(i < nu[0], i, spare), 0)
    grid_spec = pltpu.PrefetchScalarGridSpec(
        num_scalar_prefetch=2, grid=(P // tm,),
        in_specs=[pl.BlockSpec((tm, D), rows),
                  pl.BlockSpec((1, D, F2), lambda i, be, nu: (be[i], 0, 0)),
                  pl.BlockSpec((1, F2 // 2, D), lambda i, be, nu: (be[i], 0, 0))],
        out_specs=pl.BlockSpec((tm, D), rows))
    return pl.pallas_call(
        _expert_kernel, out_shape=jax.ShapeDtypeStruct((P, D), BF16), grid_spec=grid_spec,
        compiler_params=_cparams(("arbitrary",)), name="moe_experts")(blk_e, n_used, xs, w_in_e, w_out_e)


def _combine_kernel(base_ref, cnt_ref, slot_ref, gw_ref, x_ref, wi_ref, wo_ref, g_ref, b_ref, ys_ref, o_ref, ob_ref,
                    ybuf, ovbuf, acc_ref, sems, ovsem):
    t = pl.program_id(0)
    C, EG, E = MOE_CAP, MOE_EGROUP, N_EXPERTS
    n_groups = E // EG
    rows_in = (((0,), (0,)), ((), ()))

    def window(e, extra):
        return ys_ref.at[pl.ds(pl.multiple_of(base_ref[t * E + e] + extra, MOE_ALIGN), C)]

    def copies(g):
        return [pltpu.make_async_copy(window(g * EG + e, 0), ybuf.at[g % 2, pl.ds(e * C, C)], sems.at[g % 2])
                for e in range(EG)]

    for cp in copies(0):
        cp.start()
    x = x_ref[...]
    acc_ref[...] = _swiglu(x.astype(BF16), wi_ref[...], wo_ref[...])
    for g in range(n_groups):
        if g + 1 < n_groups:
            for cp in copies(g + 1):
                cp.start()
        for cp in copies(g):
            cp.wait()
        spread = _slot_matrix(slot_ref[g * EG:(g + 1) * EG, :], 0, gw_ref[g * EG:(g + 1) * EG, :])
        acc_ref[...] += lax.dot_general(spread, ybuf[g % 2], rows_in, preferred_element_type=F32)

    def overflow(e, c):
        def chunk(j, c2):
            cp = pltpu.make_async_copy(window(e, j * C), ovbuf, ovsem)
            cp.start()
            cp.wait()
            spread = _slot_matrix(slot_ref[pl.ds(e, 1), :], j * C, gw_ref[pl.ds(e, 1), :])
            acc_ref[...] += lax.dot_general(spread, ovbuf[...], rows_in, preferred_element_type=F32)
            return c2

        return lax.fori_loop(1, _n_windows(cnt_ref[t * E + e]), chunk, c)

    lax.fori_loop(0, E, overflow, 0)
    y = _ln(DN_ALPHA * x + acc_ref[...], g_ref[...], b_ref[...])
    o_ref[...] = y
    ob_ref[...] = y.astype(BF16)


def moe_combine(x, ys, slot, gatew, base, cnt, w_in_s, w_out_s, g, b):
    N, D = x.shape
    T, C, EG = MOE_TILE, MOE_CAP, MOE_EGROUP
    F2 = w_in_s.shape[1]
    row = lambda i, *_: (i, 0)
    col = lambda i, *_: (0, i)
    fix = lambda i, *_: (0, 0)
    grid_spec = pltpu.PrefetchScalarGridSpec(
        num_scalar_prefetch=2, grid=(N // T,),
        in_specs=[pl.BlockSpec((N_EXPERTS, T), col), pl.BlockSpec((N_EXPERTS, T), col), pl.BlockSpec((T, D), row),
                  pl.BlockSpec((D, F2), fix), pl.BlockSpec((F2 // 2, D), fix),
                  pl.BlockSpec((1, D), fix), pl.BlockSpec((1, D), fix),
                  pl.BlockSpec(memory_space=pl.ANY)],
        out_specs=(pl.BlockSpec((T, D), row), pl.BlockSpec((T, D), row)),
        scratch_shapes=[pltpu.VMEM((2, EG * C, D), BF16), pltpu.VMEM((C, D), BF16), pltpu.VMEM((T, D), F32),
                        pltpu.SemaphoreType.DMA((2,)), pltpu.SemaphoreType.DMA(())])
    return pl.pallas_call(
        _combine_kernel,
        out_shape=(jax.ShapeDtypeStruct((N, D), F32), jax.ShapeDtypeStruct((N, D), BF16)),
        grid_spec=grid_spec,
        compiler_params=_cparams(("arbitrary",)), name="moe_combine",
    )(base.reshape(-1), cnt.reshape(-1), slot, gatew, x, w_in_s, w_out_s, g.reshape(1, D), b.reshape(1, D), ys)


def moe_layer(x, xb, router_w, router_b, w_in_e, w_out_e, w_in_s, w_out_s, ln_g, ln_b):
    N, D = x.shape
    E, C, A, tm_e = N_EXPERTS, MOE_CAP, MOE_ALIGN, MOE_BLOCK
    n_t = N // MOE_TILE
    slot, gatew, cnt = moe_router(x, router_w, router_b)
    seg = (cnt + A - 1) // A * A
    padded = (jnp.sum(seg, axis=0) + C + tm_e - 1) // tm_e * tm_e
    pad_end = jnp.cumsum(padded)
    pad_start = pad_end - padded
    base = (pad_start[None, :] + jnp.cumsum(seg, axis=0) - seg).astype(jnp.int32)
    max_rows = N * TOP_K + n_t * E * (A - 1) + E * (C + tm_e - 1)
    n_blk = -(-max_rows // tm_e) + 1
    blk_row = jnp.arange(n_blk, dtype=jnp.int32) * tm_e
    blk_e = jnp.minimum(jnp.sum(pad_end[None, :] <= blk_row[:, None], axis=1), E - 1).astype(jnp.int32)
    n_used = (pad_end[-1:] // tm_e).astype(jnp.int32)
    zrow = jnp.concatenate([pad_end - tm_e, jnp.where(padded >= 2 * tm_e, pad_end - 2 * tm_e, -1)]).astype(jnp.int32)
    xs = moe_dispatch(xb, slot, base, cnt, zrow, n_blk * tm_e)
    ys = moe_experts(xs, blk_e, n_used, w_in_e.astype(BF16), w_out_e.astype(BF16))
    return moe_combine(x, ys, slot, gatew, base, cnt, w_in_s.astype(BF16), w_out_s.astype(BF16), ln_g, ln_b)


GLA_HEADS = 4
GLA_DK = D_MODEL // 2 // GLA_HEADS
GLA_DV = D_MODEL // GLA_HEADS
GLA_GATE_RANK = 16
GLA_TAU = 16.0
GLA_CHUNK = 64


def _split3(v):
    h1 = v.astype(BF16)
    r1 = v - h1.astype(F32)
    h2 = r1.astype(BF16)
    h3 = (r1 - h2.astype(F32)).astype(BF16)
    return h1, h2, h3


def _gla_kernel(qkvr_ref, a_ref, wah_ref, wal_ref, ba_ref, ng_ref, tri_ref, o_ref, st_ref):
    H, dk, dv, C = GLA_HEADS, GLA_DK, GLA_DV, GLA_CHUNK

    @pl.when(pl.program_id(1) == 0)
    def _():
        st_ref[...] = jnp.zeros_like(st_ref)

    T = qkvr_ref.shape[0]
    a = a_ref[...]
    ah = a.astype(BF16)
    al = (a - ah.astype(F32)).astype(BF16)
    glog = (jnp.dot(ah, wah_ref[...], preferred_element_type=F32)
            + jnp.dot(al, wah_ref[...], preferred_element_type=F32)
            + jnp.dot(ah, wal_ref[...], preferred_element_type=F32)) + ba_ref[...]
    log_a = jax.nn.log_sigmoid(glog) / GLA_TAU
    tri = tri_ref[...]
    rr = lax.broadcasted_iota(jnp.int32, (C, C), 0)
    cc = lax.broadcasted_iota(jnp.int32, (C, C), 1)
    causal = rr >= cc
    ng = ng_ref[...]
    ct = (((1,), (1,)), ((), ()))
    c0 = (((0,), (0,)), ((), ()))
    for c in range(T // C):
        rows = slice(c * C, (c + 1) * C)
        for h in range(H):
            la = log_a[rows, h * dk:(h + 1) * dk]
            p1, p2, p3 = _split3(la)
            b = (jnp.dot(tri, p1, preferred_element_type=F32) + jnp.dot(tri, p2, preferred_element_type=F32)
                 + jnp.dot(tri, p3, preferred_element_type=F32))
            b_last = b[C - 1:C, :]
            q = qkvr_ref[rows, h * dk:(h + 1) * dk].astype(F32)
            k = qkvr_ref[rows, H * dk + h * dk:H * dk + (h + 1) * dk].astype(F32)
            v = qkvr_ref[rows, 2 * H * dk + h * dv:2 * H * dk + (h + 1) * dv]
            r = qkvr_ref[rows, 2 * H * dk + H * dv + h * dv:2 * H * dk + H * dv + (h + 1) * dv].astype(F32)
            qg = (q * jnp.exp(b)).astype(BF16)
            kg = (k * jnp.exp(-b)).astype(BF16)
            kd = (k * jnp.exp(b_last - b)).astype(BF16)
            att = jnp.where(causal, lax.dot_general(qg, kg, ct, preferred_element_type=F32), 0.0)
            st = st_ref[h]
            o = (jnp.dot(att.astype(BF16), v, preferred_element_type=F32)
                 + lax.dot_general(qg, st.astype(BF16), ct, preferred_element_type=F32))
            st_ref[h] = jnp.exp(b_last) * st + lax.dot_general(v, kd, c0, preferred_element_type=F32)
            o = o * lax.rsqrt(jnp.mean(o * o, axis=-1, keepdims=True) + LN_EPS) * ng
            o_ref[rows, h * dv:(h + 1) * dv] = (o * jax.nn.silu(r)).astype(o_ref.dtype)


def gla_core(qkvr, a, w_a2, b_a, norm_g, B, S, tile=256):
    H, dk, dv, C = GLA_HEADS, GLA_DK, GLA_DV, GLA_CHUNK
    N, W = qkvr.shape
    tile = _pick(S, tile)
    nt = S // tile
    wa = jnp.zeros((LANES, H * dk), F32).at[:GLA_GATE_RANK].set(w_a2)
    wah = wa.astype(BF16)
    wal = (wa - wah.astype(F32)).astype(BF16)
    tri = (jnp.arange(C)[:, None] >= jnp.arange(C)[None, :]).astype(BF16)
    row = lambda b, t: (b * nt + t, 0)
    fix = lambda b, t: (0, 0)
    return pl.pallas_call(
        _gla_kernel, out_shape=jax.ShapeDtypeStruct((N, H * dv), BF16), grid=(B, nt),
        in_specs=[pl.BlockSpec((tile, W), row), pl.BlockSpec((tile, LANES), row),
                  pl.BlockSpec((LANES, H * dk), fix), pl.BlockSpec((LANES, H * dk), fix),
                  pl.BlockSpec((1, H * dk), fix), pl.BlockSpec((1, dv), fix), pl.BlockSpec((C, C), fix)],
        out_specs=pl.BlockSpec((tile, H * dv), row),
        scratch_shapes=[pltpu.VMEM((H, dv, dk), F32)],
        compiler_params=_cparams(("parallel", "arbitrary")), name="gla_core",
    )(qkvr, a, wah, wal, b_a.reshape(1, H * dk), norm_g.reshape(1, dv), tri)


def gla_mixer(xb, w_in, w_a2, b_a, norm_g, B, S):
    H, dk, dv = GLA_HEADS, GLA_DK, GLA_DV
    hk, hv = H * dk, H * dv
    wq, wk, wv, wa, wr = jnp.split(w_in, [hk, 2 * hk, 2 * hk + hv, 2 * hk + hv + GLA_GATE_RANK], axis=1)
    w_main = jnp.concatenate([wq * dk ** -0.5, wk, wv, wr], axis=1).astype(BF16)
    w_gate = jnp.zeros((w_in.shape[0], LANES), F32).at[:, :GLA_GATE_RANK].set(wa).astype(BF16)
    qkvr = matmul(xb, w_main, out_dtype=BF16)
    a = matmul(xb, w_gate, out_dtype=F32)
    return gla_core(qkvr, a, w_a2, b_a, norm_g, B, S)


DIL_HEADS = D_MODEL // HEAD_DIM
DIL_CONFIGS = ((128, 1), (512, 4), (2048, 16))
DIL_TQ = 128


def _dil_attn_kernel(q_ref, kc_ref, kp_ref, vc_ref, vp_ref, o_ref, lse_ref, *, tiles_per_seq, window):
    i = pl.program_id(0)
    tq = q_ref.shape[0]
    first = (i % tiles_per_seq) == 0
    qpos = lax.broadcasted_iota(jnp.int32, (tq, 2 * tq), 0) + tq
    kpos = lax.broadcasted_iota(jnp.int32, (tq, 2 * tq), 1)
    dist = qpos - kpos
    mask = (dist >= 0) & (dist <= window) & ((kpos >= tq) | jnp.logical_not(first))
    lane = lax.broadcasted_iota(jnp.int32, (tq, LANES), 1)
    lo = lane < HEAD_DIM
    ct = (((1,), (1,)), ((), ()))
    for p in range(q_ref.shape[1] // LANES):
        cols = slice(p * LANES, (p + 1) * LANES)
        q2 = q_ref[:, cols]
        k2 = jnp.concatenate([kp_ref[:, cols], kc_ref[:, cols]], axis=0)
        v2 = jnp.concatenate([vp_ref[:, cols], vc_ref[:, cols]], axis=0)
        outs, lses = [], []
        for half in (lo, jnp.logical_not(lo)):
            qm = jnp.where(half, q2, jnp.zeros_like(q2))
            s = jnp.where(mask, lax.dot_general(qm, k2, ct, preferred_element_type=F32), NEG_INF)
            m = jnp.max(s, axis=-1, keepdims=True)
            e = jnp.where(mask, jnp.exp(s - m), 0.0)
            den = jnp.sum(e, axis=-1, keepdims=True)
            pv = jnp.dot(e.astype(BF16), v2, preferred_element_type=F32)
            outs.append(pv / jnp.maximum(den, 1e-30))
            lses.append(m + jnp.log(den))
        o_ref[:, cols] = jnp.where(lo, outs[0], outs[1]).astype(o_ref.dtype)
        lse_ref[:, cols] = jnp.where(lo, lses[0], lses[1])


def dil_attention(qk, v, seq_len, window):
    N, W = v.shape
    tq = DIL_TQ
    assert window == tq and seq_len % tq == 0
    nwb = 1
    cur = lambda i: (i, 0)
    prev = lambda i: (jnp.maximum(i - 1, 0), 0)
    return pl.pallas_call(
        functools.partial(_dil_attn_kernel, tiles_per_seq=seq_len // tq, window=window),
        out_shape=(jax.ShapeDtypeStruct((N, W), BF16), jax.ShapeDtypeStruct((N, W), F32)),
        grid=(N // tq,),
        in_specs=[pl.BlockSpec((tq, W), cur),
                  pl.BlockSpec((tq, W), lambda i: (i, nwb)),
                  pl.BlockSpec((tq, W), lambda i: (jnp.maximum(i - 1, 0), nwb)),
                  pl.BlockSpec((tq, W), cur), pl.BlockSpec((tq, W), prev)],
        out_specs=(pl.BlockSpec((tq, W), cur), pl.BlockSpec((tq, W), cur)),
        compiler_params=_cparams(("parallel",)), name="dil_attn",
    )(qk, qk, qk, v, v)


def _dil_merge_kernel(o0, o1, o2, l0, l1, l2, w_ref, x_ref, g_ref, b_ref, out_ref, outb_ref):
    la, lb, lc = l0[...], l1[...], l2[...]
    m = jnp.maximum(jnp.maximum(la, lb), lc)
    ea, eb, ec = jnp.exp(la - m), jnp.exp(lb - m), jnp.exp(lc - m)
    tot = ea + eb + ec
    o = (ea / tot) * o0[...].astype(F32) + (eb / tot) * o1[...].astype(F32) + (ec / tot) * o2[...].astype(F32)
    h = jnp.dot(o.astype(BF16), w_ref[...], preferred_element_type=F32)
    y = _ln(DN_ALPHA * x_ref[...] + h, g_ref[...], b_ref[...])
    out_ref[...] = y
    outb_ref[...] = y.astype(BF16)


def dil_merge_out(os_, lses, w_out, x, g, b, tm=256):
    N, D = x.shape
    W = w_out.shape[0]
    tm = _pick(N, tm)
    row = lambda i: (i, 0)
    fix = lambda i: (0, 0)
    rs = pl.BlockSpec((tm, W), row)
    return pl.pallas_call(
        _dil_merge_kernel,
        out_shape=(jax.ShapeDtypeStruct((N, D), F32), jax.ShapeDtypeStruct((N, D), BF16)),
        grid=(N // tm,),
        in_specs=[rs, rs, rs, rs, rs, rs, pl.BlockSpec((W, D), fix), pl.BlockSpec((tm, D), row),
                  pl.BlockSpec((1, D), fix), pl.BlockSpec((1, D), fix)],
        out_specs=(pl.BlockSpec((tm, D), row), pl.BlockSpec((tm, D), row)),
        compiler_params=_cparams(("parallel",)), name="dil_merge",
    )(*os_, *lses, w_out, x, g.reshape(1, D), b.reshape(1, D))


def dilated_layer(x, xb, w_in, w_out, ln_g, ln_b, B, S):
    N, D = x.shape
    H, dh = DIL_HEADS, HEAD_DIM
    W = H * dh
    w6 = w_in.reshape(D, len(DIL_CONFIGS), 3, W)
    pos = jnp.arange(S)
    os_, lses = [], []
    for gi, (window, dil) in enumerate(DIL_CONFIGS):
        L = S // dil
        wqk = jnp.concatenate([w6[:, gi, 0] * dh ** -0.5, w6[:, gi, 1]], axis=1).astype(BF16)
        wv = w6[:, gi, 2].astype(BF16)
        xp = xb.reshape(B, L, dil, D).transpose(0, 2, 1, 3).reshape(N, D) if dil > 1 else xb
        ppos = pos.reshape(L, dil).T.reshape(S)
        qk = matmul(xp, wqk, out_dtype=BF16, rope=rope_tables(ppos))
        v = matmul(xp, wv, out_dtype=BF16)
        o, lse = dil_attention(qk, v, L, window // dil)
        if dil > 1:
            o = o.reshape(B, dil, L, W).transpose(0, 2, 1, 3).reshape(N, W)
            lse = lse.reshape(B, dil, L, W).transpose(0, 2, 1, 3).reshape(N, W)
        os_.append(o)
        lses.append(lse)
    return dil_merge_out(os_, lses, w_out.astype(BF16), x, ln_g, ln_b)


NSA_HEADS = D_MODEL // HEAD_DIM
NSA_KV_HEADS = 4
NSA_GROUP = NSA_HEADS // NSA_KV_HEADS
CMP_LEN = 32
CMP_STRIDE = 16
CMP_HIDDEN = 256
SEL_LEN = 64
SEL_TOPN = 16
NSA_WINDOW = 512
FORCE_BONUS = 100.0
NSA_TQ = 128
NSA_TK = 128
NSA_CHUNK = 512


def _compress_kernel(x_ref, p_ref, w1a_ref, w1b_ref, w2_ref, o_ref, *, transpose_out):
    x = x_ref[0].astype(F32)
    n = x.shape[0]
    first = jnp.dot((x + p_ref[0:1, :]).astype(BF16), w1a_ref[...], preferred_element_type=F32)
    second = jnp.dot((x + p_ref[1:2, :]).astype(BF16), w1b_ref[...], preferred_element_type=F32)
    hid = first + pltpu.roll(second, n - 1, 0)
    out = jnp.dot(jax.nn.gelu(hid).astype(BF16), w2_ref[...], preferred_element_type=F32)
    if transpose_out:
        o_ref[0] = out.T[:HEAD_DIM, :].astype(o_ref.dtype)
    else:
        o_ref[0] = out[:, :HEAD_DIM].astype(o_ref.dtype)


def nsa_compress(t, pos_emb, w1, w2, transpose_out):
    BK, n, W = t.shape
    half = CMP_STRIDE * HEAD_DIM
    p = pos_emb.reshape(2, half).astype(F32)
    w2p = jnp.zeros((CMP_HIDDEN, LANES), F32).at[:, :HEAD_DIM].set(w2).astype(BF16)
    oshape = (BK, HEAD_DIM, n) if transpose_out else (BK, n, HEAD_DIM)
    fix = lambda i: (0, 0)
    return pl.pallas_call(
        functools.partial(_compress_kernel, transpose_out=transpose_out),
        out_shape=jax.ShapeDtypeStruct(oshape, BF16), grid=(BK,),
        in_specs=[pl.BlockSpec((1, n, W), lambda i: (i, 0, 0)), pl.BlockSpec((2, half), fix),
                  pl.BlockSpec((half, CMP_HIDDEN), fix), pl.BlockSpec((half, CMP_HIDDEN), fix),
                  pl.BlockSpec((CMP_HIDDEN, LANES), fix)],
        out_specs=pl.BlockSpec((1,) + oshape[1:], lambda i: (i, 0, 0)),
        compiler_params=_cparams(("parallel",)), name="nsa_compress",
    )(t, p, w1[:half].astype(BF16), w1[half:].astype(BF16), w2p)


def _col_softmax_step(carry, s, valid, vt):
    m, l, acc = carry
    m_new = jnp.maximum(m, jnp.max(s, axis=0, keepdims=True))
    alpha = jnp.exp(m - m_new)
    p = jnp.exp(s - m_new)
    if valid is not None:
        p = jnp.where(valid, p, 0.0)
    l = alpha * l + jnp.sum(p, axis=0, keepdims=True)
    acc = alpha * acc + jnp.dot(vt, p.astype(BF16), preferred_element_type=F32)
    return m_new, l, acc


def _nsa_attn_t_kernel(q_ref, kc_ref, vct_ref, ks_ref, vst_ref, kw_ref, vwt_ref, oh_ref, gl_ref, ovl_ref, o_ref):
    kh = pl.program_id(1)
    i = pl.program_id(2)
    tq, dh, G = NSA_TQ, HEAD_DIM, NSA_GROUP
    tk = NSA_TK
    M = G * tq
    t0 = i * tq
    slot = kh % 2
    vrows = pl.ds(pl.multiple_of(slot * dh, dh), dh)

    def tpos(shape):
        return t0 + (lax.broadcasted_iota(jnp.int32, shape, 1) & (tq - 1))

    qn = q_ref[...].astype(F32)
    qt_pairs = [qn[:, c * LANES:(c + 1) * LANES].T for c in range(G * dh // LANES)]
    qt = jnp.concatenate([p[h * dh:(h + 1) * dh] for p in qt_pairs for h in range(LANES // dh)], axis=1)
    qt = qt.astype(BF16)

    n_cmp = kc_ref.shape[1]
    s_c = jnp.dot(kc_ref[0], qt, preferred_element_type=F32)
    cend = lax.broadcasted_iota(jnp.int32, (n_cmp, M), 0) * CMP_STRIDE + (CMP_LEN - 1)
    vis = cend <= tpos((n_cmp, M))
    s_c = jnp.where(vis, s_c, NEG_INF)
    e_c = jnp.where(vis, jnp.exp(s_c - jnp.max(s_c, axis=0, keepdims=True)), 0.0)
    p_c = e_c / jnp.maximum(jnp.sum(e_c, axis=0, keepdims=True), 1e-30)
    o_c = jnp.dot(vct_ref[0], p_c.astype(BF16), preferred_element_type=F32)

    psum = p_c[:, 0:tq]
    for g in range(1, G):
        psum = psum + p_c[:, g * tq:(g + 1) * tq]
    ovl = ovl_ref[...]
    imp = sum(jnp.dot(ovl, piece, preferred_element_type=F32) for piece in _split3(psum))
    n_sel = imp.shape[0]
    blk = lax.broadcasted_iota(jnp.int32, (n_sel, tq), 0)
    cur = tpos((n_sel, tq)) // SEL_LEN
    forced = (blk == 0) | (blk == cur) | (blk == cur - 1)
    score = jnp.where(blk <= cur, imp + jnp.where(forced, FORCE_BONUS, 0.0), -1.0)
    chosen = jnp.zeros((n_sel, tq), jnp.bool_)
    for _ in range(min(SEL_TOPN, n_sel)):
        mx, idx = _first_index_of_max(score, blk, 0, n_sel)
        hit = blk == idx
        chosen = chosen | (hit & (mx >= 0.0))
        score = jnp.where(hit, -2.0, score)
    bias = jnp.where(chosen, 0.0, NEG_INF).astype(BF16)
    zero = jnp.zeros_like(qt)
    q_pair = jnp.concatenate([jnp.where(slot == 0, qt, zero), jnp.where(slot == 1, qt, zero)], axis=0)
    pad = jnp.zeros((LANES - n_sel, M), BF16)
    q_aug = jnp.concatenate([q_pair, jnp.concatenate([bias] * G, axis=1), pad], axis=0)

    init = (jnp.full((1, M), NEG_INF, F32), jnp.zeros((1, M), F32), jnp.zeros((dh, M), F32))

    def vt_cat(ref, first_tile, n):
        return jnp.concatenate([ref[0, 0, first_tile + j, vrows, :] for j in range(n)], axis=1)

    ch = NSA_CHUNK
    per_chunk = ch // tk

    def sel_chunk(c, carry, diagonal):
        rows = pl.ds(pl.multiple_of(c * ch, ch), ch)
        k_aug = jnp.concatenate([ks_ref[rows, :], oh_ref[rows, :]], axis=1)
        s = jnp.dot(k_aug, q_aug, preferred_element_type=F32)
        if diagonal:
            kpos = c * ch + lax.broadcasted_iota(jnp.int32, (ch, M), 0)
            s = jnp.where(kpos <= tpos((ch, M)), s, NEG_INF)
        return _col_softmax_step(carry, s, None, vt_cat(vst_ref, c * per_chunk, per_chunk))

    n_full = t0 // ch
    carry = lax.fori_loop(0, n_full, lambda c, cr: sel_chunk(c, cr, False), init)
    _, l_s, acc_s = sel_chunk(n_full, carry, True)
    o_s = acc_s / jnp.maximum(l_s, 1e-30)

    n_wt = NSA_WINDOW // tk + 1
    wt0 = jnp.maximum(i + 1 - n_wt, 0)
    wrows = pl.ds(pl.multiple_of(wt0 * tk, tk), n_wt * tk)
    s_w = jnp.dot(kw_ref[wrows, :], q_pair, preferred_element_type=F32)
    dist = tpos(s_w.shape) - (wt0 * tk + lax.broadcasted_iota(jnp.int32, s_w.shape, 0))
    near = (dist >= 0) & (dist < NSA_WINDOW)
    s_w = jnp.where(near, s_w, NEG_INF)
    e_w = jnp.exp(s_w - jnp.max(s_w, axis=0, keepdims=True))
    l_w = jnp.sum(e_w, axis=0, keepdims=True)
    o_w = jnp.dot(vt_cat(vwt_ref, wt0, n_wt), e_w.astype(BF16), preferred_element_type=F32) / jnp.maximum(l_w, 1e-30)

    gates = jax.nn.sigmoid(gl_ref[...].T)
    outs = []
    for g in range(G):
        cols = slice(g * tq, (g + 1) * tq)
        outs.append(gates[g:g + 1] * o_c[:, cols] + gates[G + g:G + g + 1] * o_s[:, cols]
                    + gates[2 * G + g:2 * G + g + 1] * o_w[:, cols])
    per = LANES // dh
    o_ref[...] = jnp.concatenate(
        [jnp.concatenate(outs[c * per:(c + 1) * per], axis=0).T for c in range(G // per)], axis=1).astype(o_ref.dtype)


def nsa_attention_t(roped, vst, vwt, k_cmp, v_cmpt, gl, B, S):
    H, KH, G, dh = NSA_HEADS, NSA_KV_HEADS, NSA_GROUP, HEAD_DIM
    tq, tk = NSA_TQ, NSA_TK
    assert tq == tk and S % NSA_CHUNK == 0 and NSA_CHUNK % tk == 0 and S >= NSA_WINDOW + tq and G * dh == 2 * LANES
    nt = S // tq
    n_cmp = k_cmp.shape[1]
    n_sel = S // SEL_LEN
    c0 = np.arange(n_cmp)[None, :] * CMP_STRIDE
    s0 = np.arange(n_sel)[:, None] * SEL_LEN
    ovl = jnp.asarray((c0 < s0 + SEL_LEN) & (c0 + CMP_LEN - 1 >= s0), BF16)
    onehot = jnp.asarray(np.arange(S)[:, None] // SEL_LEN == np.arange(LANES)[None, :], BF16)
    ks_col = (H * dh + KH * dh) // LANES
    kw_col = (H * dh + 2 * KH * dh) // LANES
    qspec = pl.BlockSpec((tq, G * dh), lambda b, h, i: (b * nt + i, h))
    vspec = pl.BlockSpec((1, 1, nt, LANES, tk), lambda b, h, i: (b, h // 2, 0, 0, 0))
    return pl.pallas_call(
        _nsa_attn_t_kernel, out_shape=jax.ShapeDtypeStruct((B * S, H * dh), BF16), grid=(B, KH, nt),
        in_specs=[qspec,
                  pl.BlockSpec((1, n_cmp, dh), lambda b, h, i: (b * KH + h, 0, 0)),
                  pl.BlockSpec((1, dh, n_cmp), lambda b, h, i: (b * KH + h, 0, 0)),
                  pl.BlockSpec((S, LANES), lambda b, h, i: (b, ks_col + h // 2)), vspec,
                  pl.BlockSpec((S, LANES), lambda b, h, i: (b, kw_col + h // 2)), vspec,
                  pl.BlockSpec((S, LANES), lambda b, h, i: (0, 0)),
                  pl.BlockSpec((tq, LANES), lambda b, h, i: (b * nt + i, h)),
                  pl.BlockSpec((n_sel, n_cmp), lambda b, h, i: (0, 0))],
        out_specs=qspec,
        compiler_params=_cparams(("parallel", "parallel", "arbitrary")), name="nsa_attn",
    )(roped, k_cmp, v_cmpt, roped, vst, roped, vwt, onehot, gl, ovl)


def nsa_mixer(xb, w_in, ck_pos, ck_w1, ck_w2, cv_pos, cv_w1, cv_w2, B, S):
    H, KH, G, dh = NSA_HEADS, NSA_KV_HEADS, NSA_GROUP, HEAD_DIM
    kvw = KH * dh
    cuts = np.cumsum([H * dh, kvw, kvw, kvw, kvw, kvw, kvw]).tolist()
    wq, wkc, wvc, wks, wvs, wkw, wvw, wgl = jnp.split(w_in, cuts, axis=1)
    w_rope = jnp.concatenate([wq * dh ** -0.5, wkc, wks, wkw], axis=1).astype(BF16)
    w_val = jnp.concatenate([wvc, wvs, wvw], axis=1).astype(BF16)
    w_gate = jnp.zeros((w_in.shape[0], KH, LANES), F32).at[:, :, :3 * G].set(
        wgl.reshape(-1, KH, G, 3).transpose(0, 1, 3, 2).reshape(-1, KH, 3 * G)).reshape(-1, KH * LANES).astype(BF16)
    roped = matmul(xb, w_rope, out_dtype=BF16, rope=rope_tables(jnp.arange(S)), tn=w_rope.shape[1] // 2)
    vals = matmul(xb, w_val, out_dtype=BF16)
    gl = matmul(xb, w_gate, out_dtype=F32)

    n16 = S // CMP_STRIDE

    def blocks16(t):
        return t.reshape(B, n16, CMP_STRIDE, KH, dh).transpose(0, 3, 1, 2, 4).reshape(B * KH, n16, CMP_STRIDE * dh)

    def vt_tiles(t):
        return t.reshape(B, S // NSA_TK, NSA_TK, KH * dh // LANES, LANES).transpose(0, 3, 1, 4, 2)

    k_cmp = nsa_compress(blocks16(roped[:, H * dh:H * dh + kvw]), ck_pos, ck_w1, ck_w2, False)
    v_cmpt = nsa_compress(blocks16(vals[:, :kvw]), cv_pos, cv_w1, cv_w2, True)
    return nsa_attention_t(roped, vt_tiles(vals[:, kvw:2 * kvw]), vt_tiles(vals[:, 2 * kvw:]), k_cmp, v_cmpt, gl, B, S)


def kernel(x, l0_nsa_w_in, l0_nsa_w_out, l0_nsa_ck_pos, l0_nsa_ck_w1, l0_nsa_ck_w2, l0_nsa_cv_pos, l0_nsa_cv_w1, l0_nsa_cv_w2, l0_ln1_g, l0_ln1_b, l0_router_w, l0_router_b, l0_moe_w_in, l0_moe_w_out, l0_shared_w_in, l0_shared_w_out, l0_ln2_g, l0_ln2_b, l1_gla_w_in, l1_gla_w_a2, l1_gla_b_a, l1_gla_norm_g, l1_gla_w_out, l1_ln1_g, l1_ln1_b, l1_router_w, l1_router_b, l1_moe_w_in, l1_moe_w_out, l1_shared_w_in, l1_shared_w_out, l1_ln2_g, l1_ln2_b, l2_dil_w_in, l2_dil_w_out, l2_ln1_g, l2_ln1_b, l2_router_w, l2_router_b, l2_moe_w_in, l2_moe_w_out, l2_shared_w_in, l2_shared_w_out, l2_ln2_g, l2_ln2_b, l3_nsa_w_in, l3_nsa_w_out, l3_nsa_ck_pos, l3_nsa_ck_w1, l3_nsa_ck_w2, l3_nsa_cv_pos, l3_nsa_cv_w1, l3_nsa_cv_w2, l3_ln1_g, l3_ln1_b, l3_router_w, l3_router_b, l3_moe_w_in, l3_moe_w_out, l3_shared_w_in, l3_shared_w_out, l3_ln2_g, l3_ln2_b):
    B, S, D = x.shape
    xf = x.reshape(B * S, D)
    xb = xf.astype(BF16)

    h = nsa_mixer(xb, l0_nsa_w_in, l0_nsa_ck_pos, l0_nsa_ck_w1, l0_nsa_ck_w2, l0_nsa_cv_pos, l0_nsa_cv_w1, l0_nsa_cv_w2, B, S)
    xf, xb = matmul_res_ln(h, l0_nsa_w_out.astype(BF16), xf, l0_ln1_g, l0_ln1_b)
    xf, xb = moe_layer(xf, xb, l0_router_w, l0_router_b, l0_moe_w_in, l0_moe_w_out, l0_shared_w_in, l0_shared_w_out, l0_ln2_g, l0_ln2_b)

    h = gla_mixer(xb, l1_gla_w_in, l1_gla_w_a2, l1_gla_b_a, l1_gla_norm_g, B, S)
    xf, xb = matmul_res_ln(h, l1_gla_w_out.astype(BF16), xf, l1_ln1_g, l1_ln1_b)
    xf, xb = moe_layer(xf, xb, l1_router_w, l1_router_b, l1_moe_w_in, l1_moe_w_out, l1_shared_w_in, l1_shared_w_out, l1_ln2_g, l1_ln2_b)

    xf, xb = dilated_layer(xf, xb, l2_dil_w_in, l2_dil_w_out, l2_ln1_g, l2_ln1_b, B, S)
    xf, xb = moe_layer(xf, xb, l2_router_w, l2_router_b, l2_moe_w_in, l2_moe_w_out, l2_shared_w_in, l2_shared_w_out, l2_ln2_g, l2_ln2_b)

    h = nsa_mixer(xb, l3_nsa_w_in, l3_nsa_ck_pos, l3_nsa_ck_w1, l3_nsa_ck_w2, l3_nsa_cv_pos, l3_nsa_cv_w1, l3_nsa_cv_w2, B, S)
    xf, xb = matmul_res_ln(h, l3_nsa_w_out.astype(BF16), xf, l3_ln1_g, l3_ln1_b)
    xf, xb = moe_layer(xf, xb, l3_router_w, l3_router_b, l3_moe_w_in, l3_moe_w_out, l3_shared_w_in, l3_shared_w_out, l3_ln2_g, l3_ln2_b)
    return xf.reshape(B, S, D)
```

```python
import functools
import math

import jax
import jax.numpy as jnp
import numpy as np
from jax import lax
from jax.experimental import pallas as pl
from jax.experimental.pallas import tpu as pltpu

F32 = jnp.float32
BF16 = jnp.bfloat16

D_MODEL = 1024
DEPTH = 4
HEAD_DIM = 64
ROPE_THETA = 500000.0
ROPE_DIM = HEAD_DIM // 4
ROPE_HALF = ROPE_DIM // 2

N_EXPERTS = 64
TOP_K = 8
N_GROUPS = 8
TOPK_GROUPS = 4
GROUP_SIZE = N_EXPERTS // N_GROUPS
D_EXPERT = 256
ROUTED_SCALE = 2.5

DN_ALPHA = (2.0 * DEPTH) ** 0.25
LN_EPS = 1e-5
NEG_INF = -1e30

LANES = 128
VMEM_LIMIT = 48 * 1024 * 1024


def _cparams(sem, **kw):
    return pltpu.CompilerParams(dimension_semantics=sem, vmem_limit_bytes=VMEM_LIMIT, **kw)


def _pick(n, pref):
    t = min(pref, n)
    while n % t:
        t //= 2
    return t


def _mm_kernel(x_ref, w_ref, o_ref):
    o_ref[...] = jnp.dot(x_ref[...], w_ref[...], preferred_element_type=F32).astype(o_ref.dtype)


def _mm_rope_kernel(x_ref, w_ref, c_ref, sm_ref, sp_ref, o_ref):
    y = jnp.dot(x_ref[...], w_ref[...], preferred_element_type=F32)
    reps = y.shape[1] // LANES
    c = jnp.tile(c_ref[...], (1, reps))
    sm = jnp.tile(sm_ref[...], (1, reps))
    sp = jnp.tile(sp_ref[...], (1, reps))
    up = pltpu.roll(y, y.shape[1] - ROPE_HALF, 1)
    dn = pltpu.roll(y, ROPE_HALF, 1)
    o_ref[...] = (y * c + up * sm + dn * sp).astype(o_ref.dtype)


def matmul(x, w, out_dtype=F32, rope=None, tm=None, tn=1024):
    M, K = x.shape
    N = w.shape[1]
    tm = _pick(M, tm or (1024 if rope is None else 512))
    tn = _pick(N, tn)
    grid = (N // tn, M // tm)
    x_spec = pl.BlockSpec((tm, K), lambda j, i: (i, 0))
    w_spec = pl.BlockSpec((K, tn), lambda j, i: (0, j))
    o_spec = pl.BlockSpec((tm, tn), lambda j, i: (i, j))
    if rope is None:
        return pl.pallas_call(
            _mm_kernel, out_shape=jax.ShapeDtypeStruct((M, N), out_dtype), grid=grid,
            in_specs=[x_spec, w_spec], out_specs=o_spec,
            compiler_params=_cparams(("parallel", "parallel")), name="mm")(x, w)
    R = rope[0].shape[0]
    tm = _pick(R, tm)
    grid = (N // tn, M // tm)
    x_spec = pl.BlockSpec((tm, K), lambda j, i: (i, 0))
    o_spec = pl.BlockSpec((tm, tn), lambda j, i: (i, j))
    nr = R // tm
    t_spec = pl.BlockSpec((tm, LANES), lambda j, i: (i % nr, 0))
    return pl.pallas_call(
        _mm_rope_kernel, out_shape=jax.ShapeDtypeStruct((M, N), out_dtype), grid=grid,
        in_specs=[x_spec, w_spec, t_spec, t_spec, t_spec], out_specs=o_spec,
        compiler_params=_cparams(("parallel", "parallel")), name="mm_rope")(x, w, *rope)


def rope_tables(pos):
    inv = ROPE_THETA ** (-jnp.arange(ROPE_HALF, dtype=F32) * 2.0 / ROPE_DIM)
    ang = pos.astype(F32)[:, None] * inv[None, :]
    cos, sin = jnp.cos(ang), jnp.sin(ang)
    n = pos.shape[0]
    ones = jnp.ones((n, HEAD_DIM - ROPE_DIM), F32)
    zeros = jnp.zeros((n, HEAD_DIM - ROPE_DIM), F32)
    zh = jnp.zeros((n, ROPE_HALF), F32)
    c = jnp.concatenate([cos, cos, ones], 1)
    sm = jnp.concatenate([-sin, zh, zeros], 1)
    sp = jnp.concatenate([zh, sin, zeros], 1)
    return tuple(jnp.tile(t, (1, LANES // HEAD_DIM)) for t in (c, sm, sp))


def _ln(v, g, b):
    mu = jnp.mean(v, axis=-1, keepdims=True)
    d = v - mu
    var = jnp.mean(d * d, axis=-1, keepdims=True)
    return d * lax.rsqrt(var + LN_EPS) * g + b


def _mm_res_ln_kernel(a_ref, w_ref, x_ref, g_ref, b_ref, o_ref, ob_ref):
    h = jnp.dot(a_ref[...], w_ref[...], preferred_element_type=F32)
    y = _ln(DN_ALPHA * x_ref[...] + h, g_ref[...], b_ref[...])
    o_ref[...] = y
    ob_ref[...] = y.astype(BF16)


def matmul_res_ln(a, w, x, g, b, tm=512):
    M, K = a.shape
    D = w.shape[1]
    tm = _pick(M, tm)
    row = lambda i: (i, 0)
    fix = lambda i: (0, 0)
    return pl.pallas_call(
        _mm_res_ln_kernel,
        out_shape=(jax.ShapeDtypeStruct((M, D), F32), jax.ShapeDtypeStruct((M, D), BF16)),
        grid=(M // tm,),
        in_specs=[pl.BlockSpec((tm, K), row), pl.BlockSpec((K, D), fix), pl.BlockSpec((tm, D), row),
                  pl.BlockSpec((1, D), fix), pl.BlockSpec((1, D), fix)],
        out_specs=(pl.BlockSpec((tm, D), row), pl.BlockSpec((tm, D), row)),
        compiler_params=_cparams(("parallel",)), name="mm_res_ln")(a, w, x, g.reshape(1, D), b.reshape(1, D))


def _first_index_of_max(v, iota, axis, n):
    m = jnp.max(v, axis=axis, keepdims=True)
    idx = jnp.min(jnp.where(v == m, iota, n), axis=axis, keepdims=True)
    return m, idx


def _router_kernel(x_ref, wh_ref, wl_ref, rb_ref, tri_ref, slot_ref, gatew_ref, cnt_ref):
    x = x_ref[...]
    xh = x.astype(BF16)
    xl = (x - xh.astype(F32)).astype(BF16)
    dn = (((1,), (1,)), ((), ()))
    logits = (lax.dot_general(wh_ref[...], xh, dn, preferred_element_type=F32)
              + lax.dot_general(wh_ref[...], xl, dn, preferred_element_type=F32)
              + lax.dot_general(wl_ref[...], xh, dn, preferred_element_type=F32))
    tm = logits.shape[1]
    s = jax.nn.sigmoid(logits)
    sb = s + rb_ref[...]
    sb3 = sb.reshape(N_GROUPS, GROUP_SIZE, tm)
    io3 = lax.broadcasted_iota(jnp.int32, sb3.shape, 1)
    m1, i1 = _first_index_of_max(sb3, io3, 1, GROUP_SIZE)
    m2 = jnp.max(jnp.where(io3 == i1, -jnp.inf, sb3), axis=1, keepdims=True)
    gs = (m1 + m2).reshape(N_GROUPS, tm)
    iog = lax.broadcasted_iota(jnp.int32, gs.shape, 0)
    gmask = jnp.zeros(gs.shape, jnp.bool_)
    for _ in range(TOPK_GROUPS):
        _, gi = _first_index_of_max(gs, iog, 0, N_GROUPS)
        pick = iog == gi
        gmask = gmask | pick
        gs = jnp.where(pick, -jnp.inf, gs)
    emask = jnp.broadcast_to(gmask.reshape(N_GROUPS, 1, tm), sb3.shape).reshape(N_EXPERTS, tm)
    cand = jnp.where(emask, sb, NEG_INF)
    ioe = lax.broadcasted_iota(jnp.int32, cand.shape, 0)
    sel = jnp.zeros(cand.shape, jnp.bool_)
    picked = []
    for _ in range(TOP_K):
        _, ei = _first_index_of_max(cand, ioe, 0, N_EXPERTS)
        pick = ioe == ei
        picked.append(jnp.sum(jnp.where(pick, s, 0.0), axis=0, keepdims=True))
        sel = sel | pick
        cand = jnp.where(pick, -jnp.inf, cand)
    total = picked[0]
    for g in picked[1:]:
        total = total + g
    gatew_ref[...] = jnp.where(sel, s / total * ROUTED_SCALE, 0.0)
    routed = jnp.where(sel, 1.0, 0.0)
    before = jnp.dot(routed.astype(BF16), tri_ref[...], preferred_element_type=F32)
    slot_ref[...] = jnp.where(sel, before, -1.0).astype(jnp.int32)
    cnt_ref[0] = jnp.broadcast_to(jnp.sum(routed, axis=1, keepdims=True), cnt_ref.shape[1:]).astype(jnp.int32)


def moe_router(x, router_w, router_b):
    N, D = x.shape
    tm = MOE_TILE
    assert N % tm == 0
    wt = router_w.T
    wh = wt.astype(BF16)
    wl = (wt - wh.astype(F32)).astype(BF16)
    tri = (jnp.arange(tm)[:, None] < jnp.arange(tm)[None, :]).astype(BF16)
    fix = lambda i: (0, 0)
    col = lambda i: (0, i)
    slot, gatew, cnt = pl.pallas_call(
        _router_kernel,
        out_shape=(jax.ShapeDtypeStruct((N_EXPERTS, N), jnp.int32), jax.ShapeDtypeStruct((N_EXPERTS, N), F32),
                   jax.ShapeDtypeStruct((N // tm, N_EXPERTS, LANES), jnp.int32)),
        grid=(N // tm,),
        in_specs=[pl.BlockSpec((tm, D), lambda i: (i, 0)), pl.BlockSpec((N_EXPERTS, D), fix),
                  pl.BlockSpec((N_EXPERTS, D), fix), pl.BlockSpec((N_EXPERTS, 1), fix), pl.BlockSpec((tm, tm), fix)],
        out_specs=(pl.BlockSpec((N_EXPERTS, tm), col), pl.BlockSpec((N_EXPERTS, tm), col),
                   pl.BlockSpec((1, N_EXPERTS, LANES), lambda i: (i, 0, 0))),
        compiler_params=_cparams(("arbitrary",)), name="moe_router",
    )(x, wh, wl, router_b.reshape(N_EXPERTS, 1).astype(F32), tri)
    return slot, gatew, cnt[:, :, 0]


MOE_TILE = 512
MOE_CAP = 96
MOE_EGROUP = 16
MOE_BLOCK = 512
MOE_ALIGN = 16


def _slot_matrix(slot_rows, first, weights=None):
    n_e, T = slot_rows.shape
    r = lax.broadcasted_iota(jnp.int32, (MOE_CAP, T), 0) + first
    pieces = []
    for e in range(n_e):
        hit = slot_rows[e:e + 1, :] == r
        w = 1.0 if weights is None else weights[e:e + 1, :]
        pieces.append(jnp.where(hit, w, 0.0).astype(BF16))
    return pieces[0] if n_e == 1 else jnp.concatenate(pieces, axis=0)


def _n_windows(n_rows):
    return (n_rows + MOE_CAP - 1) // MOE_CAP


def _dispatch_kernel(base_ref, cnt_ref, zrow_ref, slot_ref, xb_ref, xs_ref, obuf, ovbuf, zbuf, sems, ovsem, zsem):
    t = pl.program_id(0)
    C, EG, E = MOE_CAP, MOE_EGROUP, N_EXPERTS
    n_groups = E // EG
    tz = zbuf.shape[0]

    @pl.when(t == 0)
    def _():
        zbuf[...] = jnp.zeros_like(zbuf)

        def zcopy(j):
            return pltpu.make_async_copy(zbuf, xs_ref.at[pl.ds(pl.multiple_of(zrow_ref[j], tz), tz)], zsem)

        def zstart(j, c):
            @pl.when(zrow_ref[j] >= 0)
            def _():
                zcopy(j).start()
            return c

        def zwait(j, c):
            @pl.when(zrow_ref[j] >= 0)
            def _():
                zcopy(j).wait()
            return c

        lax.fori_loop(0, zrow_ref.shape[0], zstart, 0)
        lax.fori_loop(0, zrow_ref.shape[0], zwait, 0)

    xb = xb_ref[...]

    def window(e, extra):
        return xs_ref.at[pl.ds(pl.multiple_of(base_ref[t * E + e] + extra, MOE_ALIGN), C)]

    def copies(g):
        return [pltpu.make_async_copy(obuf.at[g % 2, pl.ds(e * C, C)], window(g * EG + e, 0), sems.at[g % 2])
                for e in range(EG)]

    for g in range(n_groups):
        if g >= 2:
            for cp in copies(g - 2):
                cp.wait()
        onehot = _slot_matrix(slot_ref[g * EG:(g + 1) * EG, :], 0)
        obuf[g % 2] = jnp.dot(onehot, xb, preferred_element_type=F32).astype(BF16)
        for cp in copies(g):
            cp.start()
    for g in range(max(n_groups - 2, 0), n_groups):
        for cp in copies(g):
            cp.wait()

    def overflow(e, c):
        def chunk(j, c2):
            onehot = _slot_matrix(slot_ref[pl.ds(e, 1), :], j * C)
            ovbuf[...] = jnp.dot(onehot, xb, preferred_element_type=F32).astype(BF16)
            cp = pltpu.make_async_copy(ovbuf, window(e, j * C), ovsem)
            cp.start()
            cp.wait()
            return c2

        return lax.fori_loop(1, _n_windows(cnt_ref[t * E + e]), chunk, c)

    lax.fori_loop(0, E, overflow, 0)


def moe_dispatch(xb, slot, base, cnt, zrow, P):
    N, D = xb.shape
    T, C, EG = MOE_TILE, MOE_CAP, MOE_EGROUP
    grid_spec = pltpu.PrefetchScalarGridSpec(
        num_scalar_prefetch=3, grid=(N // T,),
        in_specs=[pl.BlockSpec((N_EXPERTS, T), lambda i, *_: (0, i)),
                  pl.BlockSpec((T, D), lambda i, *_: (i, 0))],
        out_specs=pl.BlockSpec(memory_space=pl.ANY),
        scratch_shapes=[pltpu.VMEM((2, EG * C, D), BF16), pltpu.VMEM((C, D), BF16), pltpu.VMEM((MOE_BLOCK, D), BF16),
                        pltpu.SemaphoreType.DMA((2,)), pltpu.SemaphoreType.DMA(()), pltpu.SemaphoreType.DMA(())])
    return pl.pallas_call(
        _dispatch_kernel, out_shape=jax.ShapeDtypeStruct((P, D), BF16), grid_spec=grid_spec,
        compiler_params=_cparams(("arbitrary",), has_side_effects=True), name="moe_dispatch",
    )(base.reshape(-1), cnt.reshape(-1), zrow, slot, xb)


def _swiglu(x_bf16, w_in, w_out):
    h = jnp.dot(x_bf16, w_in, preferred_element_type=F32)
    f = h.shape[1] // 2
    a = jax.nn.silu(h[:, :f]) * h[:, f:]
    return jnp.dot(a.astype(BF16), w_out, preferred_element_type=F32)


def _expert_kernel(blk_e_ref, used_ref, xs_ref, wi_ref, wo_ref, y_ref):
    del blk_e_ref
    live = pl.program_id(0) < used_ref[0]

    @pl.when(live)
    def _():
        y_ref[...] = _swiglu(xs_ref[...], wi_ref[0], wo_ref[0]).astype(y_ref.dtype)

    @pl.when(jnp.logical_not(live))
    def _():
        y_ref[...] = jnp.zeros_like(y_ref)


def moe_experts(xs, blk_e, n_used, w_in_e, w_out_e):
    P, D = xs.shape
    tm = MOE_BLOCK
    F2 = w_in_e.shape[2]
    spare = P // tm - 1
    rows = lambda i, be, nu: (jnp.where(i < nu[0], i, spare), 0)
    grid_spec = pltpu.PrefetchScalarGridSpec(
        num_scalar_prefetch=2, grid=(P // tm,),
        in_specs=[pl.BlockSpec((tm, D), rows),
                  pl.BlockSpec((1, D, F2), lambda i, be, nu: (be[i], 0, 0)),
                  pl.BlockSpec((1, F2 // 2, D), lambda i, be, nu: (be[i], 0, 0))],
        out_specs=pl.BlockSpec((tm, D), rows))
    return pl.pallas_call(
        _expert_kernel, out_shape=jax.ShapeDtypeStruct((P, D), BF16), grid_spec=grid_spec,
        compiler_params=_cparams(("arbitrary",)), name="moe_experts")(blk_e, n_used, xs, w_in_e, w_out_e)


def _combine_kernel(base_ref, cnt_ref, slot_ref, gw_ref, x_ref, wi_ref, wo_ref, g_ref, b_ref, ys_ref, o_ref, ob_ref,
                    ybuf, ovbuf, acc_ref, sems, ovsem):
    t = pl.program_id(0)
    C, EG, E = MOE_CAP, MOE_EGROUP, N_EXPERTS
    n_groups = E // EG
    rows_in = (((0,), (0,)), ((), ()))

    def window(e, extra):
        return ys_ref.at[pl.ds(pl.multiple_of(base_ref[t * E + e] + extra, MOE_ALIGN), C)]

    def copies(g):
        return [pltpu.make_async_copy(window(g * EG + e, 0), ybuf.at[g % 2, pl.ds(e * C, C)], sems.at[g % 2])
                for e in range(EG)]

    for cp in copies(0):
        cp.start()
    x = x_ref[...]
    acc_ref[...] = _swiglu(x.astype(BF16), wi_ref[...], wo_ref[...])
    for g in range(n_groups):
        if g + 1 < n_groups:
            for cp in copies(g + 1):
                cp.start()
        for cp in copies(g):
            cp.wait()
        spread = _slot_matrix(slot_ref[g * EG:(g + 1) * EG, :], 0, gw_ref[g * EG:(g + 1) * EG, :])
        acc_ref[...] += lax.dot_general(spread, ybuf[g % 2], rows_in, preferred_element_type=F32)

    def overflow(e, c):
        def chunk(j, c2):
            cp = pltpu.make_async_copy(window(e, j * C), ovbuf, ovsem)
            cp.start()
            cp.wait()
            spread = _slot_matrix(slot_ref[pl.ds(e, 1), :], j * C, gw_ref[pl.ds(e, 1), :])
            acc_ref[...] += lax.dot_general(spread, ovbuf[...], rows_in, preferred_element_type=F32)
            return c2

        return lax.fori_loop(1, _n_windows(cnt_ref[t * E + e]), chunk, c)

    lax.fori_loop(0, E, overflow, 0)
    y = _ln(DN_ALPHA * x + acc_ref[...], g_ref[...], b_ref[...])
    o_ref[...] = y
    ob_ref[...] = y.astype(BF16)


def moe_combine(x, ys, slot, gatew, base, cnt, w_in_s, w_out_s, g, b):
    N, D = x.shape
    T, C, EG = MOE_TILE, MOE_CAP, MOE_EGROUP
    F2 = w_in_s.shape[1]
    row = lambda i, *_: (i, 0)
    col = lambda i, *_: (0, i)
    fix = lambda i, *_: (0, 0)
    grid_spec = pltpu.PrefetchScalarGridSpec(
        num_scalar_prefetch=2, grid=(N // T,),
        in_specs=[pl.BlockSpec((N_EXPERTS, T), col), pl.BlockSpec((N_EXPERTS, T), col), pl.BlockSpec((T, D), row),
                  pl.BlockSpec((D, F2), fix), pl.BlockSpec((F2 // 2, D), fix),
                  pl.BlockSpec((1, D), fix), pl.BlockSpec((1, D), fix),
                  pl.BlockSpec(memory_space=pl.ANY)],
        out_specs=(pl.BlockSpec((T, D), row), pl.BlockSpec((T, D), row)),
        scratch_shapes=[pltpu.VMEM((2, EG * C, D), BF16), pltpu.VMEM((C, D), BF16), pltpu.VMEM((T, D), F32),
                        pltpu.SemaphoreType.DMA((2,)), pltpu.SemaphoreType.DMA(())])
    return pl.pallas_call(
        _combine_kernel,
        out_shape=(jax.ShapeDtypeStruct((N, D), F32), jax.ShapeDtypeStruct((N, D), BF16)),
        grid_spec=grid_spec,
        compiler_params=_cparams(("arbitrary",)), name="moe_combine",
    )(base.reshape(-1), cnt.reshape(-1), slot, gatew, x, w_in_s, w_out_s, g.reshape(1, D), b.reshape(1, D), ys)


def moe_layer(x, xb, router_w, router_b, w_in_e, w_out_e, w_in_s, w_out_s, ln_g, ln_b):
    N, D = x.shape
    E, C, A, tm_e = N_EXPERTS, MOE_CAP, MOE_ALIGN, MOE_BLOCK
    n_t = N // MOE_TILE
    slot, gatew, cnt = moe_router(x, router_w, router_b)
    seg = (cnt + A - 1) // A * A
    padded = (jnp.sum(seg, axis=0) + C + tm_e - 1) // tm_e * tm_e
    pad_end = jnp.cumsum(padded)
    pad_start = pad_end - padded
    base = (pad_start[None, :] + jnp.cumsum(seg, axis=0) - seg).astype(jnp.int32)
    max_rows = N * TOP_K + n_t * E * (A - 1) + E * (C + tm_e - 1)
    n_blk = -(-max_rows // tm_e) + 1
    blk_row = jnp.arange(n_blk, dtype=jnp.int32) * tm_e
    blk_e = jnp.minimum(jnp.sum(pad_end[None, :] <= blk_row[:, None], axis=1), E - 1).astype(jnp.int32)
    n_used = (pad_end[-1:] // tm_e).astype(jnp.int32)
    zrow = jnp.concatenate([pad_end - tm_e, jnp.where(padded >= 2 * tm_e, pad_end - 2 * tm_e, -1)]).astype(jnp.int32)
    xs = moe_dispatch(xb, slot, base, cnt, zrow, n_blk * tm_e)
    ys = moe_experts(xs, blk_e, n_used, w_in_e.astype(BF16), w_out_e.astype(BF16))
    return moe_combine(x, ys, slot, gatew, base, cnt, w_in_s.astype(BF16), w_out_s.astype(BF16), ln_g, ln_b)


GLA_HEADS = 4
GLA_DK = D_MODEL // 2 // GLA_HEADS
GLA_DV = D_MODEL // GLA_HEADS
GLA_GATE_RANK = 16
GLA_TAU = 16.0
GLA_CHUNK = 64


def _split3(v):
    h1 = v.astype(BF16)
    r1 = v - h1.astype(F32)
    h2 = r1.astype(BF16)
    h3 = (r1 - h2.astype(F32)).astype(BF16)
    return h1, h2, h3


def _gla_kernel(qkvr_ref, a_ref, wah_ref, wal_ref, ba_ref, ng_ref, tri_ref, o_ref, st_ref):
    H, dk, dv, C = GLA_HEADS, GLA_DK, GLA_DV, GLA_CHUNK

    @pl.when(pl.program_id(1) == 0)
    def _():
        st_ref[...] = jnp.zeros_like(st_ref)

    T = qkvr_ref.shape[0]
    a = a_ref[...]
    ah = a.astype(BF16)
    al = (a - ah.astype(F32)).astype(BF16)
    glog = (jnp.dot(ah, wah_ref[...], preferred_element_type=F32)
            + jnp.dot(al, wah_ref[...], preferred_element_type=F32)
            + jnp.dot(ah, wal_ref[...], preferred_element_type=F32)) + ba_ref[...]
    log_a = jax.nn.log_sigmoid(glog) / GLA_TAU
    tri = tri_ref[...]
    rr = lax.broadcasted_iota(jnp.int32, (C, C), 0)
    cc = lax.broadcasted_iota(jnp.int32, (C, C), 1)
    causal = rr >= cc
    ng = ng_ref[...]
    ct = (((1,), (1,)), ((), ()))
    c0 = (((0,), (0,)), ((), ()))
    for c in range(T // C):
        rows = slice(c * C, (c + 1) * C)
        for h in range(H):
            la = log_a[rows, h * dk:(h + 1) * dk]
            p1, p2, p3 = _split3(la)
            b = (jnp.dot(tri, p1, preferred_element_type=F32) + jnp.dot(tri, p2, preferred_element_type=F32)
                 + jnp.dot(tri, p3, preferred_element_type=F32))
            b_last = b[C - 1:C, :]
            q = qkvr_ref[rows, h * dk:(h + 1) * dk].astype(F32)
            k = qkvr_ref[rows, H * dk + h * dk:H * dk + (h + 1) * dk].astype(F32)
            v = qkvr_ref[rows, 2 * H * dk + h * dv:2 * H * dk + (h + 1) * dv]
            r = qkvr_ref[rows, 2 * H * dk + H * dv + h * dv:2 * H * dk + H * dv + (h + 1) * dv].astype(F32)
            qg = (q * jnp.exp(b)).astype(BF16)
            kg = (k * jnp.exp(-b)).astype(BF16)
            kd = (k * jnp.exp(b_last - b)).astype(BF16)
            att = jnp.where(causal, lax.dot_general(qg, kg, ct, preferred_element_type=F32), 0.0)
            st = st_ref[h]
            o = (jnp.dot(att.astype(BF16), v, preferred_element_type=F32)
                 + lax.dot_general(qg, st.astype(BF16), ct, preferred_element_type=F32))
            st_ref[h] = jnp.exp(b_last) * st + lax.dot_general(v, kd, c0, preferred_element_type=F32)
            o = o * lax.rsqrt(jnp.mean(o * o, axis=-1, keepdims=True) + LN_EPS) * ng
            o_ref[rows, h * dv:(h + 1) * dv] = (o * jax.nn.silu(r)).astype(o_ref.dtype)


def gla_core(qkvr, a, w_a2, b_a, norm_g, B, S, tile=256):
    H, dk, dv, C = GLA_HEADS, GLA_DK, GLA_DV, GLA_CHUNK
    N, W = qkvr.shape
    tile = _pick(S, tile)
    nt = S // tile
    wa = jnp.zeros((LANES, H * dk), F32).at[:GLA_GATE_RANK].set(w_a2)
    wah = wa.astype(BF16)
    wal = (wa - wah.astype(F32)).astype(BF16)
    tri = (jnp.arange(C)[:, None] >= jnp.arange(C)[None, :]).astype(BF16)
    row = lambda b, t: (b * nt + t, 0)
    fix = lambda b, t: (0, 0)
    return pl.pallas_call(
        _gla_kernel, out_shape=jax.ShapeDtypeStruct((N, H * dv), BF16), grid=(B, nt),
        in_specs=[pl.BlockSpec((tile, W), row), pl.BlockSpec((tile, LANES), row),
                  pl.BlockSpec((LANES, H * dk), fix), pl.BlockSpec((LANES, H * dk), fix),
                  pl.BlockSpec((1, H * dk), fix), pl.BlockSpec((1, dv), fix), pl.BlockSpec((C, C), fix)],
        out_specs=pl.BlockSpec((tile, H * dv), row),
        scratch_shapes=[pltpu.VMEM((H, dv, dk), F32)],
        compiler_params=_cparams(("parallel", "arbitrary")), name="gla_core",
    )(qkvr, a, wah, wal, b_a.reshape(1, H * dk), norm_g.reshape(1, dv), tri)


def gla_mixer(xb, w_in, w_a2, b_a, norm_g, B, S):
    H, dk, dv = GLA_HEADS, GLA_DK, GLA_DV
    hk, hv = H * dk, H * dv
    wq, wk, wv, wa, wr = jnp.split(w_in, [hk, 2 * hk, 2 * hk + hv, 2 * hk + hv + GLA_GATE_RANK], axis=1)
    w_main = jnp.concatenate([wq * dk ** -0.5, wk, wv, wr], axis=1).astype(BF16)
    w_gate = jnp.zeros((w_in.shape[0], LANES), F32).at[:, :GLA_GATE_RANK].set(wa).astype(BF16)
    qkvr = matmul(xb, w_main, out_dtype=BF16)
    a = matmul(xb, w_gate, out_dtype=F32)
    return gla_core(qkvr, a, w_a2, b_a, norm_g, B, S)


DIL_HEADS = D_MODEL // HEAD_DIM
DIL_CONFIGS = ((128, 1), (512, 4), (2048, 16))
DIL_TQ = 128


def _dil_attn_kernel(q_ref, kc_ref, kp_ref, vc_ref, vp_ref, o_ref, lse_ref, *, tiles_per_seq, window):
    i = pl.program_id(0)
    tq = q_ref.shape[0]
    first = (i % tiles_per_seq) == 0
    qpos = lax.broadcasted_iota(jnp.int32, (tq, 2 * tq), 0) + tq
    kpos = lax.broadcasted_iota(jnp.int32, (tq, 2 * tq), 1)
    dist = qpos - kpos
    mask = (dist >= 0) & (dist <= window) & ((kpos >= tq) | jnp.logical_not(first))
    lane = lax.broadcasted_iota(jnp.int32, (tq, LANES), 1)
    lo = lane < HEAD_DIM
    ct = (((1,), (1,)), ((), ()))
    for p in range(q_ref.shape[1] // LANES):
        cols = slice(p * LANES, (p + 1) * LANES)
        q2 = q_ref[:, cols]
        k2 = jnp.concatenate([kp_ref[:, cols], kc_ref[:, cols]], axis=0)
        v2 = jnp.concatenate([vp_ref[:, cols], vc_ref[:, cols]], axis=0)
        outs, lses = [], []
        for half in (lo, jnp.logical_not(lo)):
            qm = jnp.where(half, q2, jnp.zeros_like(q2))
            s = jnp.where(mask, lax.dot_general(qm, k2, ct, preferred_element_type=F32), NEG_INF)
            m = jnp.max(s, axis=-1, keepdims=True)
            e = jnp.where(mask, jnp.exp(s - m), 0.0)
            den = jnp.sum(e, axis=-1, keepdims=True)
            pv = jnp.dot(e.astype(BF16), v2, preferred_element_type=F32)
            outs.append(pv / jnp.maximum(den, 1e-30))
            lses.append(m + jnp.log(den))
        o_ref[:, cols] = jnp.where(lo, outs[0], outs[1]).astype(o_ref.dtype)
        lse_ref[:, cols] = jnp.where(lo, lses[0], lses[1])


def dil_attention(qk, v, seq_len, window):
    N, W = v.shape
    tq = DIL_TQ
    assert window == tq and seq_len % tq == 0
    nwb = 1
    cur = lambda i: (i, 0)
    prev = lambda i: (jnp.maximum(i - 1, 0), 0)
    return pl.pallas_call(
        functools.partial(_dil_attn_kernel, tiles_per_seq=seq_len // tq, window=window),
        out_shape=(jax.ShapeDtypeStruct((N, W), BF16), jax.ShapeDtypeStruct((N, W), F32)),
        grid=(N // tq,),
        in_specs=[pl.BlockSpec((tq, W), cur),
                  pl.BlockSpec((tq, W), lambda i: (i, nwb)),
                  pl.BlockSpec((tq, W), lambda i: (jnp.maximum(i - 1, 0), nwb)),
                  pl.BlockSpec((tq, W), cur), pl.BlockSpec((tq, W), prev)],
        out_specs=(pl.BlockSpec((tq, W), cur), pl.BlockSpec((tq, W), cur)),
        compiler_params=_cparams(("parallel",)), name="dil_attn",
    )(qk, qk, qk, v, v)


def _dil_merge_kernel(o0, o1, o2, l0, l1, l2, w_ref, x_ref, g_ref, b_ref, out_ref, outb_ref):
    la, lb, lc = l0[...], l1[...], l2[...]
    m = jnp.maximum(jnp.maximum(la, lb), lc)
    ea, eb, ec = jnp.exp(la - m), jnp.exp(lb - m), jnp.exp(lc - m)
    tot = ea + eb + ec
    o = (ea / tot) * o0[...].astype(F32) + (eb / tot) * o1[...].astype(F32) + (ec / tot) * o2[...].astype(F32)
    h = jnp.dot(o.astype(BF16), w_ref[...], preferred_element_type=F32)
    y = _ln(DN_ALPHA * x_ref[...] + h, g_ref[...], b_ref[...])
    out_ref[...] = y
    outb_ref[...] = y.astype(BF16)


def dil_merge_out(os_, lses, w_out, x, g, b, tm=256):
    N, D = x.shape
    W = w_out.shape[0]
    tm = _pick(N, tm)
    row = lambda i: (i, 0)
    fix = lambda i: (0, 0)
    rs = pl.BlockSpec((tm, W), row)
    return pl.pallas_call(
        _dil_merge_kernel,
        out_shape=(jax.ShapeDtypeStruct((N, D), F32), jax.ShapeDtypeStruct((N, D), BF16)),
        grid=(N // tm,),
        in_specs=[rs, rs, rs, rs, rs, rs, pl.BlockSpec((W, D), fix), pl.BlockSpec((tm, D), row),
                  pl.BlockSpec((1, D), fix), pl.BlockSpec((1, D), fix)],
        out_specs=(pl.BlockSpec((tm, D), row), pl.BlockSpec((tm, D), row)),
        compiler_params=_cparams(("parallel",)), name="dil_merge",
    )(*os_, *lses, w_out, x, g.reshape(1, D), b.reshape(1, D))


def dilated_layer(x, xb, w_in, w_out, ln_g, ln_b, B, S):
    N, D = x.shape
    H, dh = DIL_HEADS, HEAD_DIM
    W = H * dh
    w6 = w_in.reshape(D, len(DIL_CONFIGS), 3, W)
    pos = jnp.arange(S)
    os_, lses = [], []
    for gi, (window, dil) in enumerate(DIL_CONFIGS):
        L = S // dil
        wqk = jnp.concatenate([w6[:, gi, 0] * dh ** -0.5, w6[:, gi, 1]], axis=1).astype(BF16)
        wv = w6[:, gi, 2].astype(BF16)
        xp = xb.reshape(B, L, dil, D).transpose(0, 2, 1, 3).reshape(N, D) if dil > 1 else xb
        ppos = pos.reshape(L, dil).T.reshape(S)
        qk = matmul(xp, wqk, out_dtype=BF16, rope=rope_tables(ppos))
        v = matmul(xp, wv, out_dtype=BF16)
        o, lse = dil_attention(qk, v, L, window // dil)
        if dil > 1:
            o = o.reshape(B, dil, L, W).transpose(0, 2, 1, 3).reshape(N, W)
            lse = lse.reshape(B, dil, L, W).transpose(0, 2, 1, 3).reshape(N, W)
        os_.append(o)
        lses.append(lse)
    return dil_merge_out(os_, lses, w_out.astype(BF16), x, ln_g, ln_b)


NSA_HEADS = D_MODEL // HEAD_DIM
NSA_KV_HEADS = 4
NSA_GROUP = NSA_HEADS // NSA_KV_HEADS
CMP_LEN = 32
CMP_STRIDE = 16
CMP_HIDDEN = 256
SEL_LEN = 64
SEL_TOPN = 16
NSA_WINDOW = 512
FORCE_BONUS = 100.0
NSA_TQ = 128
NSA_TK = 128
NSA_CHUNK = 1024


def _compress_kernel(x_ref, p_ref, w1a_ref, w1b_ref, w2_ref, o_ref, *, transpose_out):
    x = x_ref[0].astype(F32)
    n = x.shape[0]
    first = jnp.dot((x + p_ref[0:1, :]).astype(BF16), w1a_ref[...], preferred_element_type=F32)
    second = jnp.dot((x + p_ref[1:2, :]).astype(BF16), w1b_ref[...], preferred_element_type=F32)
    hid = first + pltpu.roll(second, n - 1, 0)
    out = jnp.dot(jax.nn.gelu(hid).astype(BF16), w2_ref[...], preferred_element_type=F32)
    if transpose_out:
        o_ref[0] = out.T[:HEAD_DIM, :].astype(o_ref.dtype)
    else:
        o_ref[0] = out[:, :HEAD_DIM].astype(o_ref.dtype)


def nsa_compress(t, pos_emb, w1, w2, transpose_out):
    BK, n, W = t.shape
    half = CMP_STRIDE * HEAD_DIM
    p = pos_emb.reshape(2, half).astype(F32)
    w2p = jnp.zeros((CMP_HIDDEN, LANES), F32).at[:, :HEAD_DIM].set(w2).astype(BF16)
    oshape = (BK, HEAD_DIM, n) if transpose_out else (BK, n, HEAD_DIM)
    fix = lambda i: (0, 0)
    return pl.pallas_call(
        functools.partial(_compress_kernel, transpose_out=transpose_out),
        out_shape=jax.ShapeDtypeStruct(oshape, BF16), grid=(BK,),
        in_specs=[pl.BlockSpec((1, n, W), lambda i: (i, 0, 0)), pl.BlockSpec((2, half), fix),
                  pl.BlockSpec((half, CMP_HIDDEN), fix), pl.BlockSpec((half, CMP_HIDDEN), fix),
                  pl.BlockSpec((CMP_HIDDEN, LANES), fix)],
        out_specs=pl.BlockSpec((1,) + oshape[1:], lambda i: (i, 0, 0)),
        compiler_params=_cparams(("parallel",)), name="nsa_compress",
    )(t, p, w1[:half].astype(BF16), w1[half:].astype(BF16), w2p)


def _col_softmax_step(carry, s, valid, vt):
    m, l, acc = carry
    m_new = jnp.maximum(m, jnp.max(s, axis=0, keepdims=True))
    alpha = jnp.exp(m - m_new)
    p = jnp.exp(s - m_new)
    if valid is not None:
        p = jnp.where(valid, p, 0.0)
    l = alpha * l + jnp.sum(p, axis=0, keepdims=True)
    acc = alpha * acc + jnp.dot(vt, p.astype(BF16), preferred_element_type=F32)
    return m_new, l, acc


def _nsa_attn_t_kernel(q_ref, kc_ref, vct_ref, ks_ref, vst_ref, kw_ref, vwt_ref, oh_ref, gl_ref, ovl_ref, o_ref):
    kh = pl.program_id(1)
    i = pl.program_id(2)
    tq, dh, G = NSA_TQ, HEAD_DIM, NSA_GROUP
    tk = NSA_TK
    M = G * tq
    t0 = i * tq
    slot = kh % 2
    vrows = pl.ds(pl.multiple_of(slot * dh, dh), dh)

    def tpos(shape):
        return t0 + (lax.broadcasted_iota(jnp.int32, shape, 1) & (tq - 1))

    qn = q_ref[...].astype(F32)
    qt_pairs = [qn[:, c * LANES:(c + 1) * LANES].T for c in range(G * dh // LANES)]
    qt = jnp.concatenate([p[h * dh:(h + 1) * dh] for p in qt_pairs for h in range(LANES // dh)], axis=1)
    qt = qt.astype(BF16)

    n_cmp = kc_ref.shape[1]
    s_c = jnp.dot(kc_ref[0], qt, preferred_element_type=F32)
    cend = lax.broadcasted_iota(jnp.int32, (n_cmp, M), 0) * CMP_STRIDE + (CMP_LEN - 1)
    vis = cend <= tpos((n_cmp, M))
    s_c = jnp.where(vis, s_c, NEG_INF)
    e_c = jnp.where(vis, jnp.exp(s_c - jnp.max(s_c, axis=0, keepdims=True)), 0.0)
    p_c = e_c / jnp.maximum(jnp.sum(e_c, axis=0, keepdims=True), 1e-30)
    o_c = jnp.dot(vct_ref[0], p_c.astype(BF16), preferred_element_type=F32)

    psum = p_c[:, 0:tq]
    for g in range(1, G):
        psum = psum + p_c[:, g * tq:(g + 1) * tq]
    ovl = ovl_ref[...]
    imp = sum(jnp.dot(ovl, piece, preferred_element_type=F32) for piece in _split3(psum))
    n_sel = imp.shape[0]
    blk = lax.broadcasted_iota(jnp.int32, (n_sel, tq), 0)
    cur = tpos((n_sel, tq)) // SEL_LEN
    forced = (blk == 0) | (blk == cur) | (blk == cur - 1)
    score = jnp.where(blk <= cur, imp + jnp.where(forced, FORCE_BONUS, 0.0), -1.0)
    chosen = jnp.zeros((n_sel, tq), jnp.bool_)
    for _ in range(min(SEL_TOPN, n_sel)):
        mx, idx = _first_index_of_max(score, blk, 0, n_sel)
        hit = blk == idx
        chosen = chosen | (hit & (mx >= 0.0))
        score = jnp.where(hit, -2.0, score)
    bias = jnp.where(chosen, 0.0, NEG_INF).astype(BF16)
    zero = jnp.zeros_like(qt)
    q_pair = jnp.concatenate([jnp.where(slot == 0, qt, zero), jnp.where(slot == 1, qt, zero)], axis=0)
    pad = jnp.zeros((LANES - n_sel, M), BF16)
    q_aug = jnp.concatenate([q_pair, jnp.concatenate([bias] * G, axis=1), pad], axis=0)

    init = (jnp.full((1, M), NEG_INF, F32), jnp.zeros((1, M), F32), jnp.zeros((dh, M), F32))

    def vt_cat(ref, first_tile, n):
        return jnp.concatenate([ref[0, 0, first_tile + j, vrows, :] for j in range(n)], axis=1)

    ch = NSA_CHUNK
    per_chunk = ch // tk

    def sel_chunk(c, carry, diagonal):
        rows = pl.ds(pl.multiple_of(c * ch, ch), ch)
        k_aug = jnp.concatenate([ks_ref[rows, :], oh_ref[rows, :]], axis=1)
        s = jnp.dot(k_aug, q_aug, preferred_element_type=F32)
        if diagonal:
            kpos = c * ch + lax.broadcasted_iota(jnp.int32, (ch, M), 0)
            s = jnp.where(kpos <= tpos((ch, M)), s, NEG_INF)
        return _col_softmax_step(carry, s, None, vt_cat(vst_ref, c * per_chunk, per_chunk))

    n_full = t0 // ch
    carry = lax.fori_loop(0, n_full, lambda c, cr: sel_chunk(c, cr, False), init)
    _, l_s, acc_s = sel_chunk(n_full, carry, True)
    o_s = acc_s / jnp.maximum(l_s, 1e-30)

    n_wt = NSA_WINDOW // tk + 1
    wt0 = jnp.maximum(i + 1 - n_wt, 0)
    wrows = pl.ds(pl.multiple_of(wt0 * tk, tk), n_wt * tk)
    s_w = jnp.dot(kw_ref[wrows, :], q_pair, preferred_element_type=F32)
    dist = tpos(s_w.shape) - (wt0 * tk + lax.broadcasted_iota(jnp.int32, s_w.shape, 0))
    near = (dist >= 0) & (dist < NSA_WINDOW)
    s_w = jnp.where(near, s_w, NEG_INF)
    e_w = jnp.exp(s_w - jnp.max(s_w, axis=0, keepdims=True))
    l_w = jnp.sum(e_w, axis=0, keepdims=True)
    o_w = jnp.dot(vt_cat(vwt_ref, wt0, n_wt), e_w.astype(BF16), preferred_element_type=F32) / jnp.maximum(l_w, 1e-30)

    gates = jax.nn.sigmoid(gl_ref[...].T)
    outs = []
    for g in range(G):
        cols = slice(g * tq, (g + 1) * tq)
        outs.append(gates[g:g + 1] * o_c[:, cols] + gates[G + g:G + g + 1] * o_s[:, cols]
                    + gates[2 * G + g:2 * G + g + 1] * o_w[:, cols])
    per = LANES // dh
    o_ref[...] = jnp.concatenate(
        [jnp.concatenate(outs[c * per:(c + 1) * per], axis=0).T for c in range(G // per)], axis=1).astype(o_ref.dtype)


def nsa_attention_t(roped, vst, vwt, k_cmp, v_cmpt, gl, B, S):
    H, KH, G, dh = NSA_HEADS, NSA_KV_HEADS, NSA_GROUP, HEAD_DIM
    tq, tk = NSA_TQ, NSA_TK
    assert tq == tk and S % NSA_CHUNK == 0 and NSA_CHUNK % tk == 0 and S >= NSA_WINDOW + tq and G * dh == 2 * LANES
    nt = S // tq
    n_cmp = k_cmp.shape[1]
    n_sel = S // SEL_LEN
    c0 = np.arange(n_cmp)[None, :] * CMP_STRIDE
    s0 = np.arange(n_sel)[:, None] * SEL_LEN
    ovl = jnp.asarray((c0 < s0 + SEL_LEN) & (c0 + CMP_LEN - 1 >= s0), BF16)
    onehot = jnp.asarray(np.arange(S)[:, None] // SEL_LEN == np.arange(LANES)[None, :], BF16)
    ks_col = (H * dh + KH * dh) // LANES
    kw_col = (H * dh + 2 * KH * dh) // LANES
    qspec = pl.BlockSpec((tq, G * dh), lambda b, h, i: (b * nt + i, h))
    vspec = pl.BlockSpec((1, 1, nt, LANES, tk), lambda b, h, i: (b, h // 2, 0, 0, 0))
    return pl.pallas_call(
        _nsa_attn_t_kernel, out_shape=jax.ShapeDtypeStruct((B * S, H * dh), BF16), grid=(B, KH, nt),
        in_specs=[qspec,
                  pl.BlockSpec((1, n_cmp, dh), lambda b, h, i: (b * KH + h, 0, 0)),
                  pl.BlockSpec((1, dh, n_cmp), lambda b, h, i: (b * KH + h, 0, 0)),
                  pl.BlockSpec((S, LANES), lambda b, h, i: (b, ks_col + h // 2)), vspec,
                  pl.BlockSpec((S, LANES), lambda b, h, i: (b, kw_col + h // 2)), vspec,
                  pl.BlockSpec((S, LANES), lambda b, h, i: (0, 0)),
                  pl.BlockSpec((tq, LANES), lambda b, h, i: (b * nt + i, h)),
                  pl.BlockSpec((n_sel, n_cmp), lambda b, h, i: (0, 0))],
        out_specs=qspec,
        compiler_params=_cparams(("parallel", "parallel", "arbitrary")), name="nsa_attn",
    )(roped, k_cmp, v_cmpt, roped, vst, roped, vwt, onehot, gl, ovl)


def nsa_mixer(xb, w_in, ck_pos, ck_w1, ck_w2, cv_pos, cv_w1, cv_w2, B, S):
    H, KH, G, dh = NSA_HEADS, NSA_KV_HEADS, NSA_GROUP, HEAD_DIM
    kvw = KH * dh
    cuts = np.cumsum([H * dh, kvw, kvw, kvw, kvw, kvw, kvw]).tolist()
    wq, wkc, wvc, wks, wvs, wkw, wvw, wgl = jnp.split(w_in, cuts, axis=1)
    w_rope = jnp.concatenate([wq * dh ** -0.5, wkc, wks, wkw], axis=1).astype(BF16)
    w_val = jnp.concatenate([wvc, wvs, wvw], axis=1).astype(BF16)
    w_gate = jnp.zeros((w_in.shape[0], KH, LANES), F32).at[:, :, :3 * G].set(
        wgl.reshape(-1, KH, G, 3).transpose(0, 1, 3, 2).reshape(-1, KH, 3 * G)).reshape(-1, KH * LANES).astype(BF16)
    roped = matmul(xb, w_rope, out_dtype=BF16, rope=rope_tables(jnp.arange(S)), tn=w_rope.shape[1] // 2)
    vals = matmul(xb, w_val, out_dtype=BF16)
    gl = matmul(xb, w_gate, out_dtype=F32)

    n16 = S // CMP_STRIDE

    def blocks16(t):
        return t.reshape(B, n16, CMP_STRIDE, KH, dh).transpose(0, 3, 1, 2, 4).reshape(B * KH, n16, CMP_STRIDE * dh)

    def vt_tiles(t):
        return t.reshape(B, S // NSA_TK, NSA_TK, KH * dh // LANES, LANES).transpose(0, 3, 1, 4, 2)

    k_cmp = nsa_compress(blocks16(roped[:, H * dh:H * dh + kvw]), ck_pos, ck_w1, ck_w2, False)
    v_cmpt = nsa_compress(blocks16(vals[:, :kvw]), cv_pos, cv_w1, cv_w2, True)
    return nsa_attention_t(roped, vt_tiles(vals[:, kvw:2 * kvw]), vt_tiles(vals[:, 2 * kvw:]), k_cmp, v_cmpt, gl, B, S)


def kernel(x, l0_nsa_w_in, l0_nsa_w_out, l0_nsa_ck_pos, l0_nsa_ck_w1, l0_nsa_ck_w2, l0_nsa_cv_pos, l0_nsa_cv_w1, l0_nsa_cv_w2, l0_ln1_g, l0_ln1_b, l0_router_w, l0_router_b, l0_moe_w_in, l0_moe_w_out, l0_shared_w_in, l0_shared_w_out, l0_ln2_g, l0_ln2_b, l1_gla_w_in, l1_gla_w_a2, l1_gla_b_a, l1_gla_norm_g, l1_gla_w_out, l1_ln1_g, l1_ln1_b, l1_router_w, l1_router_b, l1_moe_w_in, l1_moe_w_out, l1_shared_w_in, l1_shared_w_out, l1_ln2_g, l1_ln2_b, l2_dil_w_in, l2_dil_w_out, l2_ln1_g, l2_ln1_b, l2_router_w, l2_router_b, l2_moe_w_in, l2_moe_w_out, l2_shared_w_in, l2_shared_w_out, l2_ln2_g, l2_ln2_b, l3_nsa_w_in, l3_nsa_w_out, l3_nsa_ck_pos, l3_nsa_ck_w1, l3_nsa_ck_w2, l3_nsa_cv_pos, l3_nsa_cv_w1, l3_nsa_cv_w2, l3_ln1_g, l3_ln1_b, l3_router_w, l3_router_b, l3_moe_w_in, l3_moe_w_out, l3_shared_w_in, l3_shared_w_out, l3_ln2_g, l3_ln2_b):
    B, S, D = x.shape
    xf = x.reshape(B * S, D)
    xb = xf.astype(BF16)

    h = nsa_mixer(xb, l0_nsa_w_in, l0_nsa_ck_pos, l0_nsa_ck_w1, l0_nsa_ck_w2, l0_nsa_cv_pos, l0_nsa_cv_w1, l0_nsa_cv_w2, B, S)
    xf, xb = matmul_res_ln(h, l0_nsa_w_out.astype(BF16), xf, l0_ln1_g, l0_ln1_b)
    xf, xb = moe_layer(xf, xb, l0_router_w, l0_router_b, l0_moe_w_in, l0_moe_w_out, l0_shared_w_in, l0_shared_w_out, l0_ln2_g, l0_ln2_b)

    h = gla_mixer(xb, l1_gla_w_in, l1_gla_w_a2, l1_gla_b_a, l1_gla_norm_g, B, S)
    xf, xb = matmul_res_ln(h, l1_gla_w_out.astype(BF16), xf, l1_ln1_g, l1_ln1_b)
    xf, xb = moe_layer(xf, xb, l1_router_w, l1_router_b, l1_moe_w_in, l1_moe_w_out, l1_shared_w_in, l1_shared_w_out, l1_ln2_g, l1_ln2_b)

    xf, xb = dilated_layer(xf, xb, l2_dil_w_in, l2_dil_w_out, l2_ln1_g, l2_ln1_b, B, S)
    xf, xb = moe_layer(xf, xb, l2_router_w, l2_router_b, l2_moe_w_in, l2_moe_w_out, l2_shared_w_in, l2_shared_w_out, l2_ln2_g, l2_ln2_b)

    h = nsa_mixer(xb, l3_nsa_w_in, l3_nsa_ck_pos, l3_nsa_ck_w1, l3_nsa_ck_w2, l3_nsa_cv_pos, l3_nsa_cv_w1, l3_nsa_cv_w2, B, S)
    xf, xb = matmul_res_ln(h, l3_nsa_w_out.astype(BF16), xf, l3_ln1_g, l3_ln1_b)
    xf, xb = moe_layer(xf, xb, l3_router_w, l3_router_b, l3_moe_w_in, l3_moe_w_out, l3_shared_w_in, l3_shared_w_out, l3_ln2_g, l3_ln2_b)
    return xf.reshape(B, S, D)
```

```python
import functools
import math

import jax
import jax.numpy as jnp
import numpy as np
from jax import lax
from jax.experimental import pallas as pl
from jax.experimental.pallas import tpu as pltpu

F32 = jnp.float32
BF16 = jnp.bfloat16

D_MODEL = 1024
DEPTH = 4
HEAD_DIM = 64
ROPE_THETA = 500000.0
ROPE_DIM = HEAD_DIM // 4
ROPE_HALF = ROPE_DIM // 2

N_EXPERTS = 64
TOP_K = 8
N_GROUPS = 8
TOPK_GROUPS = 4
GROUP_SIZE = N_EXPERTS // N_GROUPS
D_EXPERT = 256
ROUTED_SCALE = 2.5

DN_ALPHA = (2.0 * DEPTH) ** 0.25
LN_EPS = 1e-5
NEG_INF = -1e30

LANES = 128
VMEM_LIMIT = 48 * 1024 * 1024


def _cparams(sem, **kw):
    return pltpu.CompilerParams(dimension_semantics=sem, vmem_limit_bytes=VMEM_LIMIT, **kw)


def _pick(n, pref):
    t = min(pref, n)
    while n % t:
        t //= 2
    return t


def _mm_kernel(x_ref, w_ref, o_ref):
    o_ref[...] = jnp.dot(x_ref[...], w_ref[...], preferred_element_type=F32).astype(o_ref.dtype)


def _mm_rope_kernel(x_ref, w_ref, c_ref, sm_ref, sp_ref, o_ref):
    y = jnp.dot(x_ref[...], w_ref[...], preferred_element_type=F32)
    reps = y.shape[1] // LANES
    c = jnp.tile(c_ref[...], (1, reps))
    sm = jnp.tile(sm_ref[...], (1, reps))
    sp = jnp.tile(sp_ref[...], (1, reps))
    up = pltpu.roll(y, y.shape[1] - ROPE_HALF, 1)
    dn = pltpu.roll(y, ROPE_HALF, 1)
    o_ref[...] = (y * c + up * sm + dn * sp).astype(o_ref.dtype)


def matmul(x, w, out_dtype=F32, rope=None, tm=None, tn=1024):
    M, K = x.shape
    N = w.shape[1]
    tm = _pick(M, tm or (1024 if rope is None else 512))
    tn = _pick(N, tn)
    grid = (N // tn, M // tm)
    x_spec = pl.BlockSpec((tm, K), lambda j, i: (i, 0))
    w_spec = pl.BlockSpec((K, tn), lambda j, i: (0, j))
    o_spec = pl.BlockSpec((tm, tn), lambda j, i: (i, j))
    if rope is None:
        return pl.pallas_call(
            _mm_kernel, out_shape=jax.ShapeDtypeStruct((M, N), out_dtype), grid=grid,
            in_specs=[x_spec, w_spec], out_specs=o_spec,
            compiler_params=_cparams(("parallel", "parallel")), name="mm")(x, w)
    R = rope[0].shape[0]
    tm = _pick(R, tm)
    grid = (N // tn, M // tm)
    x_spec = pl.BlockSpec((tm, K), lambda j, i: (i, 0))
    o_spec = pl.BlockSpec((tm, tn), lambda j, i: (i, j))
    nr = R // tm
    t_spec = pl.BlockSpec((tm, LANES), lambda j, i: (i % nr, 0))
    return pl.pallas_call(
        _mm_rope_kernel, out_shape=jax.ShapeDtypeStruct((M, N), out_dtype), grid=grid,
        in_specs=[x_spec, w_spec, t_spec, t_spec, t_spec], out_specs=o_spec,
        compiler_params=_cparams(("parallel", "parallel")), name="mm_rope")(x, w, *rope)


def rope_tables(pos):
    inv = ROPE_THETA ** (-jnp.arange(ROPE_HALF, dtype=F32) * 2.0 / ROPE_DIM)
    ang = pos.astype(F32)[:, None] * inv[None, :]
    cos, sin = jnp.cos(ang), jnp.sin(ang)
    n = pos.shape[0]
    ones = jnp.ones((n, HEAD_DIM - ROPE_DIM), F32)
    zeros = jnp.zeros((n, HEAD_DIM - ROPE_DIM), F32)
    zh = jnp.zeros((n, ROPE_HALF), F32)
    c = jnp.concatenate([cos, cos, ones], 1)
    sm = jnp.concatenate([-sin, zh, zeros], 1)
    sp = jnp.concatenate([zh, sin, zeros], 1)
    return tuple(jnp.tile(t, (1, LANES // HEAD_DIM)) for t in (c, sm, sp))


def _ln(v, g, b):
    mu = jnp.mean(v, axis=-1, keepdims=True)
    d = v - mu
    var = jnp.mean(d * d, axis=-1, keepdims=True)
    return d * lax.rsqrt(var + LN_EPS) * g + b


def _mm_res_ln_kernel(a_ref, w_ref, x_ref, g_ref, b_ref, o_ref, ob_ref):
    h = jnp.dot(a_ref[...], w_ref[...], preferred_element_type=F32)
    y = _ln(DN_ALPHA * x_ref[...] + h, g_ref[...], b_ref[...])
    o_ref[...] = y
    ob_ref[...] = y.astype(BF16)


def matmul_res_ln(a, w, x, g, b, tm=512):
    M, K = a.shape
    D = w.shape[1]
    tm = _pick(M, tm)
    row = lambda i: (i, 0)
    fix = lambda i: (0, 0)
    return pl.pallas_call(
        _mm_res_ln_kernel,
        out_shape=(jax.ShapeDtypeStruct((M, D), F32), jax.ShapeDtypeStruct((M, D), BF16)),
        grid=(M // tm,),
        in_specs=[pl.BlockSpec((tm, K), row), pl.BlockSpec((K, D), fix), pl.BlockSpec((tm, D), row),
                  pl.BlockSpec((1, D), fix), pl.BlockSpec((1, D), fix)],
        out_specs=(pl.BlockSpec((tm, D), row), pl.BlockSpec((tm, D), row)),
        compiler_params=_cparams(("parallel",)), name="mm_res_ln")(a, w, x, g.reshape(1, D), b.reshape(1, D))


def _first_index_of_max(v, iota, axis, n):
    m = jnp.max(v, axis=axis, keepdims=True)
    idx = jnp.min(jnp.where(v == m, iota, n), axis=axis, keepdims=True)
    return m, idx


def _router_kernel(x_ref, wh_ref, wl_ref, rb_ref, tri_ref, slot_ref, gatew_ref, cnt_ref):
    x = x_ref[...]
    xh = x.astype(BF16)
    xl = (x - xh.astype(F32)).astype(BF16)
    dn = (((1,), (1,)), ((), ()))
    logits = (lax.dot_general(wh_ref[...], xh, dn, preferred_element_type=F32)
              + lax.dot_general(wh_ref[...], xl, dn, preferred_element_type=F32)
              + lax.dot_general(wl_ref[...], xh, dn, preferred_element_type=F32))
    tm = logits.shape[1]
    s = jax.nn.sigmoid(logits)
    sb = s + rb_ref[...]
    sb3 = sb.reshape(N_GROUPS, GROUP_SIZE, tm)
    io3 = lax.broadcasted_iota(jnp.int32, sb3.shape, 1)
    m1, i1 = _first_index_of_max(sb3, io3, 1, GROUP_SIZE)
    m2 = jnp.max(jnp.where(io3 == i1, -jnp.inf, sb3), axis=1, keepdims=True)
    gs = (m1 + m2).reshape(N_GROUPS, tm)
    iog = lax.broadcasted_iota(jnp.int32, gs.shape, 0)
    gmask = jnp.zeros(gs.shape, jnp.bool_)
    for _ in range(TOPK_GROUPS):
        _, gi = _first_index_of_max(gs, iog, 0, N_GROUPS)
        pick = iog == gi
        gmask = gmask | pick
        gs = jnp.where(pick, -jnp.inf, gs)
    emask = jnp.broadcast_to(gmask.reshape(N_GROUPS, 1, tm), sb3.shape).reshape(N_EXPERTS, tm)
    cand = jnp.where(emask, sb, NEG_INF)
    ioe = lax.broadcasted_iota(jnp.int32, cand.shape, 0)
    sel = jnp.zeros(cand.shape, jnp.bool_)
    picked = []
    for _ in range(TOP_K):
        _, ei = _first_index_of_max(cand, ioe, 0, N_EXPERTS)
        pick = ioe == ei
        picked.append(jnp.sum(jnp.where(pick, s, 0.0), axis=0, keepdims=True))
        sel = sel | pick
        cand = jnp.where(pick, -jnp.inf, cand)
    total = picked[0]
    for g in picked[1:]:
        total = total + g
    gatew_ref[...] = jnp.where(sel, s / total * ROUTED_SCALE, 0.0)
    routed = jnp.where(sel, 1.0, 0.0)
    before = jnp.dot(routed.astype(BF16), tri_ref[...], preferred_element_type=F32)
    slot_ref[...] = jnp.where(sel, before, -1.0).astype(jnp.int32)
    cnt_ref[0] = jnp.broadcast_to(jnp.sum(routed, axis=1, keepdims=True), cnt_ref.shape[1:]).astype(jnp.int32)


def moe_router(x, router_w, router_b):
    N, D = x.shape
    tm = MOE_TILE
    assert N % tm == 0
    wt = router_w.T
    wh = wt.astype(BF16)
    wl = (wt - wh.astype(F32)).astype(BF16)
    tri = (jnp.arange(tm)[:, None] < jnp.arange(tm)[None, :]).astype(BF16)
    fix = lambda i: (0, 0)
    col = lambda i: (0, i)
    slot, gatew, cnt = pl.pallas_call(
        _router_kernel,
        out_shape=(jax.ShapeDtypeStruct((N_EXPERTS, N), jnp.int32), jax.ShapeDtypeStruct((N_EXPERTS, N), F32),
                   jax.ShapeDtypeStruct((N // tm, N_EXPERTS, LANES), jnp.int32)),
        grid=(N // tm,),
        in_specs=[pl.BlockSpec((tm, D), lambda i: (i, 0)), pl.BlockSpec((N_EXPERTS, D), fix),
                  pl.BlockSpec((N_EXPERTS, D), fix), pl.BlockSpec((N_EXPERTS, 1), fix), pl.BlockSpec((tm, tm), fix)],
        out_specs=(pl.BlockSpec((N_EXPERTS, tm), col), pl.BlockSpec((N_EXPERTS, tm), col),
                   pl.BlockSpec((1, N_EXPERTS, LANES), lambda i: (i, 0, 0))),
        compiler_params=_cparams(("arbitrary",)), name="moe_router",
    )(x, wh, wl, router_b.reshape(N_EXPERTS, 1).astype(F32), tri)
    return slot, gatew, cnt[:, :, 0]


MOE_TILE = 512
MOE_CAP = 96
MOE_EGROUP = 16
MOE_BLOCK = 512
MOE_ALIGN = 16


def _slot_matrix(slot_rows, first, weights=None):
    n_e, T = slot_rows.shape
    r = lax.broadcasted_iota(jnp.int32, (MOE_CAP, T), 0) + first
    pieces = []
    for e in range(n_e):
        hit = slot_rows[e:e + 1, :] == r
        w = 1.0 if weights is None else weights[e:e + 1, :]
        pieces.append(jnp.where(hit, w, 0.0).astype(BF16))
    return pieces[0] if n_e == 1 else jnp.concatenate(pieces, axis=0)


def _n_windows(n_rows):
    return (n_rows + MOE_CAP - 1) // MOE_CAP


def _dispatch_kernel(base_ref, cnt_ref, zrow_ref, slot_ref, xb_ref, xs_ref, obuf, ovbuf, zbuf, sems, ovsem, zsem):
    t = pl.program_id(0)
    C, EG, E = MOE_CAP, MOE_EGROUP, N_EXPERTS
    n_groups = E // EG
    tz = zbuf.shape[0]

    @pl.when(t == 0)
    def _():
        zbuf[...] = jnp.zeros_like(zbuf)

        def zcopy(j):
            return pltpu.make_async_copy(zbuf, xs_ref.at[pl.ds(pl.multiple_of(zrow_ref[j], tz), tz)], zsem)

        def zstart(j, c):
            @pl.when(zrow_ref[j] >= 0)
            def _():
                zcopy(j).start()
            return c

        def zwait(j, c):
            @pl.when(zrow_ref[j] >= 0)
            def _():
                zcopy(j).wait()
            return c

        lax.fori_loop(0, zrow_ref.shape[0], zstart, 0)
        lax.fori_loop(0, zrow_ref.shape[0], zwait, 0)

    xb = xb_ref[...]

    def window(e, extra):
        return xs_ref.at[pl.ds(pl.multiple_of(base_ref[t * E + e] + extra, MOE_ALIGN), C)]

    def copies(g):
        return [pltpu.make_async_copy(obuf.at[g % 2, pl.ds(e * C, C)], window(g * EG + e, 0), sems.at[g % 2])
                for e in range(EG)]

    for g in range(n_groups):
        if g >= 2:
            for cp in copies(g - 2):
                cp.wait()
        onehot = _slot_matrix(slot_ref[g * EG:(g + 1) * EG, :], 0)
        obuf[g % 2] = jnp.dot(onehot, xb, preferred_element_type=F32).astype(BF16)
        for cp in copies(g):
            cp.start()
    for g in range(max(n_groups - 2, 0), n_groups):
        for cp in copies(g):
            cp.wait()

    def overflow(e, c):
        def chunk(j, c2):
            onehot = _slot_matrix(slot_ref[pl.ds(e, 1), :], j * C)
            ovbuf[...] = jnp.dot(onehot, xb, preferred_element_type=F32).astype(BF16)
            cp = pltpu.make_async_copy(ovbuf, window(e, j * C), ovsem)
            cp.start()
            cp.wait()
            return c2

        return lax.fori_loop(1, _n_windows(cnt_ref[t * E + e]), chunk, c)

    lax.fori_loop(0, E, overflow, 0)


def moe_dispatch(xb, slot, base, cnt, zrow, P):
    N, D = xb.shape
    T, C, EG = MOE_TILE, MOE_CAP, MOE_EGROUP
    grid_spec = pltpu.PrefetchScalarGridSpec(
        num_scalar_prefetch=3, grid=(N // T,),
        in_specs=[pl.BlockSpec((N_EXPERTS, T), lambda i, *_: (0, i)),
                  pl.BlockSpec((T, D), lambda i, *_: (i, 0))],
        out_specs=pl.BlockSpec(memory_space=pl.ANY),
        scratch_shapes=[pltpu.VMEM((2, EG * C, D), BF16), pltpu.VMEM((C, D), BF16), pltpu.VMEM((MOE_BLOCK, D), BF16),
                        pltpu.SemaphoreType.DMA((2,)), pltpu.SemaphoreType.DMA(()), pltpu.SemaphoreType.DMA(())])
    return pl.pallas_call(
        _dispatch_kernel, out_shape=jax.ShapeDtypeStruct((P, D), BF16), grid_spec=grid_spec,
        compiler_params=_cparams(("arbitrary",), has_side_effects=True), name="moe_dispatch",
    )(base.reshape(-1), cnt.reshape(-1), zrow, slot, xb)


def _swiglu(x_bf16, w_in, w_out):
    h = jnp.dot(x_bf16, w_in, preferred_element_type=F32)
    f = h.shape[1] // 2
    a = jax.nn.silu(h[:, :f]) * h[:, f:]
    return jnp.dot(a.astype(BF16), w_out, preferred_element_type=F32)


def _expert_kernel(blk_e_ref, used_ref, xs_ref, wi_ref, wo_ref, y_ref):
    del blk_e_ref
    live = pl.program_id(0) < used_ref[0]

    @pl.when(live)
    def _():
        y_ref[...] = _swiglu(xs_ref[...], wi_ref[0], wo_ref[0]).astype(y_ref.dtype)

    @pl.when(jnp.logical_not(live))
    def _():
        y_ref[...] = jnp.zeros_like(y_ref)


def moe_experts(xs, blk_e, n_used, w_in_e, w_out_e):
    P, D = xs.shape
    tm = MOE_BLOCK
    F2 = w_in_e.shape[2]
    spare = P // tm - 1
    rows = lambda i, be, nu: (jnp.where(i < nu[0], i, spare), 0)
    grid_spec = pltpu.PrefetchScalarGridSpec(
        num_scalar_prefetch=2, grid=(P // tm,),
        in_specs=[pl.BlockSpec((tm, D), rows),
                  pl.BlockSpec((1, D, F2), lambda i, be, nu: (be[i], 0, 0)),
                  pl.BlockSpec((1, F2 // 2, D), lambda i, be, nu: (be[i], 0, 0))],
        out_specs=pl.BlockSpec((tm, D), rows))
    return pl.pallas_call(
        _expert_kernel, out_shape=jax.ShapeDtypeStruct((P, D), BF16), grid_spec=grid_spec,
        compiler_params=_cparams(("arbitrary",)), name="moe_experts")(blk_e, n_used, xs, w_in_e, w_out_e)


def _combine_kernel(base_ref, cnt_ref, slot_ref, gw_ref, x_ref, wi_ref, wo_ref, g_ref, b_ref, ys_ref, o_ref, ob_ref,
                    ybuf, ovbuf, acc_ref, sems, ovsem):
    t = pl.program_id(0)
    C, EG, E = MOE_CAP, MOE_EGROUP, N_EXPERTS
    n_groups = E // EG
    rows_in = (((0,), (0,)), ((), ()))

    def window(e, extra):
        return ys_ref.at[pl.ds(pl.multiple_of(base_ref[t * E + e] + extra, MOE_ALIGN), C)]

    def copies(g):
        return [pltpu.make_async_copy(window(g * EG + e, 0), ybuf.at[g % 2, pl.ds(e * C, C)], sems.at[g % 2])
                for e in range(EG)]

    for cp in copies(0):
        cp.start()
    x = x_ref[...]
    acc_ref[...] = _swiglu(x.astype(BF16), wi_ref[...], wo_ref[...])
    for g in range(n_groups):
        if g + 1 < n_groups:
            for cp in copies(g + 1):
                cp.start()
        for cp in copies(g):
            cp.wait()
        spread = _slot_matrix(slot_ref[g * EG:(g + 1) * EG, :], 0, gw_ref[g * EG:(g + 1) * EG, :])
        acc_ref[...] += lax.dot_general(spread, ybuf[g % 2], rows_in, preferred_element_type=F32)

    def overflow(e, c):
        def chunk(j, c2):
            cp = pltpu.make_async_copy(window(e, j * C), ovbuf, ovsem)
            cp.start()
            cp.wait()
            spread = _slot_matrix(slot_ref[pl.ds(e, 1), :], j * C, gw_ref[pl.ds(e, 1), :])
            acc_ref[...] += lax.dot_general(spread, ovbuf[...], rows_in, preferred_element_type=F32)
            return c2

        return lax.fori_loop(1, _n_windows(cnt_ref[t * E + e]), chunk, c)

    lax.fori_loop(0, E, overflow, 0)
    y = _ln(DN_ALPHA * x + acc_ref[...], g_ref[...], b_ref[...])
    o_ref[...] = y
    ob_ref[...] = y.astype(BF16)


def moe_combine(x, ys, slot, gatew, base, cnt, w_in_s, w_out_s, g, b):
    N, D = x.shape
    T, C, EG = MOE_TILE, MOE_CAP, MOE_EGROUP
    F2 = w_in_s.shape[1]
    row = lambda i, *_: (i, 0)
    col = lambda i, *_: (0, i)
    fix = lambda i, *_: (0, 0)
    grid_spec = pltpu.PrefetchScalarGridSpec(
        num_scalar_prefetch=2, grid=(N // T,),
        in_specs=[pl.BlockSpec((N_EXPERTS, T), col), pl.BlockSpec((N_EXPERTS, T), col), pl.BlockSpec((T, D), row),
                  pl.BlockSpec((D, F2), fix), pl.BlockSpec((F2 // 2, D), fix),
                  pl.BlockSpec((1, D), fix), pl.BlockSpec((1, D), fix),
                  pl.BlockSpec(memory_space=pl.ANY)],
        out_specs=(pl.BlockSpec((T, D), row), pl.BlockSpec((T, D), row)),
        scratch_shapes=[pltpu.VMEM((2, EG * C, D), BF16), pltpu.VMEM((C, D), BF16), pltpu.VMEM((T, D), F32),
                        pltpu.SemaphoreType.DMA((2,)), pltpu.SemaphoreType.DMA(())])
    return pl.pallas_call(
        _combine_kernel,
        out_shape=(jax.ShapeDtypeStruct((N, D), F32), jax.ShapeDtypeStruct((N, D), BF16)),
        grid_spec=grid_spec,
        compiler_params=_cparams(("arbitrary",)), name="moe_combine",
    )(base.reshape(-1), cnt.reshape(-1), slot, gatew, x, w_in_s, w_out_s, g.reshape(1, D), b.reshape(1, D), ys)


def moe_layer(x, xb, router_w, router_b, w_in_e, w_out_e, w_in_s, w_out_s, ln_g, ln_b):
    N, D = x.shape
    E, C, A, tm_e = N_EXPERTS, MOE_CAP, MOE_ALIGN, MOE_BLOCK
    n_t = N // MOE_TILE
    slot, gatew, cnt = moe_router(x, router_w, router_b)
    seg = (cnt + A - 1) // A * A
    padded = (jnp.sum(seg, axis=0) + C + tm_e - 1) // tm_e * tm_e
    pad_end = jnp.cumsum(padded)
    pad_start = pad_end - padded
    base = (pad_start[None, :] + jnp.cumsum(seg, axis=0) - seg).astype(jnp.int32)
    max_rows = N * TOP_K + n_t * E * (A - 1) + E * (C + tm_e - 1)
    n_blk = -(-max_rows // tm_e) + 1
    blk_row = jnp.arange(n_blk, dtype=jnp.int32) * tm_e
    blk_e = jnp.minimum(jnp.sum(pad_end[None, :] <= blk_row[:, None], axis=1), E - 1).astype(jnp.int32)
    n_used = (pad_end[-1:] // tm_e).astype(jnp.int32)
    zrow = jnp.concatenate([pad_end - tm_e, jnp.where(padded >= 2 * tm_e, pad_end - 2 * tm_e, -1)]).astype(jnp.int32)
    xs = moe_dispatch(xb, slot, base, cnt, zrow, n_blk * tm_e)
    ys = moe_experts(xs, blk_e, n_used, w_in_e.astype(BF16), w_out_e.astype(BF16))
    return moe_combine(x, ys, slot, gatew, base, cnt, w_in_s.astype(BF16), w_out_s.astype(BF16), ln_g, ln_b)


GLA_HEADS = 4
GLA_DK = D_MODEL // 2 // GLA_HEADS
GLA_DV = D_MODEL // GLA_HEADS
GLA_GATE_RANK = 16
GLA_TAU = 16.0
GLA_CHUNK = 64


def _split3(v):
    h1 = v.astype(BF16)
    r1 = v - h1.astype(F32)
    h2 = r1.astype(BF16)
    h3 = (r1 - h2.astype(F32)).astype(BF16)
    return h1, h2, h3


def _gla_kernel(qkvr_ref, a_ref, wah_ref, wal_ref, ba_ref, ng_ref, tri_ref, o_ref, st_ref):
    H, dk, dv, C = GLA_HEADS, GLA_DK, GLA_DV, GLA_CHUNK

    @pl.when(pl.program_id(1) == 0)
    def _():
        st_ref[...] = jnp.zeros_like(st_ref)

    T = qkvr_ref.shape[0]
    a = a_ref[...]
    ah = a.astype(BF16)
    al = (a - ah.astype(F32)).astype(BF16)
    glog = (jnp.dot(ah, wah_ref[...], preferred_element_type=F32)
            + jnp.dot(al, wah_ref[...], preferred_element_type=F32)
            + jnp.dot(ah, wal_ref[...], preferred_element_type=F32)) + ba_ref[...]
    log_a = jax.nn.log_sigmoid(glog) / GLA_TAU
    tri = tri_ref[...]
    rr = lax.broadcasted_iota(jnp.int32, (C, C), 0)
    cc = lax.broadcasted_iota(jnp.int32, (C, C), 1)
    causal = rr >= cc
    ng = ng_ref[...]
    ct = (((1,), (1,)), ((), ()))
    c0 = (((0,), (0,)), ((), ()))
    for c in range(T // C):
        rows = slice(c * C, (c + 1) * C)
        for h in range(H):
            la = log_a[rows, h * dk:(h + 1) * dk]
            p1, p2, p3 = _split3(la)
            b = (jnp.dot(tri, p1, preferred_element_type=F32) + jnp.dot(tri, p2, preferred_element_type=F32)
                 + jnp.dot(tri, p3, preferred_element_type=F32))
            b_last = b[C - 1:C, :]
            q = qkvr_ref[rows, h * dk:(h + 1) * dk].astype(F32)
            k = qkvr_ref[rows, H * dk + h * dk:H * dk + (h + 1) * dk].astype(F32)
            v = qkvr_ref[rows, 2 * H * dk + h * dv:2 * H * dk + (h + 1) * dv]
            r = qkvr_ref[rows, 2 * H * dk + H * dv + h * dv:2 * H * dk + H * dv + (h + 1) * dv].astype(F32)
            qg = (q * jnp.exp(b)).astype(BF16)
            kg = (k * jnp.exp(-b)).astype(BF16)
            kd = (k * jnp.exp(b_last - b)).astype(BF16)
            att = jnp.where(causal, lax.dot_general(qg, kg, ct, preferred_element_type=F32), 0.0)
            st = st_ref[h]
            o = (jnp.dot(att.astype(BF16), v, preferred_element_type=F32)
                 + lax.dot_general(qg, st.astype(BF16), ct, preferred_element_type=F32))
            st_ref[h] = jnp.exp(b_last) * st + lax.dot_general(v, kd, c0, preferred_element_type=F32)
            o = o * lax.rsqrt(jnp.mean(o * o, axis=-1, keepdims=True) + LN_EPS) * ng
            o_ref[rows, h * dv:(h + 1) * dv] = (o * jax.nn.silu(r)).astype(o_ref.dtype)


def gla_core(qkvr, a, w_a2, b_a, norm_g, B, S, tile=256):
    H, dk, dv, C = GLA_HEADS, GLA_DK, GLA_DV, GLA_CHUNK
    N, W = qkvr.shape
    tile = _pick(S, tile)
    nt = S // tile
    wa = jnp.zeros((LANES, H * dk), F32).at[:GLA_GATE_RANK].set(w_a2)
    wah = wa.astype(BF16)
    wal = (wa - wah.astype(F32)).astype(BF16)
    tri = (jnp.arange(C)[:, None] >= jnp.arange(C)[None, :]).astype(BF16)
    row = lambda b, t: (b * nt + t, 0)
    fix = lambda b, t: (0, 0)
    return pl.pallas_call(
        _gla_kernel, out_shape=jax.ShapeDtypeStruct((N, H * dv), BF16), grid=(B, nt),
        in_specs=[pl.BlockSpec((tile, W), row), pl.BlockSpec((tile, LANES), row),
                  pl.BlockSpec((LANES, H * dk), fix), pl.BlockSpec((LANES, H * dk), fix),
                  pl.BlockSpec((1, H * dk), fix), pl.BlockSpec((1, dv), fix), pl.BlockSpec((C, C), fix)],
        out_specs=pl.BlockSpec((tile, H * dv), row),
        scratch_shapes=[pltpu.VMEM((H, dv, dk), F32)],
        compiler_params=_cparams(("parallel", "arbitrary")), name="gla_core",
    )(qkvr, a, wah, wal, b_a.reshape(1, H * dk), norm_g.reshape(1, dv), tri)


def gla_mixer(xb, w_in, w_a2, b_a, norm_g, B, S):
    H, dk, dv = GLA_HEADS, GLA_DK, GLA_DV
    hk, hv = H * dk, H * dv
    wq, wk, wv, wa, wr = jnp.split(w_in, [hk, 2 * hk, 2 * hk + hv, 2 * hk + hv + GLA_GATE_RANK], axis=1)
    w_main = jnp.concatenate([wq * dk ** -0.5, wk, wv, wr], axis=1).astype(BF16)
    w_gate = jnp.zeros((w_in.shape[0], LANES), F32).at[:, :GLA_GATE_RANK].set(wa).astype(BF16)
    qkvr = matmul(xb, w_main, out_dtype=BF16)
    a = matmul(xb, w_gate, out_dtype=F32)
    return gla_core(qkvr, a, w_a2, b_a, norm_g, B, S)


DIL_HEADS = D_MODEL // HEAD_DIM
DIL_CONFIGS = ((128, 1), (512, 4), (2048, 16))
DIL_TQ = 128


def _dil_attn_kernel(q_ref, kc_ref, kp_ref, vc_ref, vp_ref, o_ref, lse_ref, *, tiles_per_seq, window):
    i = pl.program_id(0)
    tq = q_ref.shape[0]
    first = (i % tiles_per_seq) == 0
    qpos = lax.broadcasted_iota(jnp.int32, (tq, 2 * tq), 0) + tq
    kpos = lax.broadcasted_iota(jnp.int32, (tq, 2 * tq), 1)
    dist = qpos - kpos
    mask = (dist >= 0) & (dist <= window) & ((kpos >= tq) | jnp.logical_not(first))
    lane = lax.broadcasted_iota(jnp.int32, (tq, LANES), 1)
    lo = lane < HEAD_DIM
    ct = (((1,), (1,)), ((), ()))
    for p in range(q_ref.shape[1] // LANES):
        cols = slice(p * LANES, (p + 1) * LANES)
        q2 = q_ref[:, cols]
        k2 = jnp.concatenate([kp_ref[:, cols], kc_ref[:, cols]], axis=0)
        v2 = jnp.concatenate([vp_ref[:, cols], vc_ref[:, cols]], axis=0)
        outs, lses = [], []
        for half in (lo, jnp.logical_not(lo)):
            qm = jnp.where(half, q2, jnp.zeros_like(q2))
            s = jnp.where(mask, lax.dot_general(qm, k2, ct, preferred_element_type=F32), NEG_INF)
            m = jnp.max(s, axis=-1, keepdims=True)
            e = jnp.where(mask, jnp.exp(s - m), 0.0)
            den = jnp.sum(e, axis=-1, keepdims=True)
            pv = jnp.dot(e.astype(BF16), v2, preferred_element_type=F32)
            outs.append(pv / jnp.maximum(den, 1e-30))
            lses.append(m + jnp.log(den))
        o_ref[:, cols] = jnp.where(lo, outs[0], outs[1]).astype(o_ref.dtype)
        lse_ref[:, cols] = jnp.where(lo, lses[0], lses[1])


def dil_attention(qk, v, seq_len, window):
    N, W = v.shape
    tq = DIL_TQ
    assert window == tq and seq_len % tq == 0
    nwb = 1
    cur = lambda i: (i, 0)
    prev = lambda i: (jnp.maximum(i - 1, 0), 0)
    return pl.pallas_call(
        functools.partial(_dil_attn_kernel, tiles_per_seq=seq_len // tq, window=window),
        out_shape=(jax.ShapeDtypeStruct((N, W), BF16), jax.ShapeDtypeStruct((N, W), F32)),
        grid=(N // tq,),
        in_specs=[pl.BlockSpec((tq, W), cur),
                  pl.BlockSpec((tq, W), lambda i: (i, nwb)),
                  pl.BlockSpec((tq, W), lambda i: (jnp.maximum(i - 1, 0), nwb)),
                  pl.BlockSpec((tq, W), cur), pl.BlockSpec((tq, W), prev)],
        out_specs=(pl.BlockSpec((tq, W), cur), pl.BlockSpec((tq, W), cur)),
        compiler_params=_cparams(("parallel",)), name="dil_attn",
    )(qk, qk, qk, v, v)


def _dil_merge_kernel(o0, o1, o2, l0, l1, l2, w_ref, x_ref, g_ref, b_ref, out_ref, outb_ref):
    la, lb, lc = l0[...], l1[...], l2[...]
    m = jnp.maximum(jnp.maximum(la, lb), lc)
    ea, eb, ec = jnp.exp(la - m), jnp.exp(lb - m), jnp.exp(lc - m)
    tot = ea + eb + ec
    o = (ea / tot) * o0[...].astype(F32) + (eb / tot) * o1[...].astype(F32) + (ec / tot) * o2[...].astype(F32)
    h = jnp.dot(o.astype(BF16), w_ref[...], preferred_element_type=F32)
    y = _ln(DN_ALPHA * x_ref[...] + h, g_ref[...], b_ref[...])
    out_ref[...] = y
    outb_ref[...] = y.astype(BF16)


def dil_merge_out(os_, lses, w_out, x, g, b, tm=256):
    N, D = x.shape
    W = w_out.shape[0]
    tm = _pick(N, tm)
    row = lambda i: (i, 0)
    fix = lambda i: (0, 0)
    rs = pl.BlockSpec((tm, W), row)
    return pl.pallas_call(
        _dil_merge_kernel,
        out_shape=(jax.ShapeDtypeStruct((N, D), F32), jax.ShapeDtypeStruct((N, D), BF16)),
        grid=(N // tm,),
        in_specs=[rs, rs, rs, rs, rs, rs, pl.BlockSpec((W, D), fix), pl.BlockSpec((tm, D), row),
                  pl.BlockSpec((1, D), fix), pl.BlockSpec((1, D), fix)],
        out_specs=(pl.BlockSpec((tm, D), row), pl.BlockSpec((tm, D), row)),
        compiler_params=_cparams(("parallel",)), name="dil_merge",
    )(*os_, *lses, w_out, x, g.reshape(1, D), b.reshape(1, D))


def dilated_layer(x, xb, w_in, w_out, ln_g, ln_b, B, S):
    N, D = x.shape
    H, dh = DIL_HEADS, HEAD_DIM
    W = H * dh
    w6 = w_in.reshape(D, len(DIL_CONFIGS), 3, W)
    pos = jnp.arange(S)
    os_, lses = [], []
    for gi, (window, dil) in enumerate(DIL_CONFIGS):
        L = S // dil
        wqk = jnp.concatenate([w6[:, gi, 0] * dh ** -0.5, w6[:, gi, 1]], axis=1).astype(BF16)
        wv = w6[:, gi, 2].astype(BF16)
        xp = xb.reshape(B, L, dil, D).transpose(0, 2, 1, 3).reshape(N, D) if dil > 1 else xb
        ppos = pos.reshape(L, dil).T.reshape(S)
        qk = matmul(xp, wqk, out_dtype=BF16, rope=rope_tables(ppos))
        v = matmul(xp, wv, out_dtype=BF16)
        o, lse = dil_attention(qk, v, L, window // dil)
        if dil > 1:
            o = o.reshape(B, dil, L, W).transpose(0, 2, 1, 3).reshape(N, W)
            lse = lse.reshape(B, dil, L, W).transpose(0, 2, 1, 3).reshape(N, W)
        os_.append(o)
        lses.append(lse)
    return dil_merge_out(os_, lses, w_out.astype(BF16), x, ln_g, ln_b)


NSA_HEADS = D_MODEL // HEAD_DIM
NSA_KV_HEADS = 4
NSA_GROUP = NSA_HEADS // NSA_KV_HEADS
CMP_LEN = 32
CMP_STRIDE = 16
CMP_HIDDEN = 256
SEL_LEN = 64
SEL_TOPN = 16
NSA_WINDOW = 512
FORCE_BONUS = 100.0
NSA_TQ = 128
NSA_TK = 128
NSA_CHUNK = 1024


def _compress_kernel(x_ref, p_ref, w1a_ref, w1b_ref, w2_ref, o_ref, *, transpose_out):
    x = x_ref[0].astype(F32)
    n = x.shape[0]
    first = jnp.dot((x + p_ref[0:1, :]).astype(BF16), w1a_ref[...], preferred_element_type=F32)
    second = jnp.dot((x + p_ref[1:2, :]).astype(BF16), w1b_ref[...], preferred_element_type=F32)
    hid = first + pltpu.roll(second, n - 1, 0)
    out = jnp.dot(jax.nn.gelu(hid).astype(BF16), w2_ref[...], preferred_element_type=F32)
    if transpose_out:
        o_ref[0] = out.T[:HEAD_DIM, :].astype(o_ref.dtype)
    else:
        o_ref[0] = out[:, :HEAD_DIM].astype(o_ref.dtype)


def nsa_compress(t, pos_emb, w1, w2, transpose_out):
    BK, n, W = t.shape
    half = CMP_STRIDE * HEAD_DIM
    p = pos_emb.reshape(2, half).astype(F32)
    w2p = jnp.zeros((CMP_HIDDEN, LANES), F32).at[:, :HEAD_DIM].set(w2).astype(BF16)
    oshape = (BK, HEAD_DIM, n) if transpose_out else (BK, n, HEAD_DIM)
    fix = lambda i: (0, 0)
    return pl.pallas_call(
        functools.partial(_compress_kernel, transpose_out=transpose_out),
        out_shape=jax.ShapeDtypeStruct(oshape, BF16), grid=(BK,),
        in_specs=[pl.BlockSpec((1, n, W), lambda i: (i, 0, 0)), pl.BlockSpec((2, half), fix),
                  pl.BlockSpec((half, CMP_HIDDEN), fix), pl.BlockSpec((half, CMP_HIDDEN), fix),
                  pl.BlockSpec((CMP_HIDDEN, LANES), fix)],
        out_specs=pl.BlockSpec((1,) + oshape[1:], lambda i: (i, 0, 0)),
        compiler_params=_cparams(("parallel",)), name="nsa_compress",
    )(t, p, w1[:half].astype(BF16), w1[half:].astype(BF16), w2p)


def _col_softmax_step(carry, s, valid, vt):
    m, l, acc = carry
    m_new = jnp.maximum(m, jnp.max(s, axis=0, keepdims=True))
    alpha = jnp.exp(m - m_new)
    p = jnp.exp(s - m_new)
    if valid is not None:
        p = jnp.where(valid, p, 0.0)
    l = alpha * l + jnp.sum(p, axis=0, keepdims=True)
    acc = alpha * acc + jnp.dot(vt, p.astype(BF16), preferred_element_type=F32)
    return m_new, l, acc


def _nsa_attn_t_kernel(q_ref, kc_ref, vct_ref, ks_ref, vst_ref, kw_ref, vwt_ref, oh_ref, gl_ref, ovl_ref, o_ref):
    kh = pl.program_id(1)
    i = pl.program_id(2)
    tq, dh, G = NSA_TQ, HEAD_DIM, NSA_GROUP
    tk = NSA_TK
    M = G * tq
    t0 = i * tq
    slot = kh % 2
    vrows = pl.ds(pl.multiple_of(slot * dh, dh), dh)

    def tpos(shape):
        return t0 + (lax.broadcasted_iota(jnp.int32, shape, 1) & (tq - 1))

    qn = q_ref[...].astype(F32)
    qt_pairs = [qn[:, c * LANES:(c + 1) * LANES].T for c in range(G * dh // LANES)]
    qt = jnp.concatenate([p[h * dh:(h + 1) * dh] for p in qt_pairs for h in range(LANES // dh)], axis=1)
    qt = qt.astype(BF16)

    n_cmp = kc_ref.shape[1]
    s_c = jnp.dot(kc_ref[0], qt, preferred_element_type=F32)
    cend = lax.broadcasted_iota(jnp.int32, (n_cmp, M), 0) * CMP_STRIDE + (CMP_LEN - 1)
    vis = cend <= tpos((n_cmp, M))
    s_c = jnp.where(vis, s_c, NEG_INF)
    e_c = jnp.where(vis, jnp.exp(s_c - jnp.max(s_c, axis=0, keepdims=True)), 0.0)
    p_c = e_c / jnp.maximum(jnp.sum(e_c, axis=0, keepdims=True), 1e-30)
    o_c = jnp.dot(vct_ref[0], p_c.astype(BF16), preferred_element_type=F32)

    psum = p_c[:, 0:tq]
    for g in range(1, G):
        psum = psum + p_c[:, g * tq:(g + 1) * tq]
    ovl = ovl_ref[...]
    imp = sum(jnp.dot(ovl, piece, preferred_element_type=F32) for piece in _split3(psum))
    n_sel = imp.shape[0]
    blk = lax.broadcasted_iota(jnp.int32, (n_sel, tq), 0)
    cur = tpos((n_sel, tq)) // SEL_LEN
    forced = (blk == 0) | (blk == cur) | (blk == cur - 1)
    score = jnp.where(blk <= cur, imp + jnp.where(forced, FORCE_BONUS, 0.0), -1.0)
    chosen = jnp.zeros((n_sel, tq), jnp.bool_)
    for _ in range(min(SEL_TOPN, n_sel)):
        mx, idx = _first_index_of_max(score, blk, 0, n_sel)
        hit = blk == idx
        chosen = chosen | (hit & (mx >= 0.0))
        score = jnp.where(hit, -2.0, score)
    bias = jnp.where(chosen, 0.0, NEG_INF).astype(BF16)
    zero = jnp.zeros_like(qt)
    q_pair = jnp.concatenate([jnp.where(slot == 0, qt, zero), jnp.where(slot == 1, qt, zero)], axis=0)
    pad = jnp.zeros((LANES - n_sel, M), BF16)
    q_aug = jnp.concatenate([q_pair, jnp.concatenate([bias] * G, axis=1), pad], axis=0)

    init = (jnp.full((1, M), NEG_INF, F32), jnp.zeros((1, M), F32), jnp.zeros((dh, M), F32))

    def vt_cat(ref, first_tile, n):
        return jnp.concatenate([ref[0, 0, first_tile + j, vrows, :] for j in range(n)], axis=1)

    ch = NSA_CHUNK
    per_chunk = ch // tk

    def sel_rows(start, n, carry, causal):
        rows = slice(start, start + n)
        k_aug = jnp.concatenate([ks_ref[rows, :], oh_ref[rows, :]], axis=1)
        s = jnp.dot(k_aug, q_aug, preferred_element_type=F32)
        if causal:
            kpos = start + lax.broadcasted_iota(jnp.int32, (n, M), 0)
            s = jnp.where(kpos <= tpos((n, M)), s, NEG_INF)
        return _col_softmax_step(carry, s, None, vt_cat(vst_ref, start // tk, n // tk))

    def sel_upto(n_full):
        def run():
            carry = init
            for c in range(n_full):
                carry = sel_rows(c * ch, ch, carry, False)
            _, l, acc = sel_rows(n_full * ch, ch, carry, True)
            return l, acc
        return run

    l_s, acc_s = lax.switch(t0 // ch, [sel_upto(n) for n in range(ks_ref.shape[0] // ch)])
    o_s = acc_s / jnp.maximum(l_s, 1e-30)

    n_wt = NSA_WINDOW // tk + 1
    wt0 = jnp.maximum(i + 1 - n_wt, 0)
    wrows = pl.ds(pl.multiple_of(wt0 * tk, tk), n_wt * tk)
    s_w = jnp.dot(kw_ref[wrows, :], q_pair, preferred_element_type=F32)
    dist = tpos(s_w.shape) - (wt0 * tk + lax.broadcasted_iota(jnp.int32, s_w.shape, 0))
    near = (dist >= 0) & (dist < NSA_WINDOW)
    s_w = jnp.where(near, s_w, NEG_INF)
    e_w = jnp.exp(s_w - jnp.max(s_w, axis=0, keepdims=True))
    l_w = jnp.sum(e_w, axis=0, keepdims=True)
    o_w = jnp.dot(vt_cat(vwt_ref, wt0, n_wt), e_w.astype(BF16), preferred_element_type=F32) / jnp.maximum(l_w, 1e-30)

    gates = jax.nn.sigmoid(gl_ref[...].T)
    outs = []
    for g in range(G):
        cols = slice(g * tq, (g + 1) * tq)
        outs.append(gates[g:g + 1] * o_c[:, cols] + gates[G + g:G + g + 1] * o_s[:, cols]
                    + gates[2 * G + g:2 * G + g + 1] * o_w[:, cols])
    per = LANES // dh
    o_ref[...] = jnp.concatenate(
        [jnp.concatenate(outs[c * per:(c + 1) * per], axis=0).T for c in range(G // per)], axis=1).astype(o_ref.dtype)


def nsa_attention_t(roped, vst, vwt, k_cmp, v_cmpt, gl, B, S):
    H, KH, G, dh = NSA_HEADS, NSA_KV_HEADS, NSA_GROUP, HEAD_DIM
    tq, tk = NSA_TQ, NSA_TK
    assert tq == tk and S % NSA_CHUNK == 0 and NSA_CHUNK % tk == 0 and S >= NSA_WINDOW + tq and G * dh == 2 * LANES
    nt = S // tq
    n_cmp = k_cmp.shape[1]
    n_sel = S // SEL_LEN
    c0 = np.arange(n_cmp)[None, :] * CMP_STRIDE
    s0 = np.arange(n_sel)[:, None] * SEL_LEN
    ovl = jnp.asarray((c0 < s0 + SEL_LEN) & (c0 + CMP_LEN - 1 >= s0), BF16)
    onehot = jnp.asarray(np.arange(S)[:, None] // SEL_LEN == np.arange(LANES)[None, :], BF16)
    ks_col = (H * dh + KH * dh) // LANES
    kw_col = (H * dh + 2 * KH * dh) // LANES
    qspec = pl.BlockSpec((tq, G * dh), lambda b, h, i: (b * nt + i, h))
    vspec = pl.BlockSpec((1, 1, nt, LANES, tk), lambda b, h, i: (b, h // 2, 0, 0, 0))
    return pl.pallas_call(
        _nsa_attn_t_kernel, out_shape=jax.ShapeDtypeStruct((B * S, H * dh), BF16), grid=(B, KH, nt),
        in_specs=[qspec,
                  pl.BlockSpec((1, n_cmp, dh), lambda b, h, i: (b * KH + h, 0, 0)),
                  pl.BlockSpec((1, dh, n_cmp), lambda b, h, i: (b * KH + h, 0, 0)),
                  pl.BlockSpec((S, LANES), lambda b, h, i: (b, ks_col + h // 2)), vspec,
                  pl.BlockSpec((S, LANES), lambda b, h, i: (b, kw_col + h // 2)), vspec,
                  pl.BlockSpec((S, LANES), lambda b, h, i: (0, 0)),
                  pl.BlockSpec((tq, LANES), lambda b, h, i: (b * nt + i, h)),
                  pl.BlockSpec((n_sel, n_cmp), lambda b, h, i: (0, 0))],
        out_specs=qspec,
        compiler_params=_cparams(("parallel", "parallel", "arbitrary")), name="nsa_attn",
    )(roped, k_cmp, v_cmpt, roped, vst, roped, vwt, onehot, gl, ovl)


def nsa_mixer(xb, w_in, ck_pos, ck_w1, ck_w2, cv_pos, cv_w1, cv_w2, B, S):
    H, KH, G, dh = NSA_HEADS, NSA_KV_HEADS, NSA_GROUP, HEAD_DIM
    kvw = KH * dh
    cuts = np.cumsum([H * dh, kvw, kvw, kvw, kvw, kvw, kvw]).tolist()
    wq, wkc, wvc, wks, wvs, wkw, wvw, wgl = jnp.split(w_in, cuts, axis=1)
    w_rope = jnp.concatenate([wq * dh ** -0.5, wkc, wks, wkw], axis=1).astype(BF16)
    w_val = jnp.concatenate([wvc, wvs, wvw], axis=1).astype(BF16)
    w_gate = jnp.zeros((w_in.shape[0], KH, LANES), F32).at[:, :, :3 * G].set(
        wgl.reshape(-1, KH, G, 3).transpose(0, 1, 3, 2).reshape(-1, KH, 3 * G)).reshape(-1, KH * LANES).astype(BF16)
    roped = matmul(xb, w_rope, out_dtype=BF16, rope=rope_tables(jnp.arange(S)), tn=w_rope.shape[1] // 2)
    vals = matmul(xb, w_val, out_dtype=BF16)
    gl = matmul(xb, w_gate, out_dtype=F32)

    n16 = S // CMP_STRIDE

    def blocks16(t):
        return t.reshape(B, n16, CMP_STRIDE, KH, dh).transpose(0, 3, 1, 2, 4).reshape(B * KH, n16, CMP_STRIDE * dh)

    def vt_tiles(t):
        return t.reshape(B, S // NSA_TK, NSA_TK, KH * dh // LANES, LANES).transpose(0, 3, 1, 4, 2)

    k_cmp = nsa_compress(blocks16(roped[:, H * dh:H * dh + kvw]), ck_pos, ck_w1, ck_w2, False)
    v_cmpt = nsa_compress(blocks16(vals[:, :kvw]), cv_pos, cv_w1, cv_w2, True)
    return nsa_attention_t(roped, vt_tiles(vals[:, kvw:2 * kvw]), vt_tiles(vals[:, 2 * kvw:]), k_cmp, v_cmpt, gl, B, S)


def kernel(x, l0_nsa_w_in, l0_nsa_w_out, l0_nsa_ck_pos, l0_nsa_ck_w1, l0_nsa_ck_w2, l0_nsa_cv_pos, l0_nsa_cv_w1, l0_nsa_cv_w2, l0_ln1_g, l0_ln1_b, l0_router_w, l0_router_b, l0_moe_w_in, l0_moe_w_out, l0_shared_w_in, l0_shared_w_out, l0_ln2_g, l0_ln2_b, l1_gla_w_in, l1_gla_w_a2, l1_gla_b_a, l1_gla_norm_g, l1_gla_w_out, l1_ln1_g, l1_ln1_b, l1_router_w, l1_router_b, l1_moe_w_in, l1_moe_w_out, l1_shared_w_in, l1_shared_w_out, l1_ln2_g, l1_ln2_b, l2_dil_w_in, l2_dil_w_out, l2_ln1_g, l2_ln1_b, l2_router_w, l2_router_b, l2_moe_w_in, l2_moe_w_out, l2_shared_w_in, l2_shared_w_out, l2_ln2_g, l2_ln2_b, l3_nsa_w_in, l3_nsa_w_out, l3_nsa_ck_pos, l3_nsa_ck_w1, l3_nsa_ck_w2, l3_nsa_cv_pos, l3_nsa_cv_w1, l3_nsa_cv_w2, l3_ln1_g, l3_ln1_b, l3_router_w, l3_router_b, l3_moe_w_in, l3_moe_w_out, l3_shared_w_in, l3_shared_w_out, l3_ln2_g, l3_ln2_b):
    B, S, D = x.shape
    xf = x.reshape(B * S, D)
    xb = xf.astype(BF16)

    h = nsa_mixer(xb, l0_nsa_w_in, l0_nsa_ck_pos, l0_nsa_ck_w1, l0_nsa_ck_w2, l0_nsa_cv_pos, l0_nsa_cv_w1, l0_nsa_cv_w2, B, S)
    xf, xb = matmul_res_ln(h, l0_nsa_w_out.astype(BF16), xf, l0_ln1_g, l0_ln1_b)
    xf, xb = moe_layer(xf, xb, l0_router_w, l0_router_b, l0_moe_w_in, l0_moe_w_out, l0_shared_w_in, l0_shared_w_out, l0_ln2_g, l0_ln2_b)

    h = gla_mixer(xb, l1_gla_w_in, l1_gla_w_a2, l1_gla_b_a, l1_gla_norm_g, B, S)
    xf, xb = matmul_res_ln(h, l1_gla_w_out.astype(BF16), xf, l1_ln1_g, l1_ln1_b)
    xf, xb = moe_layer(xf, xb, l1_router_w, l1_router_b, l1_moe_w_in, l1_moe_w_out, l1_shared_w_in, l1_shared_w_out, l1_ln2_g, l1_ln2_b)

    xf, xb = dilated_layer(xf, xb, l2_dil_w_in, l2_dil_w_out, l2_ln1_g, l2_ln1_b, B, S)
    xf, xb = moe_layer(xf, xb, l2_router_w, l2_router_b, l2_moe_w_in, l2_moe_w_out, l2_shared_w_in, l2_shared_w_out, l2_ln2_g, l2_ln2_b)

    h = nsa_mixer(xb, l3_nsa_w_in, l3_nsa_ck_pos, l3_nsa_ck_w1, l3_nsa_ck_w2, l3_nsa_cv_pos, l3_nsa_cv_w1, l3_nsa_cv_w2, B, S)
    xf, xb = matmul_res_ln(h, l3_nsa_w_out.astype(BF16), xf, l3_ln1_g, l3_ln1_b)
    xf, xb = moe_layer(xf, xb, l3_router_w, l3_router_b, l3_moe_w_in, l3_moe_w_out, l3_shared_w_in, l3_shared_w_out, l3_ln2_g, l3_ln2_b)
    return xf.reshape(B, S, D)
```

```python
import functools
import math

import jax
import jax.numpy as jnp
import numpy as np
from jax import lax
from jax.experimental import pallas as pl
from jax.experimental.pallas import tpu as pltpu

F32 = jnp.float32
BF16 = jnp.bfloat16

D_MODEL = 1024
DEPTH = 4
HEAD_DIM = 64
ROPE_THETA = 500000.0
ROPE_DIM = HEAD_DIM // 4
ROPE_HALF = ROPE_DIM // 2

N_EXPERTS = 64
TOP_K = 8
N_GROUPS = 8
TOPK_GROUPS = 4
GROUP_SIZE = N_EXPERTS // N_GROUPS
D_EXPERT = 256
ROUTED_SCALE = 2.5

DN_ALPHA = (2.0 * DEPTH) ** 0.25
LN_EPS = 1e-5
NEG_INF = -1e30

LANES = 128
VMEM_LIMIT = 48 * 1024 * 1024


def _cparams(sem, **kw):
    return pltpu.CompilerParams(dimension_semantics=sem, vmem_limit_bytes=VMEM_LIMIT, **kw)


def _pick(n, pref):
    t = min(pref, n)
    while n % t:
        t //= 2
    return t


def _mm_kernel(x_ref, w_ref, o_ref):
    o_ref[...] = jnp.dot(x_ref[...], w_ref[...], preferred_element_type=F32).astype(o_ref.dtype)


def _mm_rope_kernel(x_ref, w_ref, c_ref, sm_ref, sp_ref, o_ref):
    y = jnp.dot(x_ref[...], w_ref[...], preferred_element_type=F32)
    reps = y.shape[1] // LANES
    c = jnp.tile(c_ref[...], (1, reps))
    sm = jnp.tile(sm_ref[...], (1, reps))
    sp = jnp.tile(sp_ref[...], (1, reps))
    up = pltpu.roll(y, y.shape[1] - ROPE_HALF, 1)
    dn = pltpu.roll(y, ROPE_HALF, 1)
    o_ref[...] = (y * c + up * sm + dn * sp).astype(o_ref.dtype)


def matmul(x, w, out_dtype=F32, rope=None, tm=None, tn=1024):
    M, K = x.shape
    N = w.shape[1]
    tm = _pick(M, tm or (1024 if rope is None else 512))
    tn = _pick(N, tn)
    grid = (N // tn, M // tm)
    x_spec = pl.BlockSpec((tm, K), lambda j, i: (i, 0))
    w_spec = pl.BlockSpec((K, tn), lambda j, i: (0, j))
    o_spec = pl.BlockSpec((tm, tn), lambda j, i: (i, j))
    if rope is None:
        return pl.pallas_call(
            _mm_kernel, out_shape=jax.ShapeDtypeStruct((M, N), out_dtype), grid=grid,
            in_specs=[x_spec, w_spec], out_specs=o_spec,
            compiler_params=_cparams(("parallel", "parallel")), name="mm")(x, w)
    R = rope[0].shape[0]
    tm = _pick(R, tm)
    grid = (N // tn, M // tm)
    x_spec = pl.BlockSpec((tm, K), lambda j, i: (i, 0))
    o_spec = pl.BlockSpec((tm, tn), lambda j, i: (i, j))
    nr = R // tm
    t_spec = pl.BlockSpec((tm, LANES), lambda j, i: (i % nr, 0))
    return pl.pallas_call(
        _mm_rope_kernel, out_shape=jax.ShapeDtypeStruct((M, N), out_dtype), grid=grid,
        in_specs=[x_spec, w_spec, t_spec, t_spec, t_spec], out_specs=o_spec,
        compiler_params=_cparams(("parallel", "parallel")), name="mm_rope")(x, w, *rope)


def rope_tables(pos):
    inv = ROPE_THETA ** (-jnp.arange(ROPE_HALF, dtype=F32) * 2.0 / ROPE_DIM)
    ang = pos.astype(F32)[:, None] * inv[None, :]
    cos, sin = jnp.cos(ang), jnp.sin(ang)
    n = pos.shape[0]
    ones = jnp.ones((n, HEAD_DIM - ROPE_DIM), F32)
    zeros = jnp.zeros((n, HEAD_DIM - ROPE_DIM), F32)
    zh = jnp.zeros((n, ROPE_HALF), F32)
    c = jnp.concatenate([cos, cos, ones], 1)
    sm = jnp.concatenate([-sin, zh, zeros], 1)
    sp = jnp.concatenate([zh, sin, zeros], 1)
    return tuple(jnp.tile(t, (1, LANES // HEAD_DIM)) for t in (c, sm, sp))


def _ln(v, g, b):
    mu = jnp.mean(v, axis=-1, keepdims=True)
    d = v - mu
    var = jnp.mean(d * d, axis=-1, keepdims=True)
    return d * lax.rsqrt(var + LN_EPS) * g + b


def _mm_res_ln_kernel(a_ref, w_ref, x_ref, g_ref, b_ref, o_ref, ob_ref):
    h = jnp.dot(a_ref[...], w_ref[...], preferred_element_type=F32)
    y = _ln(DN_ALPHA * x_ref[...] + h, g_ref[...], b_ref[...])
    o_ref[...] = y
    ob_ref[...] = y.astype(BF16)


def matmul_res_ln(a, w, x, g, b, tm=512):
    M, K = a.shape
    D = w.shape[1]
    tm = _pick(M, tm)
    row = lambda i: (i, 0)
    fix = lambda i: (0, 0)
    return pl.pallas_call(
        _mm_res_ln_kernel,
        out_shape=(jax.ShapeDtypeStruct((M, D), F32), jax.ShapeDtypeStruct((M, D), BF16)),
        grid=(M // tm,),
        in_specs=[pl.BlockSpec((tm, K), row), pl.BlockSpec((K, D), fix), pl.BlockSpec((tm, D), row),
                  pl.BlockSpec((1, D), fix), pl.BlockSpec((1, D), fix)],
        out_specs=(pl.BlockSpec((tm, D), row), pl.BlockSpec((tm, D), row)),
        compiler_params=_cparams(("parallel",)), name="mm_res_ln")(a, w, x, g.reshape(1, D), b.reshape(1, D))


def _first_index_of_max(v, iota, axis, n):
    m = jnp.max(v, axis=axis, keepdims=True)
    idx = jnp.min(jnp.where(v == m, iota, n), axis=axis, keepdims=True)
    return m, idx


def _router_kernel(x_ref, wh_ref, wl_ref, rb_ref, tri_ref, slot_ref, gatew_ref, cnt_ref):
    x = x_ref[...]
    xh = x.astype(BF16)
    xl = (x - xh.astype(F32)).astype(BF16)
    dn = (((1,), (1,)), ((), ()))
    logits = (lax.dot_general(wh_ref[...], xh, dn, preferred_element_type=F32)
              + lax.dot_general(wh_ref[...], xl, dn, preferred_element_type=F32)
              + lax.dot_general(wl_ref[...], xh, dn, preferred_element_type=F32))
    tm = logits.shape[1]
    s = jax.nn.sigmoid(logits)
    sb = s + rb_ref[...]
    sb3 = sb.reshape(N_GROUPS, GROUP_SIZE, tm)
    io3 = lax.broadcasted_iota(jnp.int32, sb3.shape, 1)
    m1, i1 = _first_index_of_max(sb3, io3, 1, GROUP_SIZE)
    m2 = jnp.max(jnp.where(io3 == i1, -jnp.inf, sb3), axis=1, keepdims=True)
    gs = (m1 + m2).reshape(N_GROUPS, tm)
    iog = lax.broadcasted_iota(jnp.int32, gs.shape, 0)
    gmask = jnp.zeros(gs.shape, jnp.bool_)
    for _ in range(TOPK_GROUPS):
        _, gi = _first_index_of_max(gs, iog, 0, N_GROUPS)
        pick = iog == gi
        gmask = gmask | pick
        gs = jnp.where(pick, -jnp.inf, gs)
    emask = jnp.broadcast_to(gmask.reshape(N_GROUPS, 1, tm), sb3.shape).reshape(N_EXPERTS, tm)
    cand = jnp.where(emask, sb, NEG_INF)
    ioe = lax.broadcasted_iota(jnp.int32, cand.shape, 0)
    sel = jnp.zeros(cand.shape, jnp.bool_)
    picked = []
    for _ in range(TOP_K):
        _, ei = _first_index_of_max(cand, ioe, 0, N_EXPERTS)
        pick = ioe == ei
        picked.append(jnp.sum(jnp.where(pick, s, 0.0), axis=0, keepdims=True))
        sel = sel | pick
        cand = jnp.where(pick, -jnp.inf, cand)
    total = picked[0]
    for g in picked[1:]:
        total = total + g
    gatew_ref[...] = jnp.where(sel, s / total * ROUTED_SCALE, 0.0)
    routed = jnp.where(sel, 1.0, 0.0)
    before = jnp.dot(routed.astype(BF16), tri_ref[...], preferred_element_type=F32)
    slot_ref[...] = jnp.where(sel, before, -1.0).astype(jnp.int32)
    cnt_ref[0] = jnp.broadcast_to(jnp.sum(routed, axis=1, keepdims=True), cnt_ref.shape[1:]).astype(jnp.int32)


def moe_router(x, router_w, router_b):
    N, D = x.shape
    tm = MOE_TILE
    assert N % tm == 0
    wt = router_w.T
    wh = wt.astype(BF16)
    wl = (wt - wh.astype(F32)).astype(BF16)
    tri = (jnp.arange(tm)[:, None] < jnp.arange(tm)[None, :]).astype(BF16)
    fix = lambda i: (0, 0)
    col = lambda i: (0, i)
    slot, gatew, cnt = pl.pallas_call(
        _router_kernel,
        out_shape=(jax.ShapeDtypeStruct((N_EXPERTS, N), jnp.int32), jax.ShapeDtypeStruct((N_EXPERTS, N), F32),
                   jax.ShapeDtypeStruct((N // tm, N_EXPERTS, LANES), jnp.int32)),
        grid=(N // tm,),
        in_specs=[pl.BlockSpec((tm, D), lambda i: (i, 0)), pl.BlockSpec((N_EXPERTS, D), fix),
                  pl.BlockSpec((N_EXPERTS, D), fix), pl.BlockSpec((N_EXPERTS, 1), fix), pl.BlockSpec((tm, tm), fix)],
        out_specs=(pl.BlockSpec((N_EXPERTS, tm), col), pl.BlockSpec((N_EXPERTS, tm), col),
                   pl.BlockSpec((1, N_EXPERTS, LANES), lambda i: (i, 0, 0))),
        compiler_params=_cparams(("arbitrary",)), name="moe_router",
    )(x, wh, wl, router_b.reshape(N_EXPERTS, 1).astype(F32), tri)
    return slot, gatew, cnt[:, :, 0]


MOE_TILE = 256
MOE_CAP = 48
MOE_EGROUP = 16
MOE_BLOCK = 512
MOE_ALIGN = 16


def _slot_matrix(slot_rows, first, weights=None):
    n_e, T = slot_rows.shape
    r = lax.broadcasted_iota(jnp.int32, (MOE_CAP, T), 0) + first
    pieces = []
    for e in range(n_e):
        hit = slot_rows[e:e + 1, :] == r
        w = 1.0 if weights is None else weights[e:e + 1, :]
        pieces.append(jnp.where(hit, w, 0.0).astype(BF16))
    return pieces[0] if n_e == 1 else jnp.concatenate(pieces, axis=0)


def _n_windows(n_rows):
    return (n_rows + MOE_CAP - 1) // MOE_CAP


def _dispatch_kernel(base_ref, cnt_ref, zrow_ref, slot_ref, xb_ref, xs_ref, obuf, ovbuf, zbuf, sems, ovsem, zsem):
    t = pl.program_id(0)
    C, EG, E = MOE_CAP, MOE_EGROUP, N_EXPERTS
    n_groups = E // EG
    tz = zbuf.shape[0]

    @pl.when(t == 0)
    def _():
        zbuf[...] = jnp.zeros_like(zbuf)

        def zcopy(j):
            return pltpu.make_async_copy(zbuf, xs_ref.at[pl.ds(pl.multiple_of(zrow_ref[j], tz), tz)], zsem)

        def zstart(j, c):
            @pl.when(zrow_ref[j] >= 0)
            def _():
                zcopy(j).start()
            return c

        def zwait(j, c):
            @pl.when(zrow_ref[j] >= 0)
            def _():
                zcopy(j).wait()
            return c

        lax.fori_loop(0, zrow_ref.shape[0], zstart, 0)
        lax.fori_loop(0, zrow_ref.shape[0], zwait, 0)

    xb = xb_ref[...]

    def window(tile, e, extra):
        return xs_ref.at[pl.ds(pl.multiple_of(base_ref[tile * E + e] + extra, MOE_ALIGN), C)]

    def copies(tile, g):
        return [pltpu.make_async_copy(obuf.at[g % 2, pl.ds(e * C, C)], window(tile, g * EG + e, 0), sems.at[g % 2])
                for e in range(EG)]

    def wait_group(tile, g):
        for cp in copies(tile, g):
            cp.wait()

    assert n_groups % 2 == 0 and n_groups >= 2
    for g in range(n_groups):
        if g >= 2:
            wait_group(t, g - 2)
        else:
            @pl.when(t > 0)
            def _():
                wait_group(t - 1, n_groups - 2 + g)
        onehot = _slot_matrix(slot_ref[g * EG:(g + 1) * EG, :], 0)
        obuf[g % 2] = jnp.dot(onehot, xb, preferred_element_type=F32).astype(BF16)
        for cp in copies(t, g):
            cp.start()

    @pl.when(t == pl.num_programs(0) - 1)
    def _():
        wait_group(t, n_groups - 2)
        wait_group(t, n_groups - 1)

    def overflow(e, c):
        def chunk(j, c2):
            onehot = _slot_matrix(slot_ref[pl.ds(e, 1), :], j * C)
            ovbuf[...] = jnp.dot(onehot, xb, preferred_element_type=F32).astype(BF16)
            cp = pltpu.make_async_copy(ovbuf, window(t, e, j * C), ovsem)
            cp.start()
            cp.wait()
            return c2

        return lax.fori_loop(1, _n_windows(cnt_ref[t * E + e]), chunk, c)

    @pl.when(cnt_ref[pl.num_programs(0) * E + t] > C)
    def _():
        lax.fori_loop(0, E, overflow, 0)


def moe_dispatch(xb, slot, base, cnt, zrow, P):
    N, D = xb.shape
    T, C, EG = MOE_TILE, MOE_CAP, MOE_EGROUP
    grid_spec = pltpu.PrefetchScalarGridSpec(
        num_scalar_prefetch=3, grid=(N // T,),
        in_specs=[pl.BlockSpec((N_EXPERTS, T), lambda i, *_: (0, i)),
                  pl.BlockSpec((T, D), lambda i, *_: (i, 0))],
        out_specs=pl.BlockSpec(memory_space=pl.ANY),
        scratch_shapes=[pltpu.VMEM((2, EG * C, D), BF16), pltpu.VMEM((C, D), BF16), pltpu.VMEM((MOE_BLOCK, D), BF16),
                        pltpu.SemaphoreType.DMA((2,)), pltpu.SemaphoreType.DMA(()), pltpu.SemaphoreType.DMA(())])
    return pl.pallas_call(
        _dispatch_kernel, out_shape=jax.ShapeDtypeStruct((P, D), BF16), grid_spec=grid_spec,
        compiler_params=_cparams(("arbitrary",), has_side_effects=True), name="moe_dispatch",
    )(base.reshape(-1), cnt.reshape(-1), zrow, slot, xb)


def _swiglu(x_bf16, w_in, w_out):
    h = jnp.dot(x_bf16, w_in, preferred_element_type=F32)
    f = h.shape[1] // 2
    a = jax.nn.silu(h[:, :f]) * h[:, f:]
    return jnp.dot(a.astype(BF16), w_out, preferred_element_type=F32)


def _expert_kernel(blk_e_ref, used_ref, xs_ref, wi_ref, wo_ref, y_ref):
    del blk_e_ref
    live = pl.program_id(0) < used_ref[0]

    @pl.when(live)
    def _():
        y_ref[...] = _swiglu(xs_ref[...], wi_ref[0], wo_ref[0]).astype(y_ref.dtype)

    @pl.when(jnp.logical_not(live))
    def _():
        y_ref[...] = jnp.zeros_like(y_ref)


def moe_experts(xs, blk_e, n_used, w_in_e, w_out_e):
    P, D = xs.shape
    tm = MOE_BLOCK
    F2 = w_in_e.shape[2]
    spare = P // tm - 1
    rows = lambda i, be, nu: (jnp.where(i < nu[0], i, spare), 0)
    grid_spec = pltpu.PrefetchScalarGridSpec(
        num_scalar_prefetch=2, grid=(P // tm,),
        in_specs=[pl.BlockSpec((tm, D), rows),
                  pl.BlockSpec((1, D, F2), lambda i, be, nu: (be[i], 0, 0)),
                  pl.BlockSpec((1, F2 // 2, D), lambda i, be, nu: (be[i], 0, 0))],
        out_specs=pl.BlockSpec((tm, D), rows))
    return pl.pallas_call(
        _expert_kernel, out_shape=jax.ShapeDtypeStruct((P, D), BF16), grid_spec=grid_spec,
        compiler_params=_cparams(("arbitrary",)), name="moe_experts")(blk_e, n_used, xs, w_in_e, w_out_e)


def _combine_kernel(base_ref, cnt_ref, slot_ref, gw_ref, x_ref, wi_ref, wo_ref, g_ref, b_ref, ys_ref, o_ref, ob_ref,
                    ybuf, ovbuf, acc_ref, sems, ovsem):
    t = pl.program_id(0)
    C, EG, E = MOE_CAP, MOE_EGROUP, N_EXPERTS
    n_groups = E // EG
    rows_in = (((0,), (0,)), ((), ()))

    def window(e, extra):
        return ys_ref.at[pl.ds(pl.multiple_of(base_ref[t * E + e] + extra, MOE_ALIGN), C)]

    def copies(g):
        return [pltpu.make_async_copy(window(g * EG + e, 0), ybuf.at[g % 2, pl.ds(e * C, C)], sems.at[g % 2])
                for e in range(EG)]

    for cp in copies(0):
        cp.start()
    x = x_ref[...]
    acc_ref[...] = _swiglu(x.astype(BF16), wi_ref[...], wo_ref[...])
    for g in range(n_groups):
        if g + 1 < n_groups:
            for cp in copies(g + 1):
                cp.start()
        for cp in copies(g):
            cp.wait()
        spread = _slot_matrix(slot_ref[g * EG:(g + 1) * EG, :], 0, gw_ref[g * EG:(g + 1) * EG, :])
        acc_ref[...] += lax.dot_general(spread, ybuf[g % 2], rows_in, preferred_element_type=F32)

    def overflow(e, c):
        def chunk(j, c2):
            cp = pltpu.make_async_copy(window(e, j * C), ovbuf, ovsem)
            cp.start()
            cp.wait()
            spread = _slot_matrix(slot_ref[pl.ds(e, 1), :], j * C, gw_ref[pl.ds(e, 1), :])
            acc_ref[...] += lax.dot_general(spread, ovbuf[...], rows_in, preferred_element_type=F32)
            return c2

        return lax.fori_loop(1, _n_windows(cnt_ref[t * E + e]), chunk, c)

    @pl.when(cnt_ref[pl.num_programs(0) * E + t] > C)
    def _():
        lax.fori_loop(0, E, overflow, 0)
    y = _ln(DN_ALPHA * x + acc_ref[...], g_ref[...], b_ref[...])
    o_ref[...] = y
    ob_ref[...] = y.astype(BF16)


def moe_combine(x, ys, slot, gatew, base, cnt, w_in_s, w_out_s, g, b):
    N, D = x.shape
    T, C, EG = MOE_TILE, MOE_CAP, MOE_EGROUP
    F2 = w_in_s.shape[1]
    row = lambda i, *_: (i, 0)
    col = lambda i, *_: (0, i)
    fix = lambda i, *_: (0, 0)
    grid_spec = pltpu.PrefetchScalarGridSpec(
        num_scalar_prefetch=2, grid=(N // T,),
        in_specs=[pl.BlockSpec((N_EXPERTS, T), col), pl.BlockSpec((N_EXPERTS, T), col), pl.BlockSpec((T, D), row),
                  pl.BlockSpec((D, F2), fix), pl.BlockSpec((F2 // 2, D), fix),
                  pl.BlockSpec((1, D), fix), pl.BlockSpec((1, D), fix),
                  pl.BlockSpec(memory_space=pl.ANY)],
        out_specs=(pl.BlockSpec((T, D), row), pl.BlockSpec((T, D), row)),
        scratch_shapes=[pltpu.VMEM((2, EG * C, D), BF16), pltpu.VMEM((C, D), BF16), pltpu.VMEM((T, D), F32),
                        pltpu.SemaphoreType.DMA((2,)), pltpu.SemaphoreType.DMA(())])
    return pl.pallas_call(
        _combine_kernel,
        out_shape=(jax.ShapeDtypeStruct((N, D), F32), jax.ShapeDtypeStruct((N, D), BF16)),
        grid_spec=grid_spec,
        compiler_params=_cparams(("arbitrary",)), name="moe_combine",
    )(base.reshape(-1), cnt.reshape(-1), slot, gatew, x, w_in_s, w_out_s, g.reshape(1, D), b.reshape(1, D), ys)


def moe_layer(x, xb, router_w, router_b, w_in_e, w_out_e, w_in_s, w_out_s, ln_g, ln_b):
    N, D = x.shape
    E, C, A, tm_e = N_EXPERTS, MOE_CAP, MOE_ALIGN, MOE_BLOCK
    n_t = N // MOE_TILE
    slot, gatew, cnt = moe_router(x, router_w, router_b)
    seg = (cnt + A - 1) // A * A
    padded = (jnp.sum(seg, axis=0) + C + tm_e - 1) // tm_e * tm_e
    pad_end = jnp.cumsum(padded)
    pad_start = pad_end - padded
    base = (pad_start[None, :] + jnp.cumsum(seg, axis=0) - seg).astype(jnp.int32)
    max_rows = N * TOP_K + n_t * E * (A - 1) + E * (C + tm_e - 1)
    n_blk = -(-max_rows // tm_e) + 1
    blk_row = jnp.arange(n_blk, dtype=jnp.int32) * tm_e
    blk_e = jnp.minimum(jnp.sum(pad_end[None, :] <= blk_row[:, None], axis=1), E - 1).astype(jnp.int32)
    n_used = (pad_end[-1:] // tm_e).astype(jnp.int32)
    zrow = jnp.concatenate([pad_end - tm_e, jnp.where(padded >= 2 * tm_e, pad_end - 2 * tm_e, -1)]).astype(jnp.int32)
    cnt_ext = jnp.concatenate([cnt.reshape(-1), jnp.max(cnt, axis=1)])
    xs = moe_dispatch(xb, slot, base, cnt_ext, zrow, n_blk * tm_e)
    ys = moe_experts(xs, blk_e, n_used, w_in_e.astype(BF16), w_out_e.astype(BF16))
    return moe_combine(x, ys, slot, gatew, base, cnt_ext, w_in_s.astype(BF16), w_out_s.astype(BF16), ln_g, ln_b)


GLA_HEADS = 4
GLA_DK = D_MODEL // 2 // GLA_HEADS
GLA_DV = D_MODEL // GLA_HEADS
GLA_GATE_RANK = 16
GLA_TAU = 16.0
GLA_CHUNK = 64


def _split3(v):
    h1 = v.astype(BF16)
    r1 = v - h1.astype(F32)
    h2 = r1.astype(BF16)
    h3 = (r1 - h2.astype(F32)).astype(BF16)
    return h1, h2, h3


def _gla_kernel(qkvr_ref, a_ref, wah_ref, wal_ref, ba_ref, ng_ref, tri_ref, o_ref, st_ref):
    H, dk, dv, C = GLA_HEADS, GLA_DK, GLA_DV, GLA_CHUNK

    @pl.when(pl.program_id(1) == 0)
    def _():
        st_ref[...] = jnp.zeros_like(st_ref)

    T = qkvr_ref.shape[0]
    a = a_ref[...]
    ah = a.astype(BF16)
    al = (a - ah.astype(F32)).astype(BF16)
    glog = (jnp.dot(ah, wah_ref[...], preferred_element_type=F32)
            + jnp.dot(al, wah_ref[...], preferred_element_type=F32)
            + jnp.dot(ah, wal_ref[...], preferred_element_type=F32)) + ba_ref[...]
    log_a = jax.nn.log_sigmoid(glog) / GLA_TAU
    tri = tri_ref[...]
    rr = lax.broadcasted_iota(jnp.int32, (C, C), 0)
    cc = lax.broadcasted_iota(jnp.int32, (C, C), 1)
    causal = rr >= cc
    ng = ng_ref[...]
    ct = (((1,), (1,)), ((), ()))
    c0 = (((0,), (0,)), ((), ()))
    for c in range(T // C):
        rows = slice(c * C, (c + 1) * C)
        for h in range(H):
            la = log_a[rows, h * dk:(h + 1) * dk]
            p1, p2, p3 = _split3(la)
            b = (jnp.dot(tri, p1, preferred_element_type=F32) + jnp.dot(tri, p2, preferred_element_type=F32)
                 + jnp.dot(tri, p3, preferred_element_type=F32))
            b_last = b[C - 1:C, :]
            q = qkvr_ref[rows, h * dk:(h + 1) * dk].astype(F32)
            k = qkvr_ref[rows, H * dk + h * dk:H * dk + (h + 1) * dk].astype(F32)
            v = qkvr_ref[rows, 2 * H * dk + h * dv:2 * H * dk + (h + 1) * dv]
            r = qkvr_ref[rows, 2 * H * dk + H * dv + h * dv:2 * H * dk + H * dv + (h + 1) * dv].astype(F32)
            qg = (q * jnp.exp(b)).astype(BF16)
            kg = (k * jnp.exp(-b)).astype(BF16)
            kd = (k * jnp.exp(b_last - b)).astype(BF16)
            att = jnp.where(causal, lax.dot_general(qg, kg, ct, preferred_element_type=F32), 0.0)
            st = st_ref[h]
            o = (jnp.dot(att.astype(BF16), v, preferred_element_type=F32)
                 + lax.dot_general(qg, st.astype(BF16), ct, preferred_element_type=F32))
            st_ref[h] = jnp.exp(b_last) * st + lax.dot_general(v, kd, c0, preferred_element_type=F32)
            o = o * lax.rsqrt(jnp.mean(o * o, axis=-1, keepdims=True) + LN_EPS) * ng
            o_ref[rows, h * dv:(h + 1) * dv] = (o * jax.nn.silu(r)).astype(o_ref.dtype)


def gla_core(qkvr, a, w_a2, b_a, norm_g, B, S, tile=256):
    H, dk, dv, C = GLA_HEADS, GLA_DK, GLA_DV, GLA_CHUNK
    N, W = qkvr.shape
    tile = _pick(S, tile)
    nt = S // tile
    wa = jnp.zeros((LANES, H * dk), F32).at[:GLA_GATE_RANK].set(w_a2)
    wah = wa.astype(BF16)
    wal = (wa - wah.astype(F32)).astype(BF16)
    tri = (jnp.arange(C)[:, None] >= jnp.arange(C)[None, :]).astype(BF16)
    row = lambda b, t: (b * nt + t, 0)
    fix = lambda b, t: (0, 0)
    return pl.pallas_call(
        _gla_kernel, out_shape=jax.ShapeDtypeStruct((N, H * dv), BF16), grid=(B, nt),
        in_specs=[pl.BlockSpec((tile, W), row), pl.BlockSpec((tile, LANES), row),
                  pl.BlockSpec((LANES, H * dk), fix), pl.BlockSpec((LANES, H * dk), fix),
                  pl.BlockSpec((1, H * dk), fix), pl.BlockSpec((1, dv), fix), pl.BlockSpec((C, C), fix)],
        out_specs=pl.BlockSpec((tile, H * dv), row),
        scratch_shapes=[pltpu.VMEM((H, dv, dk), F32)],
        compiler_params=_cparams(("parallel", "arbitrary")), name="gla_core",
    )(qkvr, a, wah, wal, b_a.reshape(1, H * dk), norm_g.reshape(1, dv), tri)


def gla_mixer(xb, w_in, w_a2, b_a, norm_g, B, S):
    H, dk, dv = GLA_HEADS, GLA_DK, GLA_DV
    hk, hv = H * dk, H * dv
    wq, wk, wv, wa, wr = jnp.split(w_in, [hk, 2 * hk, 2 * hk + hv, 2 * hk + hv + GLA_GATE_RANK], axis=1)
    w_main = jnp.concatenate([wq * dk ** -0.5, wk, wv, wr], axis=1).astype(BF16)
    w_gate = jnp.zeros((w_in.shape[0], LANES), F32).at[:, :GLA_GATE_RANK].set(wa).astype(BF16)
    qkvr = matmul(xb, w_main, out_dtype=BF16)
    a = matmul(xb, w_gate, out_dtype=F32)
    return gla_core(qkvr, a, w_a2, b_a, norm_g, B, S)


DIL_HEADS = D_MODEL // HEAD_DIM
DIL_CONFIGS = ((128, 1), (512, 4), (2048, 16))
DIL_TQ = 128


def _dil_attn_kernel(q_ref, kc_ref, kp_ref, vc_ref, vp_ref, o_ref, lse_ref, *, tiles_per_seq, window):
    i = pl.program_id(0)
    tq = q_ref.shape[0]
    first = (i % tiles_per_seq) == 0
    qpos = lax.broadcasted_iota(jnp.int32, (tq, 2 * tq), 0) + tq
    kpos = lax.broadcasted_iota(jnp.int32, (tq, 2 * tq), 1)
    dist = qpos - kpos
    mask = (dist >= 0) & (dist <= window) & ((kpos >= tq) | jnp.logical_not(first))
    lane = lax.broadcasted_iota(jnp.int32, (tq, LANES), 1)
    lo = lane < HEAD_DIM
    ct = (((1,), (1,)), ((), ()))
    for p in range(q_ref.shape[1] // LANES):
        cols = slice(p * LANES, (p + 1) * LANES)
        q2 = q_ref[:, cols]
        k2 = jnp.concatenate([kp_ref[:, cols], kc_ref[:, cols]], axis=0)
        v2 = jnp.concatenate([vp_ref[:, cols], vc_ref[:, cols]], axis=0)
        outs, lses = [], []
        for half in (lo, jnp.logical_not(lo)):
            qm = jnp.where(half, q2, jnp.zeros_like(q2))
            s = jnp.where(mask, lax.dot_general(qm, k2, ct, preferred_element_type=F32), NEG_INF)
            m = jnp.max(s, axis=-1, keepdims=True)
            e = jnp.where(mask, jnp.exp(s - m), 0.0)
            den = jnp.sum(e, axis=-1, keepdims=True)
            pv = jnp.dot(e.astype(BF16), v2, preferred_element_type=F32)
            outs.append(pv / jnp.maximum(den, 1e-30))
            lses.append(m + jnp.log(den))
        o_ref[:, cols] = jnp.where(lo, outs[0], outs[1]).astype(o_ref.dtype)
        lse_ref[:, cols] = jnp.where(lo, lses[0], lses[1])


def dil_attention(qk, v, seq_len, window):
    N, W = v.shape
    tq = DIL_TQ
    assert window == tq and seq_len % tq == 0
    nwb = 1
    cur = lambda i: (i, 0)
    prev = lambda i: (jnp.maximum(i - 1, 0), 0)
    return pl.pallas_call(
        functools.partial(_dil_attn_kernel, tiles_per_seq=seq_len // tq, window=window),
        out_shape=(jax.ShapeDtypeStruct((N, W), BF16), jax.ShapeDtypeStruct((N, W), F32)),
        grid=(N // tq,),
        in_specs=[pl.BlockSpec((tq, W), cur),
                  pl.BlockSpec((tq, W), lambda i: (i, nwb)),
                  pl.BlockSpec((tq, W), lambda i: (jnp.maximum(i - 1, 0), nwb)),
                  pl.BlockSpec((tq, W), cur), pl.BlockSpec((tq, W), prev)],
        out_specs=(pl.BlockSpec((tq, W), cur), pl.BlockSpec((tq, W), cur)),
        compiler_params=_cparams(("parallel",)), name="dil_attn",
    )(qk, qk, qk, v, v)


def _dil_merge_kernel(o0, o1, o2, l0, l1, l2, w_ref, x_ref, g_ref, b_ref, out_ref, outb_ref):
    la, lb, lc = l0[...], l1[...], l2[...]
    m = jnp.maximum(jnp.maximum(la, lb), lc)
    ea, eb, ec = jnp.exp(la - m), jnp.exp(lb - m), jnp.exp(lc - m)
    tot = ea + eb + ec
    o = (ea / tot) * o0[...].astype(F32) + (eb / tot) * o1[...].astype(F32) + (ec / tot) * o2[...].astype(F32)
    h = jnp.dot(o.astype(BF16), w_ref[...], preferred_element_type=F32)
    y = _ln(DN_ALPHA * x_ref[...] + h, g_ref[...], b_ref[...])
    out_ref[...] = y
    outb_ref[...] = y.astype(BF16)


def dil_merge_out(os_, lses, w_out, x, g, b, tm=256):
    N, D = x.shape
    W = w_out.shape[0]
    tm = _pick(N, tm)
    row = lambda i: (i, 0)
    fix = lambda i: (0, 0)
    rs = pl.BlockSpec((tm, W), row)
    return pl.pallas_call(
        _dil_merge_kernel,
        out_shape=(jax.ShapeDtypeStruct((N, D), F32), jax.ShapeDtypeStruct((N, D), BF16)),
        grid=(N // tm,),
        in_specs=[rs, rs, rs, rs, rs, rs, pl.BlockSpec((W, D), fix), pl.BlockSpec((tm, D), row),
                  pl.BlockSpec((1, D), fix), pl.BlockSpec((1, D), fix)],
        out_specs=(pl.BlockSpec((tm, D), row), pl.BlockSpec((tm, D), row)),
        compiler_params=_cparams(("parallel",)), name="dil_merge",
    )(*os_, *lses, w_out, x, g.reshape(1, D), b.reshape(1, D))


def dilated_layer(x, xb, w_in, w_out, ln_g, ln_b, B, S):
    N, D = x.shape
    H, dh = DIL_HEADS, HEAD_DIM
    W = H * dh
    w6 = w_in.reshape(D, len(DIL_CONFIGS), 3, W)
    pos = jnp.arange(S)
    os_, lses = [], []
    for gi, (window, dil) in enumerate(DIL_CONFIGS):
        L = S // dil
        wqk = jnp.concatenate([w6[:, gi, 0] * dh ** -0.5, w6[:, gi, 1]], axis=1).astype(BF16)
        wv = w6[:, gi, 2].astype(BF16)
        xp = xb.reshape(B, L, dil, D).transpose(0, 2, 1, 3).reshape(N, D) if dil > 1 else xb
        ppos = pos.reshape(L, dil).T.reshape(S)
        qk = matmul(xp, wqk, out_dtype=BF16, rope=rope_tables(ppos))
        v = matmul(xp, wv, out_dtype=BF16)
        o, lse = dil_attention(qk, v, L, window // dil)
        if dil > 1:
            o = o.reshape(B, dil, L, W).transpose(0, 2, 1, 3).reshape(N, W)
            lse = lse.reshape(B, dil, L, W).transpose(0, 2, 1, 3).reshape(N, W)
        os_.append(o)
        lses.append(lse)
    return dil_merge_out(os_, lses, w_out.astype(BF16), x, ln_g, ln_b)


NSA_HEADS = D_MODEL // HEAD_DIM
NSA_KV_HEADS = 4
NSA_GROUP = NSA_HEADS // NSA_KV_HEADS
CMP_LEN = 32
CMP_STRIDE = 16
CMP_HIDDEN = 256
SEL_LEN = 64
SEL_TOPN = 16
NSA_WINDOW = 512
FORCE_BONUS = 100.0
NSA_TQ = 128
NSA_TK = 128
NSA_CHUNK = 1024


def _compress_kernel(x_ref, p_ref, w1a_ref, w1b_ref, w2_ref, o_ref, *, transpose_out):
    x = x_ref[0].astype(F32)
    n = x.shape[0]
    first = jnp.dot((x + p_ref[0:1, :]).astype(BF16), w1a_ref[...], preferred_element_type=F32)
    second = jnp.dot((x + p_ref[1:2, :]).astype(BF16), w1b_ref[...], preferred_element_type=F32)
    hid = first + pltpu.roll(second, n - 1, 0)
    out = jnp.dot(jax.nn.gelu(hid).astype(BF16), w2_ref[...], preferred_element_type=F32)
    if transpose_out:
        o_ref[0] = out.T[:HEAD_DIM, :].astype(o_ref.dtype)
    else:
        o_ref[0] = out[:, :HEAD_DIM].astype(o_ref.dtype)


def nsa_compress(t, pos_emb, w1, w2, transpose_out):
    BK, n, W = t.shape
    half = CMP_STRIDE * HEAD_DIM
    p = pos_emb.reshape(2, half).astype(F32)
    w2p = jnp.zeros((CMP_HIDDEN, LANES), F32).at[:, :HEAD_DIM].set(w2).astype(BF16)
    oshape = (BK, HEAD_DIM, n) if transpose_out else (BK, n, HEAD_DIM)
    fix = lambda i: (0, 0)
    return pl.pallas_call(
        functools.partial(_compress_kernel, transpose_out=transpose_out),
        out_shape=jax.ShapeDtypeStruct(oshape, BF16), grid=(BK,),
        in_specs=[pl.BlockSpec((1, n, W), lambda i: (i, 0, 0)), pl.BlockSpec((2, half), fix),
                  pl.BlockSpec((half, CMP_HIDDEN), fix), pl.BlockSpec((half, CMP_HIDDEN), fix),
                  pl.BlockSpec((CMP_HIDDEN, LANES), fix)],
        out_specs=pl.BlockSpec((1,) + oshape[1:], lambda i: (i, 0, 0)),
        compiler_params=_cparams(("parallel",)), name="nsa_compress",
    )(t, p, w1[:half].astype(BF16), w1[half:].astype(BF16), w2p)


def _col_softmax_step(carry, s, valid, vt):
    m, l, acc = carry
    m_new = jnp.maximum(m, jnp.max(s, axis=0, keepdims=True))
    alpha = jnp.exp(m - m_new)
    p = jnp.exp(s - m_new)
    if valid is not None:
        p = jnp.where(valid, p, 0.0)
    l = alpha * l + jnp.sum(p, axis=0, keepdims=True)
    acc = alpha * acc + jnp.dot(vt, p.astype(BF16), preferred_element_type=F32)
    return m_new, l, acc


def _nsa_attn_t_kernel(q_ref, kc_ref, vct_ref, ks_ref, vst_ref, kw_ref, vwt_ref, oh_ref, gl_ref, ovl_ref, o_ref):
    kh = pl.program_id(1)
    i = pl.program_id(2)
    tq, dh, G = NSA_TQ, HEAD_DIM, NSA_GROUP
    tk = NSA_TK
    M = G * tq
    t0 = i * tq
    slot = kh % 2
    vrows = pl.ds(pl.multiple_of(slot * dh, dh), dh)

    def tpos(shape):
        return t0 + (lax.broadcasted_iota(jnp.int32, shape, 1) & (tq - 1))

    qn = q_ref[...].astype(F32)
    qt_pairs = [qn[:, c * LANES:(c + 1) * LANES].T for c in range(G * dh // LANES)]
    qt = jnp.concatenate([p[h * dh:(h + 1) * dh] for p in qt_pairs for h in range(LANES // dh)], axis=1)
    qt = qt.astype(BF16)

    n_cmp = kc_ref.shape[1]
    s_c = jnp.dot(kc_ref[0], qt, preferred_element_type=F32)
    cend = lax.broadcasted_iota(jnp.int32, (n_cmp, M), 0) * CMP_STRIDE + (CMP_LEN - 1)
    vis = cend <= tpos((n_cmp, M))
    s_c = jnp.where(vis, s_c, NEG_INF)
    e_c = jnp.where(vis, jnp.exp(s_c - jnp.max(s_c, axis=0, keepdims=True)), 0.0)
    p_c = e_c / jnp.maximum(jnp.sum(e_c, axis=0, keepdims=True), 1e-30)
    o_c = jnp.dot(vct_ref[0], p_c.astype(BF16), preferred_element_type=F32)

    psum = p_c[:, 0:tq]
    for g in range(1, G):
        psum = psum + p_c[:, g * tq:(g + 1) * tq]
    ovl = ovl_ref[...]
    imp = sum(jnp.dot(ovl, piece, preferred_element_type=F32) for piece in _split3(psum))
    n_sel = imp.shape[0]
    blk = lax.broadcasted_iota(jnp.int32, (n_sel, tq), 0)
    cur = tpos((n_sel, tq)) // SEL_LEN
    forced = (blk == 0) | (blk == cur) | (blk == cur - 1)
    score = jnp.where(blk <= cur, imp + jnp.where(forced, FORCE_BONUS, 0.0), -1.0)
    chosen = jnp.zeros((n_sel, tq), jnp.bool_)
    for _ in range(min(SEL_TOPN, n_sel)):
        mx, idx = _first_index_of_max(score, blk, 0, n_sel)
        hit = blk == idx
        chosen = chosen | (hit & (mx >= 0.0))
        score = jnp.where(hit, -2.0, score)
    bias = jnp.where(chosen, 0.0, NEG_INF).astype(BF16)
    zero = jnp.zeros_like(qt)
    q_pair = jnp.concatenate([jnp.where(slot == 0, qt, zero), jnp.where(slot == 1, qt, zero)], axis=0)
    pad = jnp.zeros((LANES - n_sel, M), BF16)
    q_aug = jnp.concatenate([q_pair, jnp.concatenate([bias] * G, axis=1), pad], axis=0)

    init = (jnp.full((1, M), NEG_INF, F32), jnp.zeros((1, M), F32), jnp.zeros((dh, M), F32))

    def vt_cat(ref, first_tile, n):
        return jnp.concatenate([ref[0, 0, first_tile + j, vrows, :] for j in range(n)], axis=1)

    ch = NSA_CHUNK
    per_chunk = ch // tk

    def sel_rows(start, n, carry, causal):
        rows = slice(start, start + n)
        k_aug = jnp.concatenate([ks_ref[rows, :], oh_ref[rows, :]], axis=1)
        s = jnp.dot(k_aug, q_aug, preferred_element_type=F32)
        if causal:
            kpos = start + lax.broadcasted_iota(jnp.int32, (n, M), 0)
            s = jnp.where(kpos <= tpos((n, M)), s, NEG_INF)
        return _col_softmax_step(carry, s, None, vt_cat(vst_ref, start // tk, n // tk))

    def sel_upto(n_full):
        def run():
            carry = init
            for c in range(n_full):
                carry = sel_rows(c * ch, ch, carry, False)
            _, l, acc = sel_rows(n_full * ch, ch, carry, True)
            return l, acc
        return run

    l_s, acc_s = lax.switch(t0 // ch, [sel_upto(n) for n in range(ks_ref.shape[0] // ch)])
    o_s = acc_s / jnp.maximum(l_s, 1e-30)

    n_wt = NSA_WINDOW // tk + 1
    wt0 = jnp.maximum(i + 1 - n_wt, 0)
    wrows = pl.ds(pl.multiple_of(wt0 * tk, tk), n_wt * tk)
    s_w = jnp.dot(kw_ref[wrows, :], q_pair, preferred_element_type=F32)
    dist = tpos(s_w.shape) - (wt0 * tk + lax.broadcasted_iota(jnp.int32, s_w.shape, 0))
    near = (dist >= 0) & (dist < NSA_WINDOW)
    s_w = jnp.where(near, s_w, NEG_INF)
    e_w = jnp.exp(s_w - jnp.max(s_w, axis=0, keepdims=True))
    l_w = jnp.sum(e_w, axis=0, keepdims=True)
    o_w = jnp.dot(vt_cat(vwt_ref, wt0, n_wt), e_w.astype(BF16), preferred_element_type=F32) / jnp.maximum(l_w, 1e-30)

    gates = jax.nn.sigmoid(gl_ref[...].T)
    outs = []
    for g in range(G):
        cols = slice(g * tq, (g + 1) * tq)
        outs.append(gates[g:g + 1] * o_c[:, cols] + gates[G + g:G + g + 1] * o_s[:, cols]
                    + gates[2 * G + g:2 * G + g + 1] * o_w[:, cols])
    per = LANES // dh
    o_ref[...] = jnp.concatenate(
        [jnp.concatenate(outs[c * per:(c + 1) * per], axis=0).T for c in range(G // per)], axis=1).astype(o_ref.dtype)


def nsa_attention_t(roped, vst, vwt, k_cmp, v_cmpt, gl, B, S):
    H, KH, G, dh = NSA_HEADS, NSA_KV_HEADS, NSA_GROUP, HEAD_DIM
    tq, tk = NSA_TQ, NSA_TK
    assert tq == tk and S % NSA_CHUNK == 0 and NSA_CHUNK % tk == 0 and S >= NSA_WINDOW + tq and G * dh == 2 * LANES
    nt = S // tq
    n_cmp = k_cmp.shape[1]
    n_sel = S // SEL_LEN
    c0 = np.arange(n_cmp)[None, :] * CMP_STRIDE
    s0 = np.arange(n_sel)[:, None] * SEL_LEN
    ovl = jnp.asarray((c0 < s0 + SEL_LEN) & (c0 + CMP_LEN - 1 >= s0), BF16)
    onehot = jnp.asarray(np.arange(S)[:, None] // SEL_LEN == np.arange(LANES)[None, :], BF16)
    ks_col = (H * dh + KH * dh) // LANES
    kw_col = (H * dh + 2 * KH * dh) // LANES
    qspec = pl.BlockSpec((tq, G * dh), lambda b, h, i: (b * nt + i, h))
    vspec = pl.BlockSpec((1, 1, nt, LANES, tk), lambda b, h, i: (b, h // 2, 0, 0, 0))
    return pl.pallas_call(
        _nsa_attn_t_kernel, out_shape=jax.ShapeDtypeStruct((B * S, H * dh), BF16), grid=(B, KH, nt),
        in_specs=[qspec,
                  pl.BlockSpec((1, n_cmp, dh), lambda b, h, i: (b * KH + h, 0, 0)),
                  pl.BlockSpec((1, dh, n_cmp), lambda b, h, i: (b * KH + h, 0, 0)),
                  pl.BlockSpec((S, LANES), lambda b, h, i: (b, ks_col + h // 2)), vspec,
                  pl.BlockSpec((S, LANES), lambda b, h, i: (b, kw_col + h // 2)), vspec,
                  pl.BlockSpec((S, LANES), lambda b, h, i: (0, 0)),
                  pl.BlockSpec((tq, LANES), lambda b, h, i: (b * nt + i, h)),
                  pl.BlockSpec((n_sel, n_cmp), lambda b, h, i: (0, 0))],
        out_specs=qspec,
        compiler_params=_cparams(("parallel", "parallel", "arbitrary")), name="nsa_attn",
    )(roped, k_cmp, v_cmpt, roped, vst, roped, vwt, onehot, gl, ovl)


def nsa_mixer(xb, w_in, ck_pos, ck_w1, ck_w2, cv_pos, cv_w1, cv_w2, B, S):
    H, KH, G, dh = NSA_HEADS, NSA_KV_HEADS, NSA_GROUP, HEAD_DIM
    kvw = KH * dh
    cuts = np.cumsum([H * dh, kvw, kvw, kvw, kvw, kvw, kvw]).tolist()
    wq, wkc, wvc, wks, wvs, wkw, wvw, wgl = jnp.split(w_in, cuts, axis=1)
    w_rope = jnp.concatenate([wq * dh ** -0.5, wkc, wks, wkw], axis=1).astype(BF16)
    w_val = jnp.concatenate([wvc, wvs, wvw], axis=1).astype(BF16)
    w_gate = jnp.zeros((w_in.shape[0], KH, LANES), F32).at[:, :, :3 * G].set(
        wgl.reshape(-1, KH, G, 3).transpose(0, 1, 3, 2).reshape(-1, KH, 3 * G)).reshape(-1, KH * LANES).astype(BF16)
    roped = matmul(xb, w_rope, out_dtype=BF16, rope=rope_tables(jnp.arange(S)), tn=w_rope.shape[1] // 2)
    vals = matmul(xb, w_val, out_dtype=BF16)
    gl = matmul(xb, w_gate, out_dtype=F32)

    n16 = S // CMP_STRIDE

    def blocks16(t):
        return t.reshape(B, n16, CMP_STRIDE, KH, dh).transpose(0, 3, 1, 2, 4).reshape(B * KH, n16, CMP_STRIDE * dh)

    def vt_tiles(t):
        return t.reshape(B, S // NSA_TK, NSA_TK, KH * dh // LANES, LANES).transpose(0, 3, 1, 4, 2)

    k_cmp = nsa_compress(blocks16(roped[:, H * dh:H * dh + kvw]), ck_pos, ck_w1, ck_w2, False)
    v_cmpt = nsa_compress(blocks16(vals[:, :kvw]), cv_pos, cv_w1, cv_w2, True)
    return nsa_attention_t(roped, vt_tiles(vals[:, kvw:2 * kvw]), vt_tiles(vals[:, 2 * kvw:]), k_cmp, v_cmpt, gl, B, S)


def kernel(x, l0_nsa_w_in, l0_nsa_w_out, l0_nsa_ck_pos, l0_nsa_ck_w1, l0_nsa_ck_w2, l0_nsa_cv_pos, l0_nsa_cv_w1, l0_nsa_cv_w2, l0_ln1_g, l0_ln1_b, l0_router_w, l0_router_b, l0_moe_w_in, l0_moe_w_out, l0_shared_w_in, l0_shared_w_out, l0_ln2_g, l0_ln2_b, l1_gla_w_in, l1_gla_w_a2, l1_gla_b_a, l1_gla_norm_g, l1_gla_w_out, l1_ln1_g, l1_ln1_b, l1_router_w, l1_router_b, l1_moe_w_in, l1_moe_w_out, l1_shared_w_in, l1_shared_w_out, l1_ln2_g, l1_ln2_b, l2_dil_w_in, l2_dil_w_out, l2_ln1_g, l2_ln1_b, l2_router_w, l2_router_b, l2_moe_w_in, l2_moe_w_out, l2_shared_w_in, l2_shared_w_out, l2_ln2_g, l2_ln2_b, l3_nsa_w_in, l3_nsa_w_out, l3_nsa_ck_pos, l3_nsa_ck_w1, l3_nsa_ck_w2, l3_nsa_cv_pos, l3_nsa_cv_w1, l3_nsa_cv_w2, l3_ln1_g, l3_ln1_b, l3_router_w, l3_router_b, l3_moe_w_in, l3_moe_w_out, l3_shared_w_in, l3_shared_w_out, l3_ln2_g, l3_ln2_b):
    B, S, D = x.shape
    xf = x.reshape(B * S, D)
    xb = xf.astype(BF16)

    h = nsa_mixer(xb, l0_nsa_w_in, l0_nsa_ck_pos, l0_nsa_ck_w1, l0_nsa_ck_w2, l0_nsa_cv_pos, l0_nsa_cv_w1, l0_nsa_cv_w2, B, S)
    xf, xb = matmul_res_ln(h, l0_nsa_w_out.astype(BF16), xf, l0_ln1_g, l0_ln1_b)
    xf, xb = moe_layer(xf, xb, l0_router_w, l0_router_b, l0_moe_w_in, l0_moe_w_out, l0_shared_w_in, l0_shared_w_out, l0_ln2_g, l0_ln2_b)

    h = gla_mixer(xb, l1_gla_w_in, l1_gla_w_a2, l1_gla_b_a, l1_gla_norm_g, B, S)
    xf, xb = matmul_res_ln(h, l1_gla_w_out.astype(BF16), xf, l1_ln1_g, l1_ln1_b)
    xf, xb = moe_layer(xf, xb, l1_router_w, l1_router_b, l1_moe_w_in, l1_moe_w_out, l1_shared_w_in, l1_shared_w_out, l1_ln2_g, l1_ln2_b)

    xf, xb = dilated_layer(xf, xb, l2_dil_w_in, l2_dil_w_out, l2_ln1_g, l2_ln1_b, B, S)
    xf, xb = moe_layer(xf, xb, l2_router_w, l2_router_b, l2_moe_w_in, l2_moe_w_out, l2_shared_w_in, l2_shared_w_out, l2_ln2_g, l2_ln2_b)

    h = nsa_mixer(xb, l3_nsa_w_in, l3_nsa_ck_pos, l3_nsa_ck_w1, l3_nsa_ck_w2, l3_nsa_cv_pos, l3_nsa_cv_w1, l3_nsa_cv_w2, B, S)
    xf, xb = matmul_res_ln(h, l3_nsa_w_out.astype(BF16), xf, l3_ln1_g, l3_ln1_b)
    xf, xb = moe_layer(xf, xb, l3_router_w, l3_router_b, l3_moe_w_in, l3_moe_w_out, l3_shared_w_in, l3_shared_w_out, l3_ln2_g, l3_ln2_b)
    return xf.reshape(B, S, D)
```

```python
import functools
import math

import jax
import jax.numpy as jnp
import numpy as np
from jax import lax
from jax.experimental import pallas as pl
from jax.experimental.pallas import tpu as pltpu

F32 = jnp.float32
BF16 = jnp.bfloat16

D_MODEL = 1024
DEPTH = 4
HEAD_DIM = 64
ROPE_THETA = 500000.0
ROPE_DIM = HEAD_DIM // 4
ROPE_HALF = ROPE_DIM // 2

N_EXPERTS = 64
TOP_K = 8
N_GROUPS = 8
TOPK_GROUPS = 4
GROUP_SIZE = N_EXPERTS // N_GROUPS
D_EXPERT = 256
ROUTED_SCALE = 2.5

DN_ALPHA = (2.0 * DEPTH) ** 0.25
LN_EPS = 1e-5
NEG_INF = -1e30

LANES = 128
VMEM_LIMIT = 48 * 1024 * 1024


def _cparams(sem, **kw):
    return pltpu.CompilerParams(dimension_semantics=sem, vmem_limit_bytes=VMEM_LIMIT, **kw)


def _pick(n, pref):
    t = min(pref, n)
    while n % t:
        t //= 2
    return t


def _mm_kernel(x_ref, w_ref, o_ref):
    o_ref[...] = jnp.dot(x_ref[...], w_ref[...], preferred_element_type=F32).astype(o_ref.dtype)


def _mm_rope_kernel(x_ref, w_ref, c_ref, sm_ref, sp_ref, o_ref):
    y = jnp.dot(x_ref[...], w_ref[...], preferred_element_type=F32)
    reps = y.shape[1] // LANES
    c = jnp.tile(c_ref[...], (1, reps))
    sm = jnp.tile(sm_ref[...], (1, reps))
    sp = jnp.tile(sp_ref[...], (1, reps))
    up = pltpu.roll(y, y.shape[1] - ROPE_HALF, 1)
    dn = pltpu.roll(y, ROPE_HALF, 1)
    o_ref[...] = (y * c + up * sm + dn * sp).astype(o_ref.dtype)


def matmul(x, w, out_dtype=F32, rope=None, tm=None, tn=1024):
    M, K = x.shape
    N = w.shape[1]
    tm = _pick(M, tm or (1024 if rope is None else 512))
    tn = _pick(N, tn)
    grid = (N // tn, M // tm)
    x_spec = pl.BlockSpec((tm, K), lambda j, i: (i, 0))
    w_spec = pl.BlockSpec((K, tn), lambda j, i: (0, j))
    o_spec = pl.BlockSpec((tm, tn), lambda j, i: (i, j))
    if rope is None:
        return pl.pallas_call(
            _mm_kernel, out_shape=jax.ShapeDtypeStruct((M, N), out_dtype), grid=grid,
            in_specs=[x_spec, w_spec], out_specs=o_spec,
            compiler_params=_cparams(("parallel", "parallel")), name="mm")(x, w)
    R = rope[0].shape[0]
    tm = _pick(R, tm)
    grid = (N // tn, M // tm)
    x_spec = pl.BlockSpec((tm, K), lambda j, i: (i, 0))
    o_spec = pl.BlockSpec((tm, tn), lambda j, i: (i, j))
    nr = R // tm
    t_spec = pl.BlockSpec((tm, LANES), lambda j, i: (i % nr, 0))
    return pl.pallas_call(
        _mm_rope_kernel, out_shape=jax.ShapeDtypeStruct((M, N), out_dtype), grid=grid,
        in_specs=[x_spec, w_spec, t_spec, t_spec, t_spec], out_specs=o_spec,
        compiler_params=_cparams(("parallel", "parallel")), name="mm_rope")(x, w, *rope)


def rope_tables(pos):
    inv = ROPE_THETA ** (-jnp.arange(ROPE_HALF, dtype=F32) * 2.0 / ROPE_DIM)
    ang = pos.astype(F32)[:, None] * inv[None, :]
    cos, sin = jnp.cos(ang), jnp.sin(ang)
    n = pos.shape[0]
    ones = jnp.ones((n, HEAD_DIM - ROPE_DIM), F32)
    zeros = jnp.zeros((n, HEAD_DIM - ROPE_DIM), F32)
    zh = jnp.zeros((n, ROPE_HALF), F32)
    c = jnp.concatenate([cos, cos, ones], 1)
    sm = jnp.concatenate([-sin, zh, zeros], 1)
    sp = jnp.concatenate([zh, sin, zeros], 1)
    return tuple(jnp.tile(t, (1, LANES // HEAD_DIM)) for t in (c, sm, sp))


def _ln(v, g, b):
    mu = jnp.mean(v, axis=-1, keepdims=True)
    d = v - mu
    var = jnp.mean(d * d, axis=-1, keepdims=True)
    return d * lax.rsqrt(var + LN_EPS) * g + b


def _mm_res_ln_kernel(a_ref, w_ref, x_ref, g_ref, b_ref, o_ref, ob_ref):
    h = jnp.dot(a_ref[...], w_ref[...], preferred_element_type=F32)
    y = _ln(DN_ALPHA * x_ref[...] + h, g_ref[...], b_ref[...])
    o_ref[...] = y
    ob_ref[...] = y.astype(BF16)


def matmul_res_ln(a, w, x, g, b, tm=512):
    M, K = a.shape
    D = w.shape[1]
    tm = _pick(M, tm)
    row = lambda i: (i, 0)
    fix = lambda i: (0, 0)
    return pl.pallas_call(
        _mm_res_ln_kernel,
        out_shape=(jax.ShapeDtypeStruct((M, D), F32), jax.ShapeDtypeStruct((M, D), BF16)),
        grid=(M // tm,),
        in_specs=[pl.BlockSpec((tm, K), row), pl.BlockSpec((K, D), fix), pl.BlockSpec((tm, D), row),
                  pl.BlockSpec((1, D), fix), pl.BlockSpec((1, D), fix)],
        out_specs=(pl.BlockSpec((tm, D), row), pl.BlockSpec((tm, D), row)),
        compiler_params=_cparams(("parallel",)), name="mm_res_ln")(a, w, x, g.reshape(1, D), b.reshape(1, D))


def _first_index_of_max(v, iota, axis, n):
    m = jnp.max(v, axis=axis, keepdims=True)
    idx = jnp.min(jnp.where(v == m, iota, n), axis=axis, keepdims=True)
    return m, idx


def _router_kernel(x_ref, wh_ref, wl_ref, rb_ref, tri_ref, slot_ref, gatew_ref, cnt_ref):
    x = x_ref[...]
    xh = x.astype(BF16)
    xl = (x - xh.astype(F32)).astype(BF16)
    dn = (((1,), (1,)), ((), ()))
    logits = (lax.dot_general(wh_ref[...], xh, dn, preferred_element_type=F32)
              + lax.dot_general(wh_ref[...], xl, dn, preferred_element_type=F32)
              + lax.dot_general(wl_ref[...], xh, dn, preferred_element_type=F32))
    tm = logits.shape[1]
    s = jax.nn.sigmoid(logits)
    sb = s + rb_ref[...]
    sb3 = sb.reshape(N_GROUPS, GROUP_SIZE, tm)
    io3 = lax.broadcasted_iota(jnp.int32, sb3.shape, 1)
    m1, i1 = _first_index_of_max(sb3, io3, 1, GROUP_SIZE)
    m2 = jnp.max(jnp.where(io3 == i1, -jnp.inf, sb3), axis=1, keepdims=True)
    gs = (m1 + m2).reshape(N_GROUPS, tm)
    iog = lax.broadcasted_iota(jnp.int32, gs.shape, 0)
    gmask = jnp.zeros(gs.shape, jnp.bool_)
    for _ in range(TOPK_GROUPS):
        _, gi = _first_index_of_max(gs, iog, 0, N_GROUPS)
        pick = iog == gi
        gmask = gmask | pick
        gs = jnp.where(pick, -jnp.inf, gs)
    emask = jnp.broadcast_to(gmask.reshape(N_GROUPS, 1, tm), sb3.shape).reshape(N_EXPERTS, tm)
    cand = jnp.where(emask, sb, NEG_INF)
    ioe = lax.broadcasted_iota(jnp.int32, cand.shape, 0)
    sel = jnp.zeros(cand.shape, jnp.bool_)
    picked = []
    for _ in range(TOP_K):
        _, ei = _first_index_of_max(cand, ioe, 0, N_EXPERTS)
        pick = ioe == ei
        picked.append(jnp.sum(jnp.where(pick, s, 0.0), axis=0, keepdims=True))
        sel = sel | pick
        cand = jnp.where(pick, -jnp.inf, cand)
    total = picked[0]
    for g in picked[1:]:
        total = total + g
    gatew_ref[...] = jnp.where(sel, s / total * ROUTED_SCALE, 0.0)
    routed = jnp.where(sel, 1.0, 0.0)
    before = jnp.dot(routed.astype(BF16), tri_ref[...], preferred_element_type=F32)
    slot_ref[...] = jnp.where(sel, before, -1.0).astype(jnp.int32)
    cnt_ref[0] = jnp.broadcast_to(jnp.sum(routed, axis=1, keepdims=True), cnt_ref.shape[1:]).astype(jnp.int32)


def moe_router(x, router_w, router_b):
    N, D = x.shape
    tm = MOE_TILE
    assert N % tm == 0
    wt = router_w.T
    wh = wt.astype(BF16)
    wl = (wt - wh.astype(F32)).astype(BF16)
    tri = (jnp.arange(tm)[:, None] < jnp.arange(tm)[None, :]).astype(BF16)
    fix = lambda i: (0, 0)
    col = lambda i: (0, i)
    slot, gatew, cnt = pl.pallas_call(
        _router_kernel,
        out_shape=(jax.ShapeDtypeStruct((N_EXPERTS, N), jnp.int32), jax.ShapeDtypeStruct((N_EXPERTS, N), F32),
                   jax.ShapeDtypeStruct((N // tm, N_EXPERTS, LANES), jnp.int32)),
        grid=(N // tm,),
        in_specs=[pl.BlockSpec((tm, D), lambda i: (i, 0)), pl.BlockSpec((N_EXPERTS, D), fix),
                  pl.BlockSpec((N_EXPERTS, D), fix), pl.BlockSpec((N_EXPERTS, 1), fix), pl.BlockSpec((tm, tm), fix)],
        out_specs=(pl.BlockSpec((N_EXPERTS, tm), col), pl.BlockSpec((N_EXPERTS, tm), col),
                   pl.BlockSpec((1, N_EXPERTS, LANES), lambda i: (i, 0, 0))),
        compiler_params=_cparams(("arbitrary",)), name="moe_router",
    )(x, wh, wl, router_b.reshape(N_EXPERTS, 1).astype(F32), tri)
    return slot, gatew, cnt[:, :, 0]


MOE_TILE = 512
MOE_CAP = 96
MOE_EGROUP = 16
MOE_BLOCK = 512
MOE_ALIGN = 16


def _slot_matrix(slot_rows, first, weights=None):
    n_e, T = slot_rows.shape
    r = lax.broadcasted_iota(jnp.int32, (MOE_CAP, T), 0) + first
    pieces = []
    for e in range(n_e):
        hit = slot_rows[e:e + 1, :] == r
        w = 1.0 if weights is None else weights[e:e + 1, :]
        pieces.append(jnp.where(hit, w, 0.0).astype(BF16))
    return pieces[0] if n_e == 1 else jnp.concatenate(pieces, axis=0)


def _n_windows(n_rows):
    return (n_rows + MOE_CAP - 1) // MOE_CAP


def _dispatch_kernel(base_ref, cnt_ref, zrow_ref, slot_ref, xb_ref, xs_ref, obuf, ovbuf, zbuf, sems, ovsem, zsem):
    t = pl.program_id(0)
    C, EG, E = MOE_CAP, MOE_EGROUP, N_EXPERTS
    n_groups = E // EG
    tz = zbuf.shape[0]

    @pl.when(t == 0)
    def _():
        zbuf[...] = jnp.zeros_like(zbuf)

        def zcopy(j):
            return pltpu.make_async_copy(zbuf, xs_ref.at[pl.ds(pl.multiple_of(zrow_ref[j], tz), tz)], zsem)

        def zstart(j, c):
            @pl.when(zrow_ref[j] >= 0)
            def _():
                zcopy(j).start()
            return c

        def zwait(j, c):
            @pl.when(zrow_ref[j] >= 0)
            def _():
                zcopy(j).wait()
            return c

        lax.fori_loop(0, zrow_ref.shape[0], zstart, 0)
        lax.fori_loop(0, zrow_ref.shape[0], zwait, 0)

    xb = xb_ref[...]

    def window(e, extra):
        return xs_ref.at[pl.ds(pl.multiple_of(base_ref[t * E + e] + extra, MOE_ALIGN), C)]

    def copies(g):
        return [pltpu.make_async_copy(obuf.at[g % 2, pl.ds(e * C, C)], window(g * EG + e, 0), sems.at[g % 2])
                for e in range(EG)]

    for g in range(n_groups):
        if g >= 2:
            for cp in copies(g - 2):
                cp.wait()
        onehot = _slot_matrix(slot_ref[g * EG:(g + 1) * EG, :], 0)
        obuf[g % 2] = jnp.dot(onehot, xb, preferred_element_type=F32).astype(BF16)
        for cp in copies(g):
            cp.start()
    for g in range(max(n_groups - 2, 0), n_groups):
        for cp in copies(g):
            cp.wait()

    def overflow(e, c):
        def chunk(j, c2):
            onehot = _slot_matrix(slot_ref[pl.ds(e, 1), :], j * C)
            ovbuf[...] = jnp.dot(onehot, xb, preferred_element_type=F32).astype(BF16)
            cp = pltpu.make_async_copy(ovbuf, window(e, j * C), ovsem)
            cp.start()
            cp.wait()
            return c2

        return lax.fori_loop(1, _n_windows(cnt_ref[t * E + e]), chunk, c)

    @pl.when(cnt_ref[pl.num_programs(0) * E + t] > C)
    def _():
        lax.fori_loop(0, E, overflow, 0)


def moe_dispatch(xb, slot, base, cnt, zrow, P):
    N, D = xb.shape
    T, C, EG = MOE_TILE, MOE_CAP, MOE_EGROUP
    grid_spec = pltpu.PrefetchScalarGridSpec(
        num_scalar_prefetch=3, grid=(N // T,),
        in_specs=[pl.BlockSpec((N_EXPERTS, T), lambda i, *_: (0, i)),
                  pl.BlockSpec((T, D), lambda i, *_: (i, 0))],
        out_specs=pl.BlockSpec(memory_space=pl.ANY),
        scratch_shapes=[pltpu.VMEM((2, EG * C, D), BF16), pltpu.VMEM((C, D), BF16), pltpu.VMEM((MOE_BLOCK, D), BF16),
                        pltpu.SemaphoreType.DMA((2,)), pltpu.SemaphoreType.DMA(()), pltpu.SemaphoreType.DMA(())])
    return pl.pallas_call(
        _dispatch_kernel, out_shape=jax.ShapeDtypeStruct((P, D), BF16), grid_spec=grid_spec,
        compiler_params=_cparams(("arbitrary",), has_side_effects=True), name="moe_dispatch",
    )(base.reshape(-1), cnt.reshape(-1), zrow, slot, xb)


def _swiglu(x_bf16, w_in, w_out):
    h = jnp.dot(x_bf16, w_in, preferred_element_type=F32)
    f = h.shape[1] // 2
    a = jax.nn.silu(h[:, :f]) * h[:, f:]
    return jnp.dot(a.astype(BF16), w_out, preferred_element_type=F32)


def _expert_kernel(blk_e_ref, used_ref, xs_ref, wi_ref, wo_ref, y_ref):
    del blk_e_ref
    live = pl.program_id(0) < used_ref[0]

    @pl.when(live)
    def _():
        y_ref[...] = _swiglu(xs_ref[...], wi_ref[0], wo_ref[0]).astype(y_ref.dtype)

    @pl.when(jnp.logical_not(live))
    def _():
        y_ref[...] = jnp.zeros_like(y_ref)


def moe_experts(xs, blk_e, n_used, w_in_e, w_out_e):
    P, D = xs.shape
    tm = MOE_BLOCK
    F2 = w_in_e.shape[2]
    spare = P // tm - 1
    rows = lambda i, be, nu: (jnp.where(i < nu[0], i, spare), 0)
    grid_spec = pltpu.PrefetchScalarGridSpec(
        num_scalar_prefetch=2, grid=(P // tm,),
        in_specs=[pl.BlockSpec((tm, D), rows),
                  pl.BlockSpec((1, D, F2), lambda i, be, nu: (be[i], 0, 0)),
                  pl.BlockSpec((1, F2 // 2, D), lambda i, be, nu: (be[i], 0, 0))],
        out_specs=pl.BlockSpec((tm, D), rows))
    return pl.pallas_call(
        _expert_kernel, out_shape=jax.ShapeDtypeStruct((P, D), BF16), grid_spec=grid_spec,
        compiler_params=_cparams(("arbitrary",)), name="moe_experts")(blk_e, n_used, xs, w_in_e, w_out_e)


def _combine_kernel(base_ref, cnt_ref, slot_ref, gw_ref, x_ref, wi_ref, wo_ref, g_ref, b_ref, ys_ref, o_ref, ob_ref,
                    ybuf, ovbuf, acc_ref, sems, ovsem):
    t = pl.program_id(0)
    C, EG, E = MOE_CAP, MOE_EGROUP, N_EXPERTS
    n_groups = E // EG
    rows_in = (((0,), (0,)), ((), ()))

    def window(e, extra):
        return ys_ref.at[pl.ds(pl.multiple_of(base_ref[t * E + e] + extra, MOE_ALIGN), C)]

    def copies(g):
        return [pltpu.make_async_copy(window(g * EG + e, 0), ybuf.at[g % 2, pl.ds(e * C, C)], sems.at[g % 2])
                for e in range(EG)]

    for cp in copies(0):
        cp.start()
    x = x_ref[...]
    acc_ref[...] = _swiglu(x.astype(BF16), wi_ref[...], wo_ref[...])
    for g in range(n_groups):
        if g + 1 < n_groups:
            for cp in copies(g + 1):
                cp.start()
        for cp in copies(g):
            cp.wait()
        spread = _slot_matrix(slot_ref[g * EG:(g + 1) * EG, :], 0, gw_ref[g * EG:(g + 1) * EG, :])
        acc_ref[...] += lax.dot_general(spread, ybuf[g % 2], rows_in, preferred_element_type=F32)

    def overflow(e, c):
        def chunk(j, c2):
            cp = pltpu.make_async_copy(window(e, j * C), ovbuf, ovsem)
            cp.start()
            cp.wait()
            spread = _slot_matrix(slot_ref[pl.ds(e, 1), :], j * C, gw_ref[pl.ds(e, 1), :])
            acc_ref[...] += lax.dot_general(spread, ovbuf[...], rows_in, preferred_element_type=F32)
            return c2

        return lax.fori_loop(1, _n_windows(cnt_ref[t * E + e]), chunk, c)

    @pl.when(cnt_ref[pl.num_programs(0) * E + t] > C)
    def _():
        lax.fori_loop(0, E, overflow, 0)
    y = _ln(DN_ALPHA * x + acc_ref[...], g_ref[...], b_ref[...])
    o_ref[...] = y
    ob_ref[...] = y.astype(BF16)


def moe_combine(x, ys, slot, gatew, base, cnt, w_in_s, w_out_s, g, b):
    N, D = x.shape
    T, C, EG = MOE_TILE, MOE_CAP, MOE_EGROUP
    F2 = w_in_s.shape[1]
    row = lambda i, *_: (i, 0)
    col = lambda i, *_: (0, i)
    fix = lambda i, *_: (0, 0)
    grid_spec = pltpu.PrefetchScalarGridSpec(
        num_scalar_prefetch=2, grid=(N // T,),
        in_specs=[pl.BlockSpec((N_EXPERTS, T), col), pl.BlockSpec((N_EXPERTS, T), col), pl.BlockSpec((T, D), row),
                  pl.BlockSpec((D, F2), fix), pl.BlockSpec((F2 // 2, D), fix),
                  pl.BlockSpec((1, D), fix), pl.BlockSpec((1, D), fix),
                  pl.BlockSpec(memory_space=pl.ANY)],
        out_specs=(pl.BlockSpec((T, D), row), pl.BlockSpec((T, D), row)),
        scratch_shapes=[pltpu.VMEM((2, EG * C, D), BF16), pltpu.VMEM((C, D), BF16), pltpu.VMEM((T, D), F32),
                        pltpu.SemaphoreType.DMA((2,)), pltpu.SemaphoreType.DMA(())])
    return pl.pallas_call(
        _combine_kernel,
        out_shape=(jax.ShapeDtypeStruct((N, D), F32), jax.ShapeDtypeStruct((N, D), BF16)),
        grid_spec=grid_spec,
        compiler_params=_cparams(("arbitrary",)), name="moe_combine",
    )(base.reshape(-1), cnt.reshape(-1), slot, gatew, x, w_in_s, w_out_s, g.reshape(1, D), b.reshape(1, D), ys)


def moe_layer(x, xb, router_w, router_b, w_in_e, w_out_e, w_in_s, w_out_s, ln_g, ln_b):
    N, D = x.shape
    E, C, A, tm_e = N_EXPERTS, MOE_CAP, MOE_ALIGN, MOE_BLOCK
    n_t = N // MOE_TILE
    slot, gatew, cnt = moe_router(x, router_w, router_b)
    seg = (cnt + A - 1) // A * A
    padded = (jnp.sum(seg, axis=0) + C + tm_e - 1) // tm_e * tm_e
    pad_end = jnp.cumsum(padded)
    pad_start = pad_end - padded
    base = (pad_start[None, :] + jnp.cumsum(seg, axis=0) - seg).astype(jnp.int32)
    max_rows = N * TOP_K + n_t * E * (A - 1) + E * (C + tm_e - 1)
    n_blk = -(-max_rows // tm_e) + 1
    blk_row = jnp.arange(n_blk, dtype=jnp.int32) * tm_e
    blk_e = jnp.minimum(jnp.sum(pad_end[None, :] <= blk_row[:, None], axis=1), E - 1).astype(jnp.int32)
    n_used = (pad_end[-1:] // tm_e).astype(jnp.int32)
    zrow = jnp.concatenate([pad_end - tm_e, jnp.where(padded >= 2 * tm_e, pad_end - 2 * tm_e, -1)]).astype(jnp.int32)
    cnt_ext = jnp.concatenate([cnt.reshape(-1), jnp.max(cnt, axis=1)])
    xs = moe_dispatch(xb, slot, base, cnt_ext, zrow, n_blk * tm_e)
    ys = moe_experts(xs, blk_e, n_used, w_in_e.astype(BF16), w_out_e.astype(BF16))
    return moe_combine(x, ys, slot, gatew, base, cnt_ext, w_in_s.astype(BF16), w_out_s.astype(BF16), ln_g, ln_b)


GLA_HEADS = 4
GLA_DK = D_MODEL // 2 // GLA_HEADS
GLA_DV = D_MODEL // GLA_HEADS
GLA_GATE_RANK = 16
GLA_TAU = 16.0
GLA_CHUNK = 64


def _split3(v):
    h1 = v.astype(BF16)
    r1 = v - h1.astype(F32)
    h2 = r1.astype(BF16)
    h3 = (r1 - h2.astype(F32)).astype(BF16)
    return h1, h2, h3


def _gla_kernel(qkvr_ref, a_ref, wah_ref, wal_ref, ba_ref, ng_ref, tri_ref, o_ref, st_ref):
    H, dk, dv, C = GLA_HEADS, GLA_DK, GLA_DV, GLA_CHUNK

    @pl.when(pl.program_id(1) == 0)
    def _():
        st_ref[...] = jnp.zeros_like(st_ref)

    T = qkvr_ref.shape[0]
    a = a_ref[...]
    ah = a.astype(BF16)
    al = (a - ah.astype(F32)).astype(BF16)
    glog = (jnp.dot(ah, wah_ref[...], preferred_element_type=F32)
            + jnp.dot(al, wah_ref[...], preferred_element_type=F32)
            + jnp.dot(ah, wal_ref[...], preferred_element_type=F32)) + ba_ref[...]
    log_a = jax.nn.log_sigmoid(glog) / GLA_TAU
    tri = tri_ref[...]
    rr = lax.broadcasted_iota(jnp.int32, (C, C), 0)
    cc = lax.broadcasted_iota(jnp.int32, (C, C), 1)
    causal = rr >= cc
    ng = ng_ref[...]
    ct = (((1,), (1,)), ((), ()))
    c0 = (((0,), (0,)), ((), ()))
    for c in range(T // C):
        rows = slice(c * C, (c + 1) * C)
        for h in range(H):
            la = log_a[rows, h * dk:(h + 1) * dk]
            p1, p2, p3 = _split3(la)
            b = (jnp.dot(tri, p1, preferred_element_type=F32) + jnp.dot(tri, p2, preferred_element_type=F32)
                 + jnp.dot(tri, p3, preferred_element_type=F32))
            b_last = b[C - 1:C, :]
            q = qkvr_ref[rows, h * dk:(h + 1) * dk].astype(F32)
            k = qkvr_ref[rows, H * dk + h * dk:H * dk + (h + 1) * dk].astype(F32)
            v = qkvr_ref[rows, 2 * H * dk + h * dv:2 * H * dk + (h + 1) * dv]
            r = qkvr_ref[rows, 2 * H * dk + H * dv + h * dv:2 * H * dk + H * dv + (h + 1) * dv].astype(F32)
            qg = (q * jnp.exp(b)).astype(BF16)
            kg = (k * jnp.exp(-b)).astype(BF16)
            kd = (k * jnp.exp(b_last - b)).astype(BF16)
            att = jnp.where(causal, lax.dot_general(qg, kg, ct, preferred_element_type=F32), 0.0)
            st = st_ref[h]
            o = (jnp.dot(att.astype(BF16), v, preferred_element_type=F32)
                 + lax.dot_general(qg, st.astype(BF16), ct, preferred_element_type=F32))
            st_ref[h] = jnp.exp(b_last) * st + lax.dot_general(v, kd, c0, preferred_element_type=F32)
            o = o * lax.rsqrt(jnp.mean(o * o, axis=-1, keepdims=True) + LN_EPS) * ng
            o_ref[rows, h * dv:(h + 1) * dv] = (o * jax.nn.silu(r)).astype(o_ref.dtype)


def gla_core(qkvr, a, w_a2, b_a, norm_g, B, S, tile=256):
    H, dk, dv, C = GLA_HEADS, GLA_DK, GLA_DV, GLA_CHUNK
    N, W = qkvr.shape
    tile = _pick(S, tile)
    nt = S // tile
    wa = jnp.zeros((LANES, H * dk), F32).at[:GLA_GATE_RANK].set(w_a2)
    wah = wa.astype(BF16)
    wal = (wa - wah.astype(F32)).astype(BF16)
    tri = (jnp.arange(C)[:, None] >= jnp.arange(C)[None, :]).astype(BF16)
    row = lambda b, t: (b * nt + t, 0)
    fix = lambda b, t: (0, 0)
    return pl.pallas_call(
        _gla_kernel, out_shape=jax.ShapeDtypeStruct((N, H * dv), BF16), grid=(B, nt),
        in_specs=[pl.BlockSpec((tile, W), row), pl.BlockSpec((tile, LANES), row),
                  pl.BlockSpec((LANES, H * dk), fix), pl.BlockSpec((LANES, H * dk), fix),
                  pl.BlockSpec((1, H * dk), fix), pl.BlockSpec((1, dv), fix), pl.BlockSpec((C, C), fix)],
        out_specs=pl.BlockSpec((tile, H * dv), row),
        scratch_shapes=[pltpu.VMEM((H, dv, dk), F32)],
        compiler_params=_cparams(("parallel", "arbitrary")), name="gla_core",
    )(qkvr, a, wah, wal, b_a.reshape(1, H * dk), norm_g.reshape(1, dv), tri)


def gla_mixer(xb, w_in, w_a2, b_a, norm_g, B, S):
    H, dk, dv = GLA_HEADS, GLA_DK, GLA_DV
    hk, hv = H * dk, H * dv
    wq, wk, wv, wa, wr = jnp.split(w_in, [hk, 2 * hk, 2 * hk + hv, 2 * hk + hv + GLA_GATE_RANK], axis=1)
    w_main = jnp.concatenate([wq * dk ** -0.5, wk, wv, wr], axis=1).astype(BF16)
    w_gate = jnp.zeros((w_in.shape[0], LANES), F32).at[:, :GLA_GATE_RANK].set(wa).astype(BF16)
    qkvr = matmul(xb, w_main, out_dtype=BF16)
    a = matmul(xb, w_gate, out_dtype=F32)
    return gla_core(qkvr, a, w_a2, b_a, norm_g, B, S)


DIL_HEADS = D_MODEL // HEAD_DIM
DIL_CONFIGS = ((128, 1), (512, 4), (2048, 16))
DIL_TQ = 128


def _dil_attn_kernel(q_ref, kc_ref, kp_ref, vc_ref, vp_ref, o_ref, lse_ref, *, tiles_per_seq, window):
    i = pl.program_id(0)
    tq = q_ref.shape[0]
    first = (i % tiles_per_seq) == 0
    qpos = lax.broadcasted_iota(jnp.int32, (tq, 2 * tq), 0) + tq
    kpos = lax.broadcasted_iota(jnp.int32, (tq, 2 * tq), 1)
    dist = qpos - kpos
    mask = (dist >= 0) & (dist <= window) & ((kpos >= tq) | jnp.logical_not(first))
    lane = lax.broadcasted_iota(jnp.int32, (tq, LANES), 1)
    lo = lane < HEAD_DIM
    ct = (((1,), (1,)), ((), ()))
    for p in range(q_ref.shape[1] // LANES):
        cols = slice(p * LANES, (p + 1) * LANES)
        q2 = q_ref[:, cols]
        k2 = jnp.concatenate([kp_ref[:, cols], kc_ref[:, cols]], axis=0)
        v2 = jnp.concatenate([vp_ref[:, cols], vc_ref[:, cols]], axis=0)
        outs, lses = [], []
        for half in (lo, jnp.logical_not(lo)):
            qm = jnp.where(half, q2, jnp.zeros_like(q2))
            s = jnp.where(mask, lax.dot_general(qm, k2, ct, preferred_element_type=F32), NEG_INF)
            m = jnp.max(s, axis=-1, keepdims=True)
            e = jnp.where(mask, jnp.exp(s - m), 0.0)
            den = jnp.sum(e, axis=-1, keepdims=True)
            pv = jnp.dot(e.astype(BF16), v2, preferred_element_type=F32)
            outs.append(pv / jnp.maximum(den, 1e-30))
            lses.append(m + jnp.log(den))
        o_ref[:, cols] = jnp.where(lo, outs[0], outs[1]).astype(o_ref.dtype)
        lse_ref[:, cols] = jnp.where(lo, lses[0], lses[1])


def dil_attention(qk, v, seq_len, window):
    N, W = v.shape
    tq = DIL_TQ
    assert window == tq and seq_len % tq == 0
    nwb = 1
    cur = lambda i: (i, 0)
    prev = lambda i: (jnp.maximum(i - 1, 0), 0)
    return pl.pallas_call(
        functools.partial(_dil_attn_kernel, tiles_per_seq=seq_len // tq, window=window),
        out_shape=(jax.ShapeDtypeStruct((N, W), BF16), jax.ShapeDtypeStruct((N, W), F32)),
        grid=(N // tq,),
        in_specs=[pl.BlockSpec((tq, W), cur),
                  pl.BlockSpec((tq, W), lambda i: (i, nwb)),
                  pl.BlockSpec((tq, W), lambda i: (jnp.maximum(i - 1, 0), nwb)),
                  pl.BlockSpec((tq, W), cur), pl.BlockSpec((tq, W), prev)],
        out_specs=(pl.BlockSpec((tq, W), cur), pl.BlockSpec((tq, W), cur)),
        compiler_params=_cparams(("parallel",)), name="dil_attn",
    )(qk, qk, qk, v, v)


def _dil_merge_kernel(o0, o1, o2, l0, l1, l2, w_ref, x_ref, g_ref, b_ref, out_ref, outb_ref):
    la, lb, lc = l0[...], l1[...], l2[...]
    m = jnp.maximum(jnp.maximum(la, lb), lc)
    ea, eb, ec = jnp.exp(la - m), jnp.exp(lb - m), jnp.exp(lc - m)
    tot = ea + eb + ec
    o = (ea / tot) * o0[...].astype(F32) + (eb / tot) * o1[...].astype(F32) + (ec / tot) * o2[...].astype(F32)
    h = jnp.dot(o.astype(BF16), w_ref[...], preferred_element_type=F32)
    y = _ln(DN_ALPHA * x_ref[...] + h, g_ref[...], b_ref[...])
    out_ref[...] = y
    outb_ref[...] = y.astype(BF16)


def dil_merge_out(os_, lses, w_out, x, g, b, tm=256):
    N, D = x.shape
    W = w_out.shape[0]
    tm = _pick(N, tm)
    row = lambda i: (i, 0)
    fix = lambda i: (0, 0)
    rs = pl.BlockSpec((tm, W), row)
    return pl.pallas_call(
        _dil_merge_kernel,
        out_shape=(jax.ShapeDtypeStruct((N, D), F32), jax.ShapeDtypeStruct((N, D), BF16)),
        grid=(N // tm,),
        in_specs=[rs, rs, rs, rs, rs, rs, pl.BlockSpec((W, D), fix), pl.BlockSpec((tm, D), row),
                  pl.BlockSpec((1, D), fix), pl.BlockSpec((1, D), fix)],
        out_specs=(pl.BlockSpec((tm, D), row), pl.BlockSpec((tm, D), row)),
        compiler_params=_cparams(("parallel",)), name="dil_merge",
    )(*os_, *lses, w_out, x, g.reshape(1, D), b.reshape(1, D))


def dilated_layer(x, xb, w_in, w_out, ln_g, ln_b, B, S):
    N, D = x.shape
    H, dh = DIL_HEADS, HEAD_DIM
    W = H * dh
    w6 = w_in.reshape(D, len(DIL_CONFIGS), 3, W)
    pos = jnp.arange(S)
    os_, lses = [], []
    for gi, (window, dil) in enumerate(DIL_CONFIGS):
        L = S // dil
        wqk = jnp.concatenate([w6[:, gi, 0] * dh ** -0.5, w6[:, gi, 1]], axis=1).astype(BF16)
        wv = w6[:, gi, 2].astype(BF16)
        xp = xb.reshape(B, L, dil, D).transpose(0, 2, 1, 3).reshape(N, D) if dil > 1 else xb
        ppos = pos.reshape(L, dil).T.reshape(S)
        qk = matmul(xp, wqk, out_dtype=BF16, rope=rope_tables(ppos))
        v = matmul(xp, wv, out_dtype=BF16)
        o, lse = dil_attention(qk, v, L, window // dil)
        if dil > 1:
            o = o.reshape(B, dil, L, W).transpose(0, 2, 1, 3).reshape(N, W)
            lse = lse.reshape(B, dil, L, W).transpose(0, 2, 1, 3).reshape(N, W)
        os_.append(o)
        lses.append(lse)
    return dil_merge_out(os_, lses, w_out.astype(BF16), x, ln_g, ln_b)


NSA_HEADS = D_MODEL // HEAD_DIM
NSA_KV_HEADS = 4
NSA_GROUP = NSA_HEADS // NSA_KV_HEADS
CMP_LEN = 32
CMP_STRIDE = 16
CMP_HIDDEN = 256
SEL_LEN = 64
SEL_TOPN = 16
NSA_WINDOW = 512
FORCE_BONUS = 100.0
NSA_TQ = 128
NSA_TK = 128
NSA_CHUNK = 1024


def _compress_kernel(x_ref, p_ref, w1a_ref, w1b_ref, w2_ref, o_ref, *, transpose_out):
    x = x_ref[0].astype(F32)
    n = x.shape[0]
    first = jnp.dot((x + p_ref[0:1, :]).astype(BF16), w1a_ref[...], preferred_element_type=F32)
    second = jnp.dot((x + p_ref[1:2, :]).astype(BF16), w1b_ref[...], preferred_element_type=F32)
    hid = first + pltpu.roll(second, n - 1, 0)
    out = jnp.dot(jax.nn.gelu(hid).astype(BF16), w2_ref[...], preferred_element_type=F32)
    if transpose_out:
        o_ref[0] = out.T[:HEAD_DIM, :].astype(o_ref.dtype)
    else:
        o_ref[0] = out[:, :HEAD_DIM].astype(o_ref.dtype)


def nsa_compress(t, pos_emb, w1, w2, transpose_out):
    BK, n, W = t.shape
    half = CMP_STRIDE * HEAD_DIM
    p = pos_emb.reshape(2, half).astype(F32)
    w2p = jnp.zeros((CMP_HIDDEN, LANES), F32).at[:, :HEAD_DIM].set(w2).astype(BF16)
    oshape = (BK, HEAD_DIM, n) if transpose_out else (BK, n, HEAD_DIM)
    fix = lambda i: (0, 0)
    return pl.pallas_call(
        functools.partial(_compress_kernel, transpose_out=transpose_out),
        out_shape=jax.ShapeDtypeStruct(oshape, BF16), grid=(BK,),
        in_specs=[pl.BlockSpec((1, n, W), lambda i: (i, 0, 0)), pl.BlockSpec((2, half), fix),
                  pl.BlockSpec((half, CMP_HIDDEN), fix), pl.BlockSpec((half, CMP_HIDDEN), fix),
                  pl.BlockSpec((CMP_HIDDEN, LANES), fix)],
        out_specs=pl.BlockSpec((1,) + oshape[1:], lambda i: (i, 0, 0)),
        compiler_params=_cparams(("parallel",)), name="nsa_compress",
    )(t, p, w1[:half].astype(BF16), w1[half:].astype(BF16), w2p)


def _col_softmax_step(carry, s, valid, vt):
    m, l, acc = carry
    m_new = jnp.maximum(m, jnp.max(s, axis=0, keepdims=True))
    alpha = jnp.exp(m - m_new)
    p = jnp.exp(s - m_new)
    if valid is not None:
        p = jnp.where(valid, p, 0.0)
    l = alpha * l + jnp.sum(p, axis=0, keepdims=True)
    acc = alpha * acc + jnp.dot(vt, p.astype(BF16), preferred_element_type=F32)
    return m_new, l, acc


def _nsa_attn_t_kernel(q_ref, kc_ref, vct_ref, ks_ref, vst_ref, kw_ref, vwt_ref, oh_ref, gl_ref, ovl_ref, o_ref):
    kh = pl.program_id(1)
    i = pl.program_id(2)
    tq, dh, G = NSA_TQ, HEAD_DIM, NSA_GROUP
    tk = NSA_TK
    M = G * tq
    t0 = i * tq
    slot = kh % 2
    vrows = pl.ds(pl.multiple_of(slot * dh, dh), dh)

    def tpos(shape):
        return t0 + (lax.broadcasted_iota(jnp.int32, shape, 1) & (tq - 1))

    qn = q_ref[...].astype(F32)
    qt_pairs = [qn[:, c * LANES:(c + 1) * LANES].T for c in range(G * dh // LANES)]
    qt = jnp.concatenate([p[h * dh:(h + 1) * dh] for p in qt_pairs for h in range(LANES // dh)], axis=1)
    qt = qt.astype(BF16)

    n_cmp = kc_ref.shape[1]
    s_c = jnp.dot(kc_ref[0], qt, preferred_element_type=F32)
    cend = lax.broadcasted_iota(jnp.int32, (n_cmp, M), 0) * CMP_STRIDE + (CMP_LEN - 1)
    vis = cend <= tpos((n_cmp, M))
    s_c = jnp.where(vis, s_c, NEG_INF)
    e_c = jnp.where(vis, jnp.exp(s_c - jnp.max(s_c, axis=0, keepdims=True)), 0.0)
    p_c = e_c / jnp.maximum(jnp.sum(e_c, axis=0, keepdims=True), 1e-30)
    o_c = jnp.dot(vct_ref[0], p_c.astype(BF16), preferred_element_type=F32)

    psum = p_c[:, 0:tq]
    for g in range(1, G):
        psum = psum + p_c[:, g * tq:(g + 1) * tq]
    ovl = ovl_ref[...]
    imp = sum(jnp.dot(ovl, piece, preferred_element_type=F32) for piece in _split3(psum))
    n_sel = imp.shape[0]
    blk = lax.broadcasted_iota(jnp.int32, (n_sel, tq), 0)
    cur = tpos((n_sel, tq)) // SEL_LEN
    forced = (blk == 0) | (blk == cur) | (blk == cur - 1)
    score = jnp.where(blk <= cur, imp + jnp.where(forced, FORCE_BONUS, 0.0), -1.0)
    chosen = jnp.zeros((n_sel, tq), jnp.bool_)
    for _ in range(min(SEL_TOPN, n_sel)):
        mx, idx = _first_index_of_max(score, blk, 0, n_sel)
        hit = blk == idx
        chosen = chosen | (hit & (mx >= 0.0))
        score = jnp.where(hit, -2.0, score)
    bias = jnp.where(chosen, 0.0, NEG_INF).astype(BF16)
    zero = jnp.zeros_like(qt)
    q_pair = jnp.concatenate([jnp.where(slot == 0, qt, zero), jnp.where(slot == 1, qt, zero)], axis=0)
    pad = jnp.zeros((LANES - n_sel, M), BF16)
    q_aug = jnp.concatenate([q_pair, jnp.concatenate([bias] * G, axis=1), pad], axis=0)

    init = (jnp.full((1, M), NEG_INF, F32), jnp.zeros((1, M), F32), jnp.zeros((dh, M), F32))

    def vt_cat(ref, first_tile, n):
        return jnp.concatenate([ref[0, 0, first_tile + j, vrows, :] for j in range(n)], axis=1)

    ch = NSA_CHUNK
    per_chunk = ch // tk

    def sel_rows(start, n, carry, causal):
        rows = slice(start, start + n)
        k_aug = jnp.concatenate([ks_ref[rows, :], oh_ref[rows, :]], axis=1)
        s = jnp.dot(k_aug, q_aug, preferred_element_type=F32)
        if causal:
            kpos = start + lax.broadcasted_iota(jnp.int32, (n, M), 0)
            s = jnp.where(kpos <= tpos((n, M)), s, NEG_INF)
        return _col_softmax_step(carry, s, None, vt_cat(vst_ref, start // tk, n // tk))

    def sel_upto(n_full):
        def run():
            carry = init
            for c in range(n_full):
                carry = sel_rows(c * ch, ch, carry, False)
            _, l, acc = sel_rows(n_full * ch, ch, carry, True)
            return l, acc
        return run

    l_s, acc_s = lax.switch(t0 // ch, [sel_upto(n) for n in range(ks_ref.shape[0] // ch)])
    o_s = acc_s / jnp.maximum(l_s, 1e-30)

    n_wt = NSA_WINDOW // tk + 1
    wt0 = jnp.maximum(i + 1 - n_wt, 0)
    wrows = pl.ds(pl.multiple_of(wt0 * tk, tk), n_wt * tk)
    s_w = jnp.dot(kw_ref[wrows, :], q_pair, preferred_element_type=F32)
    dist = tpos(s_w.shape) - (wt0 * tk + lax.broadcasted_iota(jnp.int32, s_w.shape, 0))
    near = (dist >= 0) & (dist < NSA_WINDOW)
    s_w = jnp.where(near, s_w, NEG_INF)
    e_w = jnp.exp(s_w - jnp.max(s_w, axis=0, keepdims=True))
    l_w = jnp.sum(e_w, axis=0, keepdims=True)
    o_w = jnp.dot(vt_cat(vwt_ref, wt0, n_wt), e_w.astype(BF16), preferred_element_type=F32) / jnp.maximum(l_w, 1e-30)

    gates = jax.nn.sigmoid(gl_ref[...].T)
    outs = []
    for g in range(G):
        cols = slice(g * tq, (g + 1) * tq)
        outs.append(gates[g:g + 1] * o_c[:, cols] + gates[G + g:G + g + 1] * o_s[:, cols]
                    + gates[2 * G + g:2 * G + g + 1] * o_w[:, cols])
    per = LANES // dh
    o_ref[...] = jnp.concatenate(
        [jnp.concatenate(outs[c * per:(c + 1) * per], axis=0).T for c in range(G // per)], axis=1).astype(o_ref.dtype)


def nsa_attention_t(roped, vst, vwt, k_cmp, v_cmpt, gl, B, S):
    H, KH, G, dh = NSA_HEADS, NSA_KV_HEADS, NSA_GROUP, HEAD_DIM
    tq, tk = NSA_TQ, NSA_TK
    assert tq == tk and S % NSA_CHUNK == 0 and NSA_CHUNK % tk == 0 and S >= NSA_WINDOW + tq and G * dh == 2 * LANES
    nt = S // tq
    n_cmp = k_cmp.shape[1]
    n_sel = S // SEL_LEN
    c0 = np.arange(n_cmp)[None, :] * CMP_STRIDE
    s0 = np.arange(n_sel)[:, None] * SEL_LEN
    ovl = jnp.asarray((c0 < s0 + SEL_LEN) & (c0 + CMP_LEN - 1 >= s0), BF16)
    onehot = jnp.asarray(np.arange(S)[:, None] // SEL_LEN == np.arange(LANES)[None, :], BF16)
    ks_col = (H * dh + KH * dh) // LANES
    kw_col = (H * dh + 2 * KH * dh) // LANES
    qspec = pl.BlockSpec((tq, G * dh), lambda b, h, i: (b * nt + i, h))
    vspec = pl.BlockSpec((1, 1, nt, LANES, tk), lambda b, h, i: (b, h // 2, 0, 0, 0))
    return pl.pallas_call(
        _nsa_attn_t_kernel, out_shape=jax.ShapeDtypeStruct((B * S, H * dh), BF16), grid=(B, KH, nt),
        in_specs=[qspec,
                  pl.BlockSpec((1, n_cmp, dh), lambda b, h, i: (b * KH + h, 0, 0)),
                  pl.BlockSpec((1, dh, n_cmp), lambda b, h, i: (b * KH + h, 0, 0)),
                  pl.BlockSpec((S, LANES), lambda b, h, i: (b, ks_col + h // 2)), vspec,
                  pl.BlockSpec((S, LANES), lambda b, h, i: (b, kw_col + h // 2)), vspec,
                  pl.BlockSpec((S, LANES), lambda b, h, i: (0, 0)),
                  pl.BlockSpec((tq, LANES), lambda b, h, i: (b * nt + i, h)),
                  pl.BlockSpec((n_sel, n_cmp), lambda b, h, i: (0, 0))],
        out_specs=qspec,
        compiler_params=_cparams(("parallel", "parallel", "arbitrary")), name="nsa_attn",
    )(roped, k_cmp, v_cmpt, roped, vst, roped, vwt, onehot, gl, ovl)


def nsa_mixer(xb, w_in, ck_pos, ck_w1, ck_w2, cv_pos, cv_w1, cv_w2, B, S):
    H, KH, G, dh = NSA_HEADS, NSA_KV_HEADS, NSA_GROUP, HEAD_DIM
    kvw = KH * dh
    cuts = np.cumsum([H * dh, kvw, kvw, kvw, kvw, kvw, kvw]).tolist()
    wq, wkc, wvc, wks, wvs, wkw, wvw, wgl = jnp.split(w_in, cuts, axis=1)
    w_rope = jnp.concatenate([wq * dh ** -0.5, wkc, wks, wkw], axis=1).astype(BF16)
    w_val = jnp.concatenate([wvc, wvs, wvw], axis=1).astype(BF16)
    w_gate = jnp.zeros((w_in.shape[0], KH, LANES), F32).at[:, :, :3 * G].set(
        wgl.reshape(-1, KH, G, 3).transpose(0, 1, 3, 2).reshape(-1, KH, 3 * G)).reshape(-1, KH * LANES).astype(BF16)
    roped = matmul(xb, w_rope, out_dtype=BF16, rope=rope_tables(jnp.arange(S)), tn=w_rope.shape[1] // 2)
    vals = matmul(xb, w_val, out_dtype=BF16)
    gl = matmul(xb, w_gate, out_dtype=F32)

    n16 = S // CMP_STRIDE

    def blocks16(t):
        return t.reshape(B, n16, CMP_STRIDE, KH, dh).transpose(0, 3, 1, 2, 4).reshape(B * KH, n16, CMP_STRIDE * dh)

    def vt_tiles(t):
        return t.reshape(B, S // NSA_TK, NSA_TK, KH * dh // LANES, LANES).transpose(0, 3, 1, 4, 2)

    k_cmp = nsa_compress(blocks16(roped[:, H * dh:H * dh + kvw]), ck_pos, ck_w1, ck_w2, False)
    v_cmpt = nsa_compress(blocks16(vals[:, :kvw]), cv_pos, cv_w1, cv_w2, True)
    return nsa_attention_t(roped, vt_tiles(vals[:, kvw:2 * kvw]), vt_tiles(vals[:, 2 * kvw:]), k_cmp, v_cmpt, gl, B, S)


def kernel(x, l0_nsa_w_in, l0_nsa_w_out, l0_nsa_ck_pos, l0_nsa_ck_w1, l0_nsa_ck_w2, l0_nsa_cv_pos, l0_nsa_cv_w1, l0_nsa_cv_w2, l0_ln1_g, l0_ln1_b, l0_router_w, l0_router_b, l0_moe_w_in, l0_moe_w_out, l0_shared_w_in, l0_shared_w_out, l0_ln2_g, l0_ln2_b, l1_gla_w_in, l1_gla_w_a2, l1_gla_b_a, l1_gla_norm_g, l1_gla_w_out, l1_ln1_g, l1_ln1_b, l1_router_w, l1_router_b, l1_moe_w_in, l1_moe_w_out, l1_shared_w_in, l1_shared_w_out, l1_ln2_g, l1_ln2_b, l2_dil_w_in, l2_dil_w_out, l2_ln1_g, l2_ln1_b, l2_router_w, l2_router_b, l2_moe_w_in, l2_moe_w_out, l2_shared_w_in, l2_shared_w_out, l2_ln2_g, l2_ln2_b, l3_nsa_w_in, l3_nsa_w_out, l3_nsa_ck_pos, l3_nsa_ck_w1, l3_nsa_ck_w2, l3_nsa_cv_pos, l3_nsa_cv_w1, l3_nsa_cv_w2, l3_ln1_g, l3_ln1_b, l3_router_w, l3_router_b, l3_moe_w_in, l3_moe_w_out, l3_shared_w_in, l3_shared_w_out, l3_ln2_g, l3_ln2_b):
    B, S, D = x.shape
    xf = x.reshape(B * S, D)
    xb = xf.astype(BF16)

    h = nsa_mixer(xb, l0_nsa_w_in, l0_nsa_ck_pos, l0_nsa_ck_w1, l0_nsa_ck_w2, l0_nsa_cv_pos, l0_nsa_cv_w1, l0_nsa_cv_w2, B, S)
    xf, xb = matmul_res_ln(h, l0_nsa_w_out.astype(BF16), xf, l0_ln1_g, l0_ln1_b)
    xf, xb = moe_layer(xf, xb, l0_router_w, l0_router_b, l0_moe_w_in, l0_moe_w_out, l0_shared_w_in, l0_shared_w_out, l0_ln2_g, l0_ln2_b)

    h = gla_mixer(xb, l1_gla_w_in, l1_gla_w_a2, l1_gla_b_a, l1_gla_norm_g, B, S)
    xf, xb = matmul_res_ln(h, l1_gla_w_out.astype(BF16), xf, l1_ln1_g, l1_ln1_b)
    xf, xb = moe_layer(xf, xb, l1_router_w, l1_router_b, l1_moe_w_in, l1_moe_w_out, l1_shared_w_in, l1_shared_w_out, l1_ln2_g, l1_ln2_b)

    xf, xb = dilated_layer(xf, xb, l2_dil_w_in, l2_dil_w_out, l2_ln1_g, l2_ln1_b, B, S)
    xf, xb = moe_layer(xf, xb, l2_router_w, l2_router_b, l2_moe_w_in, l2_moe_w_out, l2_shared_w_in, l2_shared_w_out, l2_ln2_g, l2_ln2_b)

    h = nsa_mixer(xb, l3_nsa_w_in, l3_nsa_ck_pos, l3_nsa_ck_w1, l3_nsa_ck_w2, l3_nsa_cv_pos, l3_nsa_cv_w1, l3_nsa_cv_w2, B, S)
    xf, xb = matmul_res_ln(h, l3_nsa_w_out.astype(BF16), xf, l3_ln1_g, l3_ln1_b)
    xf, xb = moe_layer(xf, xb, l3_router_w, l3_router_b, l3_moe_w_in, l3_moe_w_out, l3_shared_w_in, l3_shared_w_out, l3_ln2_g, l3_ln2_b)
    return xf.reshape(B, S, D)
```
